```python
import jax, jax.numpy as jnp
from jax import lax
import numpy as np

D_MODEL = 1024
BATCH = 8
SEQ = 4096
DEPTH = 2
DEC_BATCH = 32
DEC_SEQ = 16
PAST_LEN = 2048

CHUNK = 64
A_HEADS = 8
A_HEAD_DIM = 64
A_WIDTH = A_HEADS * A_HEAD_DIM
A_BAND_CHUNKS = 8
A_WINDOW = A_BAND_CHUNKS * CHUNK
A_BAND = A_WINDOW + CHUNK
MAX_REL = 256
B_HEADS = 4
B_HEAD_DIM = 64
B_WIDTH = B_HEADS * B_HEAD_DIM
CONV_W = 4
C_WIDTH = D_MODEL - A_WIDTH - B_WIDTH
C_GROUP = 16
C_GROUPS = C_WIDTH // C_GROUP
C_STATE = 64
D_FF = 4 * D_MODEL
IN_SIZES = (A_WIDTH, A_WIDTH, A_WIDTH, 3 * B_WIDTH, B_HEADS, B_HEADS, B_WIDTH, C_WIDTH)
IN_WIDTH = 3 * A_WIDTH + 4 * B_WIDTH + 2 * B_HEADS + C_WIDTH
DN_ALPHA = (2.0 * DEPTH) ** 0.25
DN_BETA = (8.0 * DEPTH) ** -0.25
LN_EPS = 1e-5
RMS_EPS = 1e-6
NEG_INF = -1e30

kernel_name = 'hymba_streaming_encoder_step'


def split_points(sizes):
    pts, acc = [], 0
    for s in sizes[:-1]:
        acc += s
        pts.append(acc)
    return pts


def layer_norm(x, g, b):
    xf = x.astype(jnp.float32)
    mu = jnp.mean(xf, -1, keepdims=True)
    xc = xf - mu
    var = jnp.mean(xc * xc, -1, keepdims=True)
    y = xc * lax.rsqrt(var + LN_EPS) * g.astype(jnp.float32) + b.astype(jnp.float32)
    return y.astype(x.dtype)


def l2_normalize(x):
    return x * lax.rsqrt(jnp.sum(x * x, -1, keepdims=True) + RMS_EPS)


def rel_bias(table, rel):
    return jnp.take(table, jnp.clip(rel, -MAX_REL, MAX_REL) + MAX_REL, axis=1).astype(jnp.float32)


def band_attention_prompt(q, k, v, table):
    bsz, t_len, n_h, d_h = q.shape
    n_chunks = t_len // CHUNK
    pad = ((0, 0), (A_WINDOW, 0), (0, 0), (0, 0))
    kp, vp = jnp.pad(k, pad), jnp.pad(v, pad)
    rel = A_WINDOW + jnp.arange(CHUNK)[:, None] - jnp.arange(A_BAND)[None, :]
    bias = rel_bias(table, rel)
    scale = d_h ** -0.5

    def one_chunk(c):
        start = c * CHUNK
        qc = lax.dynamic_slice_in_dim(q, start, CHUNK, axis=1)
        kc = lax.dynamic_slice_in_dim(kp, start, A_BAND, axis=1)
        vc = lax.dynamic_slice_in_dim(vp, start, A_BAND, axis=1)
        s = jnp.einsum('bqhd,bkhd->bhqk', qc, kc).astype(jnp.float32) * scale + bias
        valid = start - A_WINDOW + jnp.arange(A_BAND) >= 0
        s = jnp.where(valid, s, NEG_INF)
        p = jax.nn.softmax(s, axis=-1).astype(v.dtype)
        return jnp.einsum('bhqk,bkhd->bqhd', p, vc)

    out = lax.map(one_chunk, jnp.arange(n_chunks))
    return jnp.moveaxis(out, 0, 1).reshape(bsz, t_len, n_h * d_h)


def band_attention_sample(q, k, v, k_cache, v_cache, table):
    bsz, s_len, n_h, d_h = q.shape
    n_cache = k_cache.shape[1]
    kk = jnp.concatenate([k_cache.astype(k.dtype), k], axis=1)
    vv = jnp.concatenate([v_cache.astype(v.dtype), v], axis=1)
    rel = n_cache + jnp.arange(s_len)[:, None] - jnp.arange(n_cache + s_len)[None, :]
    s = jnp.einsum('bqhd,bkhd->bhqk', q, kk).astype(jnp.float32) * d_h ** -0.5 + rel_bias(table, rel)
    p = jax.nn.softmax(s, axis=-1).astype(v.dtype)
    return jnp.einsum('bhqk,bkhd->bqhd', p, vv).reshape(bsz, s_len, n_h * d_h)


def gated_delta_rule(q, k, v, g, beta, s0):
    bsz, t_len, n_h, d_k = q.shape
    d_v = v.shape[-1]
    pad = (-t_len) % CHUNK
    n_chunks = (t_len + pad) // CHUNK

    def to_chunks(t):
        t = jnp.pad(t.astype(jnp.float32), ((0, 0), (0, pad)) + ((0, 0),) * (t.ndim - 2))
        t = t.reshape((bsz, n_chunks, CHUNK) + t.shape[2:])
        return jnp.moveaxis(t, 3, 1)

    q, k, v, g, beta = to_chunks(q), to_chunks(k), to_chunks(v), to_chunks(g), to_chunks(beta)
    q = q * d_k ** -0.5
    gc = jnp.cumsum(g, axis=-1)
    tri = jnp.tril(jnp.ones((CHUNK, CHUNK), dtype=bool))
    strict = jnp.tril(jnp.ones((CHUNK, CHUNK), dtype=bool), -1)
    diff = gc[..., :, None] - gc[..., None, :]
    decay = jnp.where(tri, jnp.exp(jnp.where(tri, diff, 0.0)), 0.0)
    k_beta = k * beta[..., None]
    a_low = jnp.where(strict, jnp.einsum('bhnid,bhnjd->bhnij', k_beta, k) * decay, 0.0)
    rhs = jnp.concatenate([v * beta[..., None], k_beta * jnp.exp(gc)[..., None]], axis=-1)
    sol = lax.linalg.triangular_solve(a_low + jnp.eye(CHUNK, dtype=jnp.float32), rhs,
                                      left_side=True, lower=True, unit_diagonal=True)
    u, w = sol[..., :d_v], sol[..., d_v:]
    qk = jnp.einsum('bhnid,bhnjd->bhnij', q, k) * decay
    q_dec = q * jnp.exp(gc)[..., None]
    k_dec = k * jnp.exp(gc[..., -1:] - gc)[..., None]
    g_tot = jnp.exp(gc[..., -1])

    def step(state, xs):
        q_c, k_c, u_c, w_c, qk_c, gt_c = xs
        v_new = u_c - jnp.einsum('bhcd,bhde->bhce', w_c, state)
        o_c = jnp.einsum('bhcd,bhde->bhce', q_c, state) + jnp.einsum('bhij,bhje->bhie', qk_c, v_new)
        state = state * gt_c[..., None, None] + jnp.einsum('bhcd,bhce->bhde', k_c, v_new)
        return state, o_c

    xs = tuple(jnp.moveaxis(t, 2, 0) for t in (q_dec, k_dec, u, w, qk, g_tot))
    s_final, o = lax.scan(step, s0.astype(jnp.float32), xs)
    o = jnp.moveaxis(jnp.moveaxis(o, 0, 2), 1, 3)
    return o.reshape(bsz, t_len + pad, n_h, d_v)[:, :t_len], s_final


def complex_linear_combine(e1, e2):
    a1r, a1i, b1r, b1i = e1
    a2r, a2i, b2r, b2i = e2
    return (a2r * a1r - a2i * a1i, a2r * a1i + a2i * a1r,
            a2r * b1r - a2i * b1i + b2r, a2r * b1i + a2i * b1r + b2i)


def s5_ssm(u, h0_re, h0_im, a_re, a_im, log_dt, b_re, b_im, c_re, c_im, d_skip):
    f32 = jnp.float32
    a_re, a_im, b_re, b_im = a_re.astype(f32), a_im.astype(f32), b_re.astype(f32), b_im.astype(f32)
    c_re, c_im, d_skip = c_re.astype(f32), c_im.astype(f32), d_skip.astype(f32)
    dt = jnp.exp(log_dt.astype(f32))[:, None]
    mag = jnp.exp(dt * a_re)
    lam_re, lam_im = mag * jnp.cos(dt * a_im), mag * jnp.sin(dt * a_im)
    den = a_re * a_re + a_im * a_im
    coef_re = ((lam_re - 1.0) * a_re + lam_im * a_im) / den
    coef_im = (lam_im * a_re - (lam_re - 1.0) * a_im) / den
    bb_re = coef_re[..., None] * b_re - coef_im[..., None] * b_im
    bb_im = coef_re[..., None] * b_im + coef_im[..., None] * b_re
    x_re = jnp.einsum('btgh,gph->btgp', u, bb_re)
    x_im = jnp.einsum('btgh,gph->btgp', u, bb_im)
    x_re = x_re.at[:, 0].add(lam_re * h0_re - lam_im * h0_im)
    x_im = x_im.at[:, 0].add(lam_re * h0_im + lam_im * h0_re)
    lr = jnp.broadcast_to(lam_re, x_re.shape)
    li = jnp.broadcast_to(lam_im, x_im.shape)
    _, _, h_re, h_im = lax.associative_scan(complex_linear_combine, (lr, li, x_re, x_im), axis=1)
    y = (jnp.einsum('btgp,ghp->btgh', h_re, c_re) - jnp.einsum('btgp,ghp->btgh', h_im, c_im)
         + d_skip * u)
    return y, h_re[:, -1], h_im[:, -1]


def hybrid_mixer(x, lp, kv_cache, conv_buf, s0, h0_re, h0_im):
    bsz, t_len, _ = x.shape
    f32 = jnp.float32
    proj = jnp.einsum('btd,de->bte', x, lp['w_in'])
    qa, ka, va, qkv_b, beta_in, a_in, gate_b, u_c = jnp.split(proj, split_points(IN_SIZES), axis=-1)

    qa = qa.reshape(bsz, t_len, A_HEADS, A_HEAD_DIM)
    ka = ka.reshape(bsz, t_len, A_HEADS, A_HEAD_DIM)
    va = va.reshape(bsz, t_len, A_HEADS, A_HEAD_DIM)
    if kv_cache is None:
        out_a = band_attention_prompt(qa, ka, va, lp['a_rel_bias'])
        new_k, new_v = ka[:, -A_WINDOW:], va[:, -A_WINDOW:]
    else:
        out_a = band_attention_sample(qa, ka, va, kv_cache[0], kv_cache[1], lp['a_rel_bias'])
        new_k, new_v = ka, va

    buf = jnp.concatenate([conv_buf.astype(qkv_b.dtype), qkv_b], axis=1)
    new_conv = buf[:, -(CONV_W - 1):]
    conv = lax.conv_general_dilated(buf, lp['b_conv_w'][:, None, :].astype(buf.dtype),
                                    window_strides=(1,), padding='VALID',
                                    dimension_numbers=('NWC', 'WIO', 'NWC'),
                                    feature_group_count=3 * B_WIDTH)
    conv = jax.nn.silu(conv.astype(f32) + lp['b_conv_b'].astype(f32))
    qb, kb, vb = jnp.split(conv, 3, axis=-1)
    qb = l2_normalize(qb.reshape(bsz, t_len, B_HEADS, B_HEAD_DIM))
    kb = l2_normalize(kb.reshape(bsz, t_len, B_HEADS, B_HEAD_DIM))
    vb = vb.reshape(bsz, t_len, B_HEADS, B_HEAD_DIM)
    beta = jax.nn.sigmoid(beta_in.astype(f32))
    g = -jnp.exp(lp['b_a_log'].astype(f32)) * jax.nn.softplus(a_in.astype(f32) + lp['b_dt_bias'].astype(f32))
    ob, s_new = gated_delta_rule(qb, kb, vb, g, beta, s0)
    gate = jax.nn.silu(gate_b.astype(f32).reshape(bsz, t_len, B_HEADS, B_HEAD_DIM))
    ob = ob * lax.rsqrt(jnp.mean(ob * ob, -1, keepdims=True) + RMS_EPS) * lp['b_norm_w'].astype(f32) * gate
    out_b = ob.reshape(bsz, t_len, B_WIDTH).astype(x.dtype)

    u = u_c.astype(f32).reshape(bsz, t_len, C_GROUPS, C_GROUP)
    y_c, h_re, h_im = s5_ssm(u, h0_re.astype(f32), h0_im.astype(f32), lp['c_a_re'], lp['c_a_im'],
                             lp['c_log_dt'], lp['c_b_re'], lp['c_b_im'], lp['c_c_re'], lp['c_c_im'], lp['c_d'])
    z = jax.nn.gelu(y_c.reshape(bsz, t_len, C_WIDTH))
    out_c = (z * jax.nn.sigmoid(z @ lp['c_glu_w'].astype(f32) + lp['c_glu_b'].astype(f32))).astype(x.dtype)

    mix = jnp.concatenate([out_a.astype(x.dtype), out_b, out_c], axis=-1) @ lp['w_out']
    return mix, new_k, new_v, new_conv, s_new, h_re, h_im


def trunk_layer(x, lp, kv_cache, conv_buf, s0, h0_re, h0_im):
    mix, new_k, new_v, new_conv, s_new, h_re, h_im = hybrid_mixer(x, lp, kv_cache, conv_buf, s0, h0_re, h0_im)
    x = layer_norm(DN_ALPHA * x + mix.astype(x.dtype), lp['ln1_g'], lp['ln1_b'])
    hid = jnp.square(jax.nn.relu(x @ lp['w_up'] + lp['b_up']))
    x = layer_norm(DN_ALPHA * x + (hid @ lp['w_down']).astype(x.dtype), lp['ln2_g'], lp['ln2_b'])
    return x, new_k, new_v, new_conv, s_new, h_re, h_im


def setup_inputs(seed: int = 0) -> dict:
    key = jax.random.key(seed)
    ks = jax.random.split(key, 40)
    f32 = jnp.float32

    def nrm(i, shape, scale):
        return scale * jax.random.normal(ks[i], shape, f32)

    n_cache = min(A_WINDOW, PAST_LEN)
    dt_b = jnp.exp(jax.random.uniform(ks[13], (DEPTH, B_HEADS), f32, np.log(1e-3), np.log(1e-1)))
    n_idx = jnp.broadcast_to(jnp.arange(C_STATE, dtype=f32), (DEPTH, C_GROUPS, C_STATE))
    return {
        'x_prompt': nrm(0, (BATCH, SEQ, D_MODEL), 1.0),
        'x_sample': nrm(1, (DEC_BATCH, DEC_SEQ, D_MODEL), 1.0),
        'cache_a_k': nrm(2, (DEPTH, DEC_BATCH, n_cache, A_HEADS, A_HEAD_DIM), 1.0),
        'cache_a_v': nrm(3, (DEPTH, DEC_BATCH, n_cache, A_HEADS, A_HEAD_DIM), 1.0),
        'state_b_conv': nrm(4, (DEPTH, DEC_BATCH, CONV_W - 1, 3 * B_WIDTH), 1.0),
        'state_b_ssm': nrm(5, (DEPTH, DEC_BATCH, B_HEADS, B_HEAD_DIM, B_HEAD_DIM), 0.1),
        'state_c_re': nrm(6, (DEPTH, DEC_BATCH, C_GROUPS, C_STATE), 0.5),
        'state_c_im': nrm(7, (DEPTH, DEC_BATCH, C_GROUPS, C_STATE), 0.5),
        'w_in': nrm(8, (DEPTH, D_MODEL, IN_WIDTH), D_MODEL ** -0.5),
        'a_rel_bias': nrm(9, (DEPTH, A_HEADS, 2 * MAX_REL + 1), 0.1),
        'b_conv_w': nrm(10, (DEPTH, CONV_W, 3 * B_WIDTH), CONV_W ** -0.5),
        'b_conv_b': nrm(11, (DEPTH, 3 * B_WIDTH), 0.01),
        'b_a_log': jnp.log(jax.random.uniform(ks[12], (DEPTH, B_HEADS), f32, 1.0, 16.0)),
        'b_dt_bias': dt_b + jnp.log(-jnp.expm1(-dt_b)),
        'b_norm_w': 1.0 + nrm(14, (DEPTH, B_HEAD_DIM), 0.01),
        'c_a_re': -0.5 + nrm(15, (DEPTH, C_GROUPS, C_STATE), 0.01),
        'c_a_im': np.pi * n_idx + nrm(16, (DEPTH, C_GROUPS, C_STATE), 0.01),
        'c_log_dt': jax.random.uniform(ks[17], (DEPTH, C_GROUPS), f32, np.log(1e-3), np.log(1e-1)),
        'c_b_re': nrm(18, (DEPTH, C_GROUPS, C_STATE, C_GROUP), (2.0 * C_GROUP) ** -0.5),
        'c_b_im': nrm(19, (DEPTH, C_GROUPS, C_STATE, C_GROUP), (2.0 * C_GROUP) ** -0.5),
        'c_c_re': nrm(20, (DEPTH, C_GROUPS, C_GROUP, C_STATE), C_STATE ** -0.5),
        'c_c_im': nrm(21, (DEPTH, C_GROUPS, C_GROUP, C_STATE), C_STATE ** -0.5),
        'c_d': nrm(22, (DEPTH, C_GROUPS, C_GROUP), 1.0),
        'c_glu_w': nrm(23, (DEPTH, C_WIDTH, C_WIDTH), C_WIDTH ** -0.5),
        'c_glu_b': nrm(24, (DEPTH, C_WIDTH), 0.01),
        'w_out': nrm(25, (DEPTH, D_MODEL, D_MODEL), D_MODEL ** -0.5 * DN_BETA),
        'ln1_g': 1.0 + nrm(26, (DEPTH, D_MODEL), 0.01),
        'ln1_b': nrm(27, (DEPTH, D_MODEL), 0.01),
        'w_up': nrm(28, (DEPTH, D_MODEL, D_FF), D_MODEL ** -0.5),
        'b_up': nrm(29, (DEPTH, D_FF), 0.01),
        'w_down': nrm(30, (DEPTH, D_FF, D_MODEL), D_FF ** -0.5 * DN_BETA),
        'ln2_g': 1.0 + nrm(31, (DEPTH, D_MODEL), 0.01),
        'ln2_b': nrm(32, (DEPTH, D_MODEL), 0.01),
    }


def reference(x_prompt, x_sample, cache_a_k, cache_a_v, state_b_conv, state_b_ssm, state_c_re, state_c_im,
              w_in, a_rel_bias, b_conv_w, b_conv_b, b_a_log, b_dt_bias, b_norm_w,
              c_a_re, c_a_im, c_log_dt, c_b_re, c_b_im, c_c_re, c_c_im, c_d, c_glu_w, c_glu_b,
              w_out, ln1_g, ln1_b, w_up, b_up, w_down, ln2_g, ln2_b):
    yp, ys = x_prompt, x_sample
    bsz = x_prompt.shape[0]
    p_out = [[], [], [], [], [], []]
    s_out = [[], [], [], [], [], []]
    for l in range(DEPTH):
        lp = {
            'w_in': w_in[l], 'a_rel_bias': a_rel_bias[l],
            'b_conv_w': b_conv_w[l], 'b_conv_b': b_conv_b[l], 'b_a_log': b_a_log[l],
            'b_dt_bias': b_dt_bias[l], 'b_norm_w': b_norm_w[l],
            'c_a_re': c_a_re[l], 'c_a_im': c_a_im[l], 'c_log_dt': c_log_dt[l],
            'c_b_re': c_b_re[l], 'c_b_im': c_b_im[l], 'c_c_re': c_c_re[l], 'c_c_im': c_c_im[l],
            'c_d': c_d[l], 'c_glu_w': c_glu_w[l], 'c_glu_b': c_glu_b[l], 'w_out': w_out[l],
            'ln1_g': ln1_g[l], 'ln1_b': ln1_b[l], 'w_up': w_up[l], 'b_up': b_up[l],
            'w_down': w_down[l], 'ln2_g': ln2_g[l], 'ln2_b': ln2_b[l],
        }
        zero_conv = jnp.zeros((bsz, CONV_W - 1, 3 * B_WIDTH), x_prompt.dtype)
        zero_s = jnp.zeros((bsz, B_HEADS, B_HEAD_DIM, B_HEAD_DIM), jnp.float32)
        zero_h = jnp.zeros((bsz, C_GROUPS, C_STATE), jnp.float32)
        yp, *st_p = trunk_layer(yp, lp, None, zero_conv, zero_s, zero_h, zero_h)
        ys, *st_s = trunk_layer(ys, lp, (cache_a_k[l], cache_a_v[l]), state_b_conv[l], state_b_ssm[l],
                                state_c_re[l], state_c_im[l])
        for acc, s in zip(p_out, st_p):
            acc.append(s)
        for acc, s in zip(s_out, st_s):
            acc.append(s)
    new_a_k_prompt = jnp.stack(p_out[0])
    new_a_v_prompt = jnp.stack(p_out[1])
    new_b_conv_prompt = jnp.stack(p_out[2])
    new_b_ssm_prompt = jnp.stack(p_out[3])
    new_c_re_prompt = jnp.stack(p_out[4])
    new_c_im_prompt = jnp.stack(p_out[5])
    new_a_k_sample = jnp.stack(s_out[0])
    new_a_v_sample = jnp.stack(s_out[1])
    new_b_conv_sample = jnp.stack(s_out[2])
    new_b_ssm_sample = jnp.stack(s_out[3])
    new_c_re_sample = jnp.stack(s_out[4])
    new_c_im_sample = jnp.stack(s_out[5])
    return (yp, ys,
            new_a_k_prompt, new_a_v_prompt, new_b_conv_prompt, new_b_ssm_prompt, new_c_re_prompt, new_c_im_prompt,
            new_a_k_sample, new_a_v_sample, new_b_conv_sample, new_b_ssm_sample, new_c_re_sample, new_c_im_sample)
```

```python
import functools

import jax
import jax.numpy as jnp
import numpy as np
from jax import lax
from jax.experimental import pallas as pl
from jax.experimental.pallas import tpu as pltpu

F32 = jnp.float32
BF16 = jnp.bfloat16

D_MODEL = 1024
DEPTH = 2
CHUNK = 64
A_HEADS = 8
A_HEAD_DIM = 64
A_WIDTH = A_HEADS * A_HEAD_DIM
A_WINDOW = 8 * CHUNK
MAX_REL = 256
B_HEADS = 4
B_HEAD_DIM = 64
B_WIDTH = B_HEADS * B_HEAD_DIM
CONV_W = 4
C_WIDTH = D_MODEL - A_WIDTH - B_WIDTH
C_GROUP = 16
C_GROUPS = C_WIDTH // C_GROUP
C_STATE = 64
C_LANES = C_GROUPS * C_STATE
D_FF = 4 * D_MODEL
DN_ALPHA = (2.0 * DEPTH) ** 0.25
LN_EPS = 1e-5
RMS_EPS = 1e-6
NEG_INF = -1e30

SUBLANES = 8
LANES = 128
VMEM_LIMIT_BYTES = 56 * 1024 * 1024

ROW_TILE = 512
ATT_Q_BLOCK = 512
ATT_Q_SUB = 2 * CHUNK
ATT_K_SUB = ATT_Q_SUB + A_WINDOW
PAIR = 2 * CHUNK
S5_TIME_BLOCK = 128
S5_BATCH = SUBLANES
FF_SLAB = 1024

_SEG = {}
_off = 0
for _name, _w in (("q", A_WIDTH), ("k", A_WIDTH), ("v", A_WIDTH), ("qkvb", 3 * B_WIDTH), ("gate", B_WIDTH),
                  ("uc", C_WIDTH), ("beta", B_WIDTH), ("a", B_WIDTH)):
    _SEG[_name] = (_off, _off + _w)
    _off += _w
PROJ_WIDTH = _off


def _bdot(a, b):
    return jnp.dot(a.astype(BF16), b.astype(BF16), preferred_element_type=F32)


def _bdot_nt(a, b):
    return lax.dot_general(a.astype(BF16), b.astype(BF16), (((1,), (1,)), ((), ())), preferred_element_type=F32)


def _split3(a):
    hi = a.astype(BF16)
    r = a - hi.astype(F32)
    mid = r.astype(BF16)
    lo = (r - mid.astype(F32)).astype(BF16)
    return hi, mid, lo


def _dot_f32_by_exact(a, e):
    hi, mid, lo = _split3(a)
    d = functools.partial(jnp.dot, preferred_element_type=F32)
    return d(hi, e) + d(mid, e) + d(lo, e)


def _dot_exact_by_f32(e, a):
    hi, mid, lo = _split3(a)
    d = functools.partial(jnp.dot, preferred_element_type=F32)
    return d(e, hi) + d(e, mid) + d(e, lo)


def _dotx(a, b):
    a_hi = a.astype(BF16)
    a_lo = (a - a_hi.astype(F32)).astype(BF16)
    b_hi = b.astype(BF16)
    b_lo = (b - b_hi.astype(F32)).astype(BF16)
    d = functools.partial(jnp.dot, preferred_element_type=F32)
    return d(a_hi, b_hi) + d(a_hi, b_lo) + d(a_lo, b_hi)


def _blk(idx, size):
    return jnp.bitwise_and(idx, -size)


def _layer_norm(x, g, b):
    mu = jnp.mean(x, -1, keepdims=True)
    xc = x - mu
    var = jnp.mean(xc * xc, -1, keepdims=True)
    return xc * lax.rsqrt(var + LN_EPS) * g + b


def _silu(x):
    return x * jax.nn.sigmoid(x)


def _softplus(x):
    return jnp.maximum(x, 0.0) + jnp.log1p(jnp.exp(-jnp.abs(x)))


def _gelu_tanh(x):
    c = float(np.sqrt(2.0 / np.pi))
    return x * (0.5 * (1.0 + jnp.tanh(c * (x + 0.044715 * (x * x * x)))))


def _const_spec(shape):
    nd = len(shape)
    return pl.BlockSpec(shape, lambda *_: (0,) * nd, pipeline_mode=pl.Buffered(1))


def _params(semantics):
    return pltpu.CompilerParams(dimension_semantics=semantics, vmem_limit_bytes=VMEM_LIMIT_BYTES)


def _proj_kernel(x_ref, w_ref, q_ref, k_ref, v_ref, qkvb_ref, gate_ref, uc_ref, beta_ref, a_ref):
    xb = x_ref[...].astype(BF16)
    outs = {"q": q_ref, "k": k_ref, "v": v_ref, "qkvb": qkvb_ref, "gate": gate_ref, "uc": uc_ref,
            "beta": beta_ref, "a": a_ref}
    for name, ref in outs.items():
        lo, hi = _SEG[name]
        ref[...] = jnp.dot(xb, w_ref[:, lo:hi], preferred_element_type=F32)


def _proj(x2d, w, n_batch, uc_time_major):
    n_rows = x2d.shape[0]
    tm = min(ROW_TILE, n_rows)
    n_steps = n_rows // tm
    widths = {name: hi - lo for name, (lo, hi) in _SEG.items()}
    order = ("q", "k", "v", "qkvb", "gate", "uc", "beta", "a")
    out_shape, out_specs = [], []
    for name in order:
        wd = widths[name]
        if name == "uc" and uc_time_major:
            t_len = n_rows // n_batch
            per_b = t_len // tm
            out_shape.append(jax.ShapeDtypeStruct((t_len, n_batch * wd), F32))
            out_specs.append(pl.BlockSpec((tm, wd), lambda i: (i % per_b, i // per_b)))
        else:
            out_shape.append(jax.ShapeDtypeStruct((n_rows, wd), F32))
            out_specs.append(pl.BlockSpec((tm, wd), lambda i: (i, 0)))
    return pl.pallas_call(
        _proj_kernel,
        grid=(n_steps,),
        in_specs=[pl.BlockSpec((tm, D_MODEL), lambda i: (i, 0)), _const_spec(w.shape)],
        out_specs=out_specs,
        out_shape=out_shape,
        compiler_params=_params(("parallel",)),
        name="proj",
    )(x2d, w)


def _softmax_pv(q2, key_segs, val_segs, bias_segs, valid_segs):
    lane = lax.broadcasted_iota(jnp.int32, q2.shape, 1)
    out = jnp.zeros(q2.shape, F32)
    for e in range(2):
        sel = (lane >= A_HEAD_DIM) if e else (lane < A_HEAD_DIM)
        qe = jnp.where(sel, q2, 0.0).astype(BF16)
        scores = []
        for kseg, bias, valid in zip(key_segs, bias_segs[e], valid_segs):
            s = lax.dot_general(qe, kseg, (((1,), (1,)), ((), ())), preferred_element_type=F32) + bias
            if valid is not None:
                s = jnp.where(valid, s, NEG_INF)
            scores.append(s)
        m = functools.reduce(jnp.maximum, [jnp.max(s, -1, keepdims=True) for s in scores])
        probs = [jnp.exp(s - m) for s in scores]
        denom = functools.reduce(jnp.add, [jnp.sum(p, -1, keepdims=True) for p in probs])
        pv = functools.reduce(jnp.add, [jnp.dot(p.astype(BF16), vseg, preferred_element_type=F32)
                                        for p, vseg in zip(probs, val_segs)])
        out = jnp.where(sel, pv * (1.0 / denom), out)
    return out


def _attn_prompt_kernel(q_ref, kp_ref, kc_ref, vp_ref, vc_ref, bias_ref, o_ref, kk_ref, vv_ref):
    jb = pl.program_id(1)
    kk_ref[0:ATT_Q_BLOCK, :] = kp_ref[...].astype(BF16)
    kk_ref[ATT_Q_BLOCK:, :] = kc_ref[...].astype(BF16)
    vv_ref[0:ATT_Q_BLOCK, :] = vp_ref[...].astype(BF16)
    vv_ref[ATT_Q_BLOCK:, :] = vc_ref[...].astype(BF16)
    scale = A_HEAD_DIM ** -0.5
    for sub in range(ATT_Q_BLOCK // ATT_Q_SUB):
        r0 = sub * ATT_Q_SUB
        pos = lax.broadcasted_iota(jnp.int32, (1, ATT_K_SUB), 1) + ((jb - 1) * ATT_Q_BLOCK + r0)
        valid = pos >= 0
        for hp in range(A_HEADS // 2):
            c0 = hp * LANES
            q2 = q_ref[r0:r0 + ATT_Q_SUB, c0:c0 + LANES] * scale
            kseg = kk_ref[r0:r0 + ATT_K_SUB, c0:c0 + LANES]
            vseg = vv_ref[r0:r0 + ATT_K_SUB, c0:c0 + LANES]
            bias = [[bias_ref[2 * hp + e]] for e in range(2)]
            o_ref[r0:r0 + ATT_Q_SUB, c0:c0 + LANES] = _softmax_pv(q2, [kseg], [vseg], bias, [valid])


def _attn_prompt(q, k, v, bias, n_batch):
    n_rows = q.shape[0]
    per_b = n_rows // n_batch // ATT_Q_BLOCK
    blk = (ATT_Q_BLOCK, A_WIDTH)
    cur = pl.BlockSpec(blk, lambda b, j: (b * per_b + j, 0))
    prev = pl.BlockSpec(blk, lambda b, j: (b * per_b + jnp.maximum(j - 1, 0), 0))
    return pl.pallas_call(
        _attn_prompt_kernel,
        grid=(n_batch, per_b),
        in_specs=[cur, prev, cur, prev, cur, _const_spec(bias.shape)],
        out_specs=cur,
        out_shape=jax.ShapeDtypeStruct((n_rows, A_WIDTH), F32),
        scratch_shapes=[pltpu.VMEM((2 * ATT_Q_BLOCK, A_WIDTH), BF16), pltpu.VMEM((2 * ATT_Q_BLOCK, A_WIDTH), BF16)],
        compiler_params=_params(("parallel", "arbitrary")),
        name="attn_prompt",
    )(q, k, k, v, v, bias)


def _attn_sample_kernel(q_ref, kn_ref, vn_ref, kc_ref, vc_ref, bias_c_ref, bias_n_ref, o_ref):
    scale = A_HEAD_DIM ** -0.5
    for hp in range(A_HEADS // 2):
        c0 = hp * LANES
        q2 = q_ref[:, c0:c0 + LANES] * scale
        keys = [kc_ref[:, c0:c0 + LANES].astype(BF16), kn_ref[:, c0:c0 + LANES].astype(BF16)]
        vals = [vc_ref[:, c0:c0 + LANES].astype(BF16), vn_ref[:, c0:c0 + LANES].astype(BF16)]
        bias = [[bias_c_ref[2 * hp + e], bias_n_ref[2 * hp + e]] for e in range(2)]
        o_ref[:, c0:c0 + LANES] = _softmax_pv(q2, keys, vals, bias, [None, None])


def _attn_sample(q, k, v, k_cache, v_cache, bias_c, bias_n, n_batch):
    n_rows = q.shape[0]
    s_len = n_rows // n_batch
    n_cache = k_cache.shape[1]
    new = pl.BlockSpec((s_len, A_WIDTH), lambda b: (b, 0))
    cache = pl.BlockSpec((None, n_cache, A_WIDTH), lambda b: (b, 0, 0))
    return pl.pallas_call(
        _attn_sample_kernel,
        grid=(n_batch,),
        in_specs=[new, new, new, cache, cache, _const_spec(bias_c.shape), _const_spec(bias_n.shape)],
        out_specs=new,
        out_shape=jax.ShapeDtypeStruct((n_rows, A_WIDTH), F32),
        compiler_params=_params(("parallel",)),
        name="attn_sample",
    )(q, k, v, k_cache, v_cache, bias_c, bias_n)


def _rel_bias_blocks(table, n_q, n_k, q_offset, banded):
    i = np.arange(n_q)[:, None]
    j = np.arange(n_k)[None, :]
    if banded:
        jj = j - (i // CHUNK) * CHUNK
        inband = (jj >= 0) & (jj < A_WINDOW + CHUNK)
        rel = A_WINDOW + (i % CHUNK) - jj
    else:
        inband = np.ones((n_q, n_k), bool)
        rel = q_offset + i - j
    idx = np.clip(rel, -MAX_REL, MAX_REL) + MAX_REL
    return jnp.where(jnp.asarray(inband), table[:, idx].astype(F32), NEG_INF)


def _inv_unit_lower(a, row, col):
    base = 8
    a0 = jnp.where(_blk(row, base) == _blk(col, base), a, 0.0)
    x = jnp.where(row == col, 1.0, 0.0) - a0
    p = _dotx(a0, a0)
    x = x + _dotx(x, p)
    p = _dotx(p, p)
    x = x + _dotx(x, p)
    bs = base
    while bs < CHUNK:
        off = (_blk(row, 2 * bs) == _blk(col, 2 * bs)) & (_blk(row, bs) != _blk(col, bs))
        x = x - _dotx(x, _dotx(jnp.where(off, a, 0.0), x))
        bs *= 2
    return x


def _delta_kernel(qkvb_ref, gate_ref, beta_ref, a_ref, cst_ref, s0_ref, cw_ref, cb_ref, alog_ref, dtb_ref, nw_ref,
                  o_ref, cout_ref, sout_ref, xbuf, s_ref, *, t_valid):
    j = pl.program_id(1)

    @pl.when(j == 0)
    def _():
        xbuf[0:SUBLANES, :] = cst_ref[0]
        s_ref[...] = s0_ref[0]

    xbuf[SUBLANES:SUBLANES + PAIR, :] = qkvb_ref[...]
    conv = cb_ref[...]
    for w in range(CONV_W):
        start = SUBLANES - (CONV_W - 1) + w
        conv = conv + xbuf[start:start + PAIR, :] * cw_ref[w:w + 1, :]
    conv = _silu(conv)
    tail = xbuf[t_valid:t_valid + SUBLANES, :]
    cout_ref[0] = tail
    xbuf[0:SUBLANES, :] = tail

    hrow = lax.broadcasted_iota(jnp.int32, (B_WIDTH, B_WIDTH), 0)
    hcol = lax.broadcasted_iota(jnp.int32, (B_WIDTH, B_WIDTH), 1)
    head_ones = jnp.where(_blk(hrow, B_HEAD_DIM) == _blk(hcol, B_HEAD_DIM), 1.0, 0.0).astype(BF16)

    def l2n(t):
        return t * lax.rsqrt(_dot_f32_by_exact(t * t, head_ones) + RMS_EPS)

    qn = l2n(conv[:, 0:B_WIDTH]) * (B_HEAD_DIM ** -0.5)
    kn = l2n(conv[:, B_WIDTH:2 * B_WIDTH])
    vv = conv[:, 2 * B_WIDTH:3 * B_WIDTH]
    beta = jax.nn.sigmoid(beta_ref[...])
    g = -jnp.exp(alog_ref[...]) * _softplus(a_ref[...] + dtb_ref[...])
    if t_valid < PAIR:
        live = lax.broadcasted_iota(jnp.int32, (PAIR, B_WIDTH), 0) < t_valid
        kn = jnp.where(live, kn, 0.0)
        vv = jnp.where(live, vv, 0.0)
        beta = jnp.where(live, beta, 0.0)
        g = jnp.where(live, g, 0.0)

    row = lax.broadcasted_iota(jnp.int32, (PAIR, PAIR), 0)
    col = lax.broadcasted_iota(jnp.int32, (PAIR, PAIR), 1)
    same = _blk(row, CHUNK) == _blk(col, CHUNK)
    tril = same & (row >= col)
    strict = same & (row > col)
    gc = _dot_exact_by_f32(jnp.where(tril, 1.0, 0.0).astype(BF16), g)
    gl = _dot_exact_by_f32(jnp.where(same, 1.0, 0.0).astype(BF16), g)
    egc = jnp.exp(gc)
    erest = jnp.exp(gl - gc)
    egl = jnp.exp(gl)

    heads_out = []
    for h in range(B_HEADS):
        sl = slice(h * B_HEAD_DIM, (h + 1) * B_HEAD_DIM)
        k_h, q_h, v_h, b_h = kn[:, sl], qn[:, sl], vv[:, sl], beta[:, sl]
        gc_h, egc_h = gc[:, sl], egc[:, sl]
        gcol = jnp.concatenate([gc_h, gc_h], axis=1)
        diff = gcol - gcol.T
        decay = jnp.where(tril, jnp.exp(jnp.where(tril, diff, 0.0)), 0.0)
        kb = k_h * b_h
        a_low = jnp.where(strict, _bdot_nt(kb, k_h) * decay, 0.0)
        x_inv = _inv_unit_lower(a_low, row, col)
        uw = _dotx(x_inv, jnp.concatenate([v_h * b_h, kb * egc_h], axis=1))
        u, w = uw[:, :B_HEAD_DIM], uw[:, B_HEAD_DIM:]
        qk = jnp.where(tril, _bdot_nt(q_h, k_h) * decay, 0.0)
        q_dec = q_h * egc_h
        k_dec = k_h * erest[:, sl]
        state = s_ref[h]
        o_chunks = []
        for c in range(PAIR // CHUNK):
            rs = slice(c * CHUNK, (c + 1) * CHUNK)
            ws = _bdot(jnp.concatenate([w[rs], q_dec[rs]], axis=0), state)
            v_new = u[rs] - ws[:CHUNK]
            o_chunks.append(ws[CHUNK:] + _bdot(qk[rs, rs], v_new))
            kv = lax.dot_general(k_dec[rs].astype(BF16), v_new.astype(BF16), (((0,), (0,)), ((), ())),
                                 preferred_element_type=F32)
            state = state * egl[c * CHUNK:c * CHUNK + 1, sl] + kv
        s_ref[h] = state
        heads_out.append(jnp.concatenate(o_chunks, axis=0))
    ob = jnp.concatenate(heads_out, axis=1)
    ms = _dot_f32_by_exact(ob * ob, head_ones) * (1.0 / B_HEAD_DIM)
    o_ref[...] = ob * lax.rsqrt(ms + RMS_EPS) * nw_ref[...] * _silu(gate_ref[...])
    sout_ref[0] = s_ref[...]


def _delta(qkvb, gate, beta, a, conv_state, s0, lw, n_batch, t_valid):
    n_rows = qkvb.shape[0]
    per_b = n_rows // n_batch // PAIR

    def rows(width):
        return pl.BlockSpec((PAIR, width), lambda b, j: (b * per_b + j, 0))

    cst = pl.BlockSpec((1, SUBLANES, 3 * B_WIDTH), lambda b, j: (b, 0, 0))
    st = pl.BlockSpec((1, B_HEADS, B_HEAD_DIM, B_HEAD_DIM), lambda b, j: (b, 0, 0, 0))
    consts = [lw["conv_w"], lw["conv_b"], lw["a_log"], lw["dt_bias"], lw["norm_w"]]
    return pl.pallas_call(
        functools.partial(_delta_kernel, t_valid=t_valid),
        grid=(n_batch, per_b),
        in_specs=[rows(3 * B_WIDTH), rows(B_WIDTH), rows(B_WIDTH), rows(B_WIDTH), cst, st]
        + [_const_spec(c.shape) for c in consts],
        out_specs=[rows(B_WIDTH), cst, st],
        out_shape=[jax.ShapeDtypeStruct((n_rows, B_WIDTH), F32),
                   jax.ShapeDtypeStruct((n_batch, SUBLANES, 3 * B_WIDTH), F32),
                   jax.ShapeDtypeStruct((n_batch, B_HEADS, B_HEAD_DIM, B_HEAD_DIM), F32)],
        scratch_shapes=[pltpu.VMEM((SUBLANES + PAIR, 3 * B_WIDTH), F32),
                        pltpu.VMEM((B_HEADS, B_HEAD_DIM, B_HEAD_DIM), F32)],
        compiler_params=_params(("parallel", "arbitrary")),
        name="delta",
    )(qkvb, gate, beta, a, conv_state, s0, *consts)


def _s5_kernel(u_ref, h0r_ref, h0i_ref, lr_ref, li_ref, bre_ref, bim_ref, cre_ref, cim_ref, d_ref, gw_ref, gb_ref,
               o_ref, hro_ref, hio_ref, xr, xi, hs, *, t_block):
    tb = pl.program_id(1)

    @pl.when(tb == 0)
    def _():
        hs[0] = h0r_ref[0]
        hs[1] = h0i_ref[0]

    u = u_ref[0]
    ub = u.astype(BF16)
    xr[...] = jnp.dot(ub, bre_ref[...], preferred_element_type=F32)
    xi[...] = jnp.dot(ub, bim_ref[...], preferred_element_type=F32)
    lam_r = jnp.broadcast_to(lr_ref[...], (S5_BATCH, C_LANES))
    lam_i = jnp.broadcast_to(li_ref[...], (S5_BATCH, C_LANES))

    def step(t, carry):
        h_r, h_i = carry
        rows = pl.ds(pl.multiple_of(t * S5_BATCH, S5_BATCH), S5_BATCH)
        n_r = lam_r * h_r - lam_i * h_i + xr[rows, :]
        n_i = lam_r * h_i + lam_i * h_r + xi[rows, :]
        xr[rows, :] = n_r
        xi[rows, :] = n_i
        return n_r, n_i

    h_r, h_i = lax.fori_loop(0, t_block, step, (hs[0], hs[1]))
    hs[0] = h_r
    hs[1] = h_i
    hro_ref[0] = h_r
    hio_ref[0] = h_i
    y = (jnp.dot(xr[...].astype(BF16), cre_ref[...], preferred_element_type=F32)
         - jnp.dot(xi[...].astype(BF16), cim_ref[...], preferred_element_type=F32) + d_ref[...] * u)
    z = _gelu_tanh(y)
    o_ref[0] = z * jax.nn.sigmoid(jnp.dot(z.astype(BF16), gw_ref[...], preferred_element_type=F32) + gb_ref[...])


def _s5(u, h0_re, h0_im, lw, t_block):
    n_groups, n_rows, _ = u.shape
    rows_blk = t_block * S5_BATCH
    n_tb = n_rows // rows_blk
    u_spec = pl.BlockSpec((1, rows_blk, C_WIDTH), lambda g, t: (g, t, 0))
    h_spec = pl.BlockSpec((1, S5_BATCH, C_LANES), lambda g, t: (g, 0, 0))
    consts = [lw["lam_re"], lw["lam_im"], lw["b_re"], lw["b_im"], lw["c_re"], lw["c_im"], lw["c_d"],
              lw["glu_w"], lw["glu_b"]]
    return pl.pallas_call(
        functools.partial(_s5_kernel, t_block=t_block),
        grid=(n_groups, n_tb),
        in_specs=[u_spec, h_spec, h_spec] + [_const_spec(c.shape) for c in consts],
        out_specs=[u_spec, h_spec, h_spec],
        out_shape=[jax.ShapeDtypeStruct(u.shape, F32),
                   jax.ShapeDtypeStruct(h0_re.shape, F32), jax.ShapeDtypeStruct(h0_im.shape, F32)],
        scratch_shapes=[pltpu.VMEM((rows_blk, C_LANES), F32), pltpu.VMEM((rows_blk, C_LANES), F32),
                        pltpu.VMEM((2, S5_BATCH, C_LANES), F32)],
        compiler_params=_params(("parallel", "arbitrary")),
        name="s5",
    )(u, h0_re, h0_im, *consts)


def _post_kernel(x_ref, a_ref, b_ref, c_ref, wo_ref, g1_ref, b1_ref, wu_ref, bu_ref, wd_ref, g2_ref, b2_ref, o_ref):
    mix = (jnp.dot(a_ref[...].astype(BF16), wo_ref[0:A_WIDTH, :], preferred_element_type=F32)
           + jnp.dot(b_ref[...].astype(BF16), wo_ref[A_WIDTH:A_WIDTH + B_WIDTH, :], preferred_element_type=F32)
           + jnp.dot(c_ref[...].astype(BF16), wo_ref[A_WIDTH + B_WIDTH:, :], preferred_element_type=F32))
    x1 = _layer_norm(DN_ALPHA * x_ref[...] + mix, g1_ref[...], b1_ref[...])
    x1b = x1.astype(BF16)
    acc = jnp.zeros(x1.shape, F32)
    for s in range(D_FF // FF_SLAB):
        cs = slice(s * FF_SLAB, (s + 1) * FF_SLAB)
        hid = jnp.dot(x1b, wu_ref[:, cs], preferred_element_type=F32) + bu_ref[:, cs]
        hid = jnp.square(jnp.maximum(hid, 0.0))
        acc = acc + jnp.dot(hid.astype(BF16), wd_ref[cs, :], preferred_element_type=F32)
    o_ref[...] = _layer_norm(DN_ALPHA * x1 + acc, g2_ref[...], b2_ref[...])


def _post(x2d, out_a, out_b, out_c, lw, n_batch, c_time_major):
    n_rows = x2d.shape[0]
    tm = min(ROW_TILE, n_rows)

    def rows(width):
        return pl.BlockSpec((tm, width), lambda i: (i, 0))

    if c_time_major:
        per_b = n_rows // n_batch // tm
        c_spec = pl.BlockSpec((tm, C_WIDTH), lambda i: (i % per_b, i // per_b))
    else:
        c_spec = rows(C_WIDTH)
    consts = [lw["w_out"], lw["ln1_g"], lw["ln1_b"], lw["w_up"], lw["b_up"], lw["w_down"], lw["ln2_g"], lw["ln2_b"]]
    return pl.pallas_call(
        _post_kernel,
        grid=(n_rows // tm,),
        in_specs=[rows(D_MODEL), rows(A_WIDTH), rows(B_WIDTH), c_spec] + [_const_spec(c.shape) for c in consts],
        out_specs=rows(D_MODEL),
        out_shape=jax.ShapeDtypeStruct((n_rows, D_MODEL), F32),
        compiler_params=_params(("parallel",)),
        name="post",
    )(x2d, out_a, out_b, out_c, *consts)


def _layer_weights(l, w_in, a_rel_bias, b_conv_w, b_conv_b, b_a_log, b_dt_bias, b_norm_w, c_a_re, c_a_im, c_log_dt,
                   c_b_re, c_b_im, c_c_re, c_c_im, c_d, c_glu_w, c_glu_b, w_out, ln1_g, ln1_b, w_up, b_up, w_down,
                   ln2_g, ln2_b, s_len, n_cache):
    wi = w_in[l]
    o_q, o_k, o_v = 0, A_WIDTH, 2 * A_WIDTH
    o_qkvb = 3 * A_WIDTH
    o_beta = o_qkvb + 3 * B_WIDTH
    o_a = o_beta + B_HEADS
    o_gate = o_a + B_HEADS
    o_uc = o_gate + B_WIDTH
    per_head = lambda cols: jnp.repeat(cols, B_HEAD_DIM, axis=-1)
    w_proj = jnp.concatenate([
        wi[:, o_q:o_q + A_WIDTH], wi[:, o_k:o_k + A_WIDTH], wi[:, o_v:o_v + A_WIDTH],
        wi[:, o_qkvb:o_qkvb + 3 * B_WIDTH], wi[:, o_gate:o_gate + B_WIDTH], wi[:, o_uc:o_uc + C_WIDTH],
        per_head(wi[:, o_beta:o_beta + B_HEADS]), per_head(wi[:, o_a:o_a + B_HEADS])], axis=1).astype(BF16)

    table = a_rel_bias[l]
    bias_sample = _rel_bias_blocks(table, s_len, n_cache + s_len, n_cache, banded=False)

    a_re, a_im = c_a_re[l].astype(F32), c_a_im[l].astype(F32)
    dt = jnp.exp(c_log_dt[l].astype(F32))[:, None]
    mag = jnp.exp(dt * a_re)
    lam_re, lam_im = mag * jnp.cos(dt * a_im), mag * jnp.sin(dt * a_im)
    den = a_re * a_re + a_im * a_im
    coef_re = ((lam_re - 1.0) * a_re + lam_im * a_im) / den
    coef_im = (lam_im * a_re - (lam_re - 1.0) * a_im) / den
    bre, bim = c_b_re[l].astype(F32), c_b_im[l].astype(F32)
    bb_re = coef_re[..., None] * bre - coef_im[..., None] * bim
    bb_im = coef_re[..., None] * bim + coef_im[..., None] * bre
    eye = jnp.eye(C_GROUPS, dtype=F32)
    in_bd = lambda t: jnp.einsum("gph,gk->ghkp", t, eye).reshape(C_WIDTH, C_LANES).astype(BF16)
    out_bd = lambda t: jnp.einsum("ghp,gk->gpkh", t.astype(F32), eye).reshape(C_LANES, C_WIDTH).astype(BF16)
    row = lambda t: t.astype(F32).reshape(1, -1)

    return {
        "w_proj": w_proj,
        "bias_prompt": _rel_bias_blocks(table, ATT_Q_SUB, ATT_K_SUB, 0, banded=True),
        "bias_cache": bias_sample[:, :, :n_cache], "bias_new": bias_sample[:, :, n_cache:],
        "conv_w": b_conv_w[l].astype(F32), "conv_b": row(b_conv_b[l]),
        "a_log": row(per_head(b_a_log[l])), "dt_bias": row(per_head(b_dt_bias[l])),
        "norm_w": row(jnp.tile(b_norm_w[l], B_HEADS)),
        "lam_re": row(lam_re), "lam_im": row(lam_im), "b_re": in_bd(bb_re), "b_im": in_bd(bb_im),
        "c_re": out_bd(c_c_re[l]), "c_im": out_bd(c_c_im[l]), "c_d": row(c_d[l]),
        "glu_w": c_glu_w[l].astype(BF16), "glu_b": row(c_glu_b[l]),
        "w_out": w_out[l].astype(BF16), "ln1_g": row(ln1_g[l]), "ln1_b": row(ln1_b[l]),
        "w_up": w_up[l].astype(BF16), "b_up": row(b_up[l]), "w_down": w_down[l].astype(BF16),
        "ln2_g": row(ln2_g[l]), "ln2_b": row(ln2_b[l]),
    }


def _pad_conv_state(conv_buf):
    return jnp.pad(conv_buf.astype(F32), ((0, 0), (SUBLANES - (CONV_W - 1), 0), (0, 0)))


def _prompt_layer(x2d, lw, n_batch):
    t_len = x2d.shape[0] // n_batch
    q, k, v, qkvb, gate, uc, beta, a = _proj(x2d, lw["w_proj"], n_batch, uc_time_major=True)
    out_a = _attn_prompt(q, k, v, lw["bias_prompt"], n_batch)
    out_b, conv_o, s_new = _delta(
        qkvb, gate, beta, a, jnp.zeros((n_batch, SUBLANES, 3 * B_WIDTH), F32),
        jnp.zeros((n_batch, B_HEADS, B_HEAD_DIM, B_HEAD_DIM), F32), lw, n_batch, t_valid=PAIR)
    zeros_h = jnp.zeros((1, S5_BATCH, C_LANES), F32)
    out_c, h_re, h_im = _s5(uc.reshape(1, t_len * n_batch, C_WIDTH), zeros_h, zeros_h, lw, S5_TIME_BLOCK)
    y = _post(x2d, out_a, out_b, out_c.reshape(t_len, n_batch * C_WIDTH), lw, n_batch, c_time_major=True)
    heads = lambda t: t.reshape(n_batch, t_len, A_HEADS, A_HEAD_DIM)[:, -A_WINDOW:]
    state = lambda t: t.reshape(n_batch, C_GROUPS, C_STATE)
    return y, (heads(k), heads(v), conv_o[:, -(CONV_W - 1):], s_new, state(h_re), state(h_im))


def _sample_layer(x2d, lw, n_batch, k_cache, v_cache, conv_buf, s0, h0_re, h0_im):
    s_len = x2d.shape[0] // n_batch
    n_groups = n_batch // S5_BATCH
    n_cache = k_cache.shape[1]
    q, k, v, qkvb, gate, uc, beta, a = _proj(x2d, lw["w_proj"], n_batch, uc_time_major=False)
    out_a = _attn_sample(q, k, v, k_cache.reshape(n_batch, n_cache, A_WIDTH), v_cache.reshape(n_batch, n_cache, A_WIDTH),
                         lw["bias_cache"], lw["bias_new"], n_batch)

    def pad_rows(t):
        t = t.reshape(n_batch, s_len, -1)
        return jnp.pad(t, ((0, 0), (0, PAIR - s_len), (0, 0))).reshape(n_batch * PAIR, -1)

    out_b, conv_o, s_new = _delta(pad_rows(qkvb), pad_rows(gate), pad_rows(beta), pad_rows(a),
                                  _pad_conv_state(conv_buf), s0.astype(F32), lw, n_batch, t_valid=s_len)
    out_b = out_b.reshape(n_batch, PAIR, B_WIDTH)[:, :s_len].reshape(n_batch * s_len, B_WIDTH)

    u = uc.reshape(n_groups, S5_BATCH, s_len, C_WIDTH).transpose(0, 2, 1, 3).reshape(n_groups, s_len * S5_BATCH, C_WIDTH)
    grp = lambda t: t.astype(F32).reshape(n_groups, S5_BATCH, C_LANES)
    out_c, h_re, h_im = _s5(u, grp(h0_re), grp(h0_im), lw, s_len)
    out_c = out_c.reshape(n_groups, s_len, S5_BATCH, C_WIDTH).transpose(0, 2, 1, 3).reshape(n_batch * s_len, C_WIDTH)
    y = _post(x2d, out_a, out_b, out_c, lw, n_batch, c_time_major=False)
    heads = lambda t: t.reshape(n_batch, s_len, A_HEADS, A_HEAD_DIM)
    state = lambda t: t.reshape(n_batch, C_GROUPS, C_STATE)
    return y, (heads(k), heads(v), conv_o[:, -(CONV_W - 1):], s_new, state(h_re), state(h_im))


def kernel(x_prompt, x_sample, cache_a_k, cache_a_v, state_b_conv, state_b_ssm, state_c_re, state_c_im, w_in, a_rel_bias, b_conv_w, b_conv_b, b_a_log, b_dt_bias, b_norm_w, c_a_re, c_a_im, c_log_dt, c_b_re, c_b_im, c_c_re, c_c_im, c_d, c_glu_w, c_glu_b, w_out, ln1_g, ln1_b, w_up, b_up, w_down, ln2_g, ln2_b):
    n_p, t_p, _ = x_prompt.shape
    n_s, t_s, _ = x_sample.shape
    n_cache = cache_a_k.shape[2]
    yp = x_prompt.reshape(n_p * t_p, D_MODEL)
    ys = x_sample.reshape(n_s * t_s, D_MODEL)
    p_out, s_out = [], []
    for l in range(DEPTH):
        lw = _layer_weights(l, w_in, a_rel_bias, b_conv_w, b_conv_b, b_a_log, b_dt_bias, b_norm_w, c_a_re, c_a_im,
                            c_log_dt, c_b_re, c_b_im, c_c_re, c_c_im, c_d, c_glu_w, c_glu_b, w_out, ln1_g, ln1_b,
                            w_up, b_up, w_down, ln2_g, ln2_b, t_s, n_cache)
        yp, st_p = _prompt_layer(yp, lw, n_p)
        ys, st_s = _sample_layer(ys, lw, n_s, cache_a_k[l], cache_a_v[l], state_b_conv[l], state_b_ssm[l],
                                 state_c_re[l], state_c_im[l])
        p_out.append(st_p)
        s_out.append(st_s)
    stack = lambda outs, i: jnp.stack([o[i] for o in outs])
    return (yp.reshape(n_p, t_p, D_MODEL), ys.reshape(n_s, t_s, D_MODEL),
            *[stack(p_out, i) for i in range(6)], *[stack(s_out, i) for i in range(6)])
```

```python
import functools

import jax
import jax.numpy as jnp
import numpy as np
from jax import lax
from jax.experimental import pallas as pl
from jax.experimental.pallas import tpu as pltpu

F32 = jnp.float32
BF16 = jnp.bfloat16

D_MODEL = 1024
DEPTH = 2
CHUNK = 64
A_HEADS = 8
A_HEAD_DIM = 64
A_WIDTH = A_HEADS * A_HEAD_DIM
A_WINDOW = 8 * CHUNK
MAX_REL = 256
B_HEADS = 4
B_HEAD_DIM = 64
B_WIDTH = B_HEADS * B_HEAD_DIM
CONV_W = 4
C_WIDTH = D_MODEL - A_WIDTH - B_WIDTH
C_GROUP = 16
C_GROUPS = C_WIDTH // C_GROUP
C_STATE = 64
C_LANES = C_GROUPS * C_STATE
D_FF = 4 * D_MODEL
DN_ALPHA = (2.0 * DEPTH) ** 0.25
LN_EPS = 1e-5
RMS_EPS = 1e-6
NEG_INF = -1e30

SUBLANES = 8
LANES = 128
VMEM_LIMIT_BYTES = 56 * 1024 * 1024

ROW_TILE = 512
ATT_Q_BLOCK = 512
ATT_Q_SUB = 2 * CHUNK
ATT_K_SUB = ATT_Q_SUB + A_WINDOW
PAIR = 2 * CHUNK
DELTA_ROWS = 2 * PAIR
S5_TIME_BLOCK = 128
S5_BATCH = SUBLANES
FF_SLAB = 1024

_SEG = {}
_off = 0
for _name, _w in (("q", A_WIDTH), ("k", A_WIDTH), ("v", A_WIDTH), ("qkvb", 3 * B_WIDTH), ("gate", B_WIDTH),
                  ("uc", C_WIDTH), ("beta", B_WIDTH), ("a", B_HEADS * LANES)):
    _SEG[_name] = (_off, _off + _w)
    _off += _w
PROJ_WIDTH = _off


def _bdot(a, b):
    return jnp.dot(a.astype(BF16), b.astype(BF16), preferred_element_type=F32)


def _bdot_nt(a, b):
    return lax.dot_general(a.astype(BF16), b.astype(BF16), (((1,), (1,)), ((), ())), preferred_element_type=F32)


def _split3(a):
    hi = a.astype(BF16)
    r = a - hi.astype(F32)
    mid = r.astype(BF16)
    lo = (r - mid.astype(F32)).astype(BF16)
    return hi, mid, lo


def _dot_f32_by_exact(a, e):
    lhs = jnp.concatenate(_split3(a), axis=1)
    return jnp.dot(lhs, jnp.concatenate([e, e, e], axis=0), preferred_element_type=F32)


def _dot_exact_by_f32(e, a):
    rhs = jnp.concatenate(_split3(a), axis=0)
    return jnp.dot(jnp.concatenate([e, e, e], axis=1), rhs, preferred_element_type=F32)


def _dotx(a, b):
    a_hi = a.astype(BF16)
    a_lo = (a - a_hi.astype(F32)).astype(BF16)
    b_hi = b.astype(BF16)
    b_lo = (b - b_hi.astype(F32)).astype(BF16)
    lhs = jnp.concatenate([a_hi, a_hi, a_lo], axis=1)
    rhs = jnp.concatenate([b_hi, b_lo, b_hi], axis=0)
    return jnp.dot(lhs, rhs, preferred_element_type=F32)


def _blk(idx, size):
    return jnp.bitwise_and(idx, -size)


def _layer_norm(x, g, b):
    mu = jnp.mean(x, -1, keepdims=True)
    xc = x - mu
    var = jnp.mean(xc * xc, -1, keepdims=True)
    return xc * lax.rsqrt(var + LN_EPS) * g + b


def _silu(x):
    return x * jax.nn.sigmoid(x)


def _softplus(x):
    return jnp.maximum(x, 0.0) + jnp.log1p(jnp.exp(-jnp.abs(x)))


def _gelu_tanh(x):
    c = float(np.sqrt(2.0 / np.pi))
    return x * (0.5 * (1.0 + jnp.tanh(c * (x + 0.044715 * (x * x * x)))))


def _const_spec(shape):
    nd = len(shape)
    return pl.BlockSpec(shape, lambda *_: (0,) * nd, pipeline_mode=pl.Buffered(1))


def _params(semantics):
    return pltpu.CompilerParams(dimension_semantics=semantics, vmem_limit_bytes=VMEM_LIMIT_BYTES)


def _proj_kernel(x_ref, w_ref, q_ref, k_ref, v_ref, qkvb_ref, gate_ref, uc_ref, beta_ref, a_ref):
    xb = x_ref[...].astype(BF16)
    outs = {"q": q_ref, "k": k_ref, "v": v_ref, "qkvb": qkvb_ref, "gate": gate_ref, "uc": uc_ref,
            "beta": beta_ref, "a": a_ref}
    for name, ref in outs.items():
        lo, hi = _SEG[name]
        ref[...] = jnp.dot(xb, w_ref[:, lo:hi], preferred_element_type=F32)


def _proj(x2d, w, n_batch, uc_time_major):
    n_rows = x2d.shape[0]
    tm = min(ROW_TILE, n_rows)
    n_steps = n_rows // tm
    widths = {name: hi - lo for name, (lo, hi) in _SEG.items()}
    order = ("q", "k", "v", "qkvb", "gate", "uc", "beta", "a")
    out_shape, out_specs = [], []
    for name in order:
        wd = widths[name]
        if name == "uc" and uc_time_major:
            t_len = n_rows // n_batch
            per_b = t_len // tm
            out_shape.append(jax.ShapeDtypeStruct((t_len, n_batch * wd), F32))
            out_specs.append(pl.BlockSpec((tm, wd), lambda i: (i % per_b, i // per_b)))
        else:
            out_shape.append(jax.ShapeDtypeStruct((n_rows, wd), F32))
            out_specs.append(pl.BlockSpec((tm, wd), lambda i: (i, 0)))
    return pl.pallas_call(
        _proj_kernel,
        grid=(n_steps,),
        in_specs=[pl.BlockSpec((tm, D_MODEL), lambda i: (i, 0)), _const_spec(w.shape)],
        out_specs=out_specs,
        out_shape=out_shape,
        compiler_params=_params(("parallel",)),
        name="proj",
    )(x2d, w)


def _softmax_pv(q2, key_segs, val_segs, bias_segs, valid_segs):
    lane = lax.broadcasted_iota(jnp.int32, q2.shape, 1)
    out = jnp.zeros(q2.shape, F32)
    for e in range(2):
        sel = (lane >= A_HEAD_DIM) if e else (lane < A_HEAD_DIM)
        qe = jnp.where(sel, q2, 0.0).astype(BF16)
        scores = []
        for kseg, bias, valid in zip(key_segs, bias_segs[e], valid_segs):
            s = lax.dot_general(qe, kseg, (((1,), (1,)), ((), ())), preferred_element_type=F32) + bias
            if valid is not None:
                s = jnp.where(valid, s, NEG_INF)
            scores.append(s)
        m = functools.reduce(jnp.maximum, [jnp.max(s, -1, keepdims=True) for s in scores])
        probs = [jnp.exp(s - m) for s in scores]
        denom = functools.reduce(jnp.add, [jnp.sum(p, -1, keepdims=True) for p in probs])
        pv = functools.reduce(jnp.add, [jnp.dot(p.astype(BF16), vseg, preferred_element_type=F32)
                                        for p, vseg in zip(probs, val_segs)])
        out = jnp.where(sel, pv * (1.0 / denom), out)
    return out


def _attn_prompt_kernel(q_ref, kp_ref, kc_ref, vp_ref, vc_ref, bias_ref, o_ref, kk_ref, vv_ref):
    jb = pl.program_id(1)
    kk_ref[0:ATT_Q_BLOCK, :] = kp_ref[...].astype(BF16)
    kk_ref[ATT_Q_BLOCK:, :] = kc_ref[...].astype(BF16)
    vv_ref[0:ATT_Q_BLOCK, :] = vp_ref[...].astype(BF16)
    vv_ref[ATT_Q_BLOCK:, :] = vc_ref[...].astype(BF16)
    scale = A_HEAD_DIM ** -0.5
    for sub in range(ATT_Q_BLOCK // ATT_Q_SUB):
        r0 = sub * ATT_Q_SUB
        pos = lax.broadcasted_iota(jnp.int32, (1, ATT_K_SUB), 1) + ((jb - 1) * ATT_Q_BLOCK + r0)
        valid = pos >= 0
        for hp in range(A_HEADS // 2):
            c0 = hp * LANES
            q2 = q_ref[r0:r0 + ATT_Q_SUB, c0:c0 + LANES] * scale
            kseg = kk_ref[r0:r0 + ATT_K_SUB, c0:c0 + LANES]
            vseg = vv_ref[r0:r0 + ATT_K_SUB, c0:c0 + LANES]
            bias = [[bias_ref[2 * hp + e]] for e in range(2)]
            o_ref[r0:r0 + ATT_Q_SUB, c0:c0 + LANES] = _softmax_pv(q2, [kseg], [vseg], bias, [valid])


def _attn_prompt(q, k, v, bias, n_batch):
    n_rows = q.shape[0]
    per_b = n_rows // n_batch // ATT_Q_BLOCK
    blk = (ATT_Q_BLOCK, A_WIDTH)
    cur = pl.BlockSpec(blk, lambda b, j: (b * per_b + j, 0))
    prev = pl.BlockSpec(blk, lambda b, j: (b * per_b + jnp.maximum(j - 1, 0), 0))
    return pl.pallas_call(
        _attn_prompt_kernel,
        grid=(n_batch, per_b),
        in_specs=[cur, prev, cur, prev, cur, _const_spec(bias.shape)],
        out_specs=cur,
        out_shape=jax.ShapeDtypeStruct((n_rows, A_WIDTH), F32),
        scratch_shapes=[pltpu.VMEM((2 * ATT_Q_BLOCK, A_WIDTH), BF16), pltpu.VMEM((2 * ATT_Q_BLOCK, A_WIDTH), BF16)],
        compiler_params=_params(("parallel", "arbitrary")),
        name="attn_prompt",
    )(q, k, k, v, v, bias)


def _attn_sample_kernel(q_ref, kn_ref, vn_ref, kc_ref, vc_ref, bias_c_ref, bias_n_ref, o_ref):
    scale = A_HEAD_DIM ** -0.5
    for hp in range(A_HEADS // 2):
        c0 = hp * LANES
        q2 = q_ref[:, c0:c0 + LANES] * scale
        keys = [kc_ref[:, c0:c0 + LANES].astype(BF16), kn_ref[:, c0:c0 + LANES].astype(BF16)]
        vals = [vc_ref[:, c0:c0 + LANES].astype(BF16), vn_ref[:, c0:c0 + LANES].astype(BF16)]
        bias = [[bias_c_ref[2 * hp + e], bias_n_ref[2 * hp + e]] for e in range(2)]
        o_ref[:, c0:c0 + LANES] = _softmax_pv(q2, keys, vals, bias, [None, None])


def _attn_sample(q, k, v, k_cache, v_cache, bias_c, bias_n, n_batch):
    n_rows = q.shape[0]
    s_len = n_rows // n_batch
    n_cache = k_cache.shape[1]
    new = pl.BlockSpec((s_len, A_WIDTH), lambda b: (b, 0))
    cache = pl.BlockSpec((None, n_cache, A_WIDTH), lambda b: (b, 0, 0))
    return pl.pallas_call(
        _attn_sample_kernel,
        grid=(n_batch,),
        in_specs=[new, new, new, cache, cache, _const_spec(bias_c.shape), _const_spec(bias_n.shape)],
        out_specs=new,
        out_shape=jax.ShapeDtypeStruct((n_rows, A_WIDTH), F32),
        compiler_params=_params(("parallel",)),
        name="attn_sample",
    )(q, k, v, k_cache, v_cache, bias_c, bias_n)


def _rel_bias_blocks(table, n_q, n_k, q_offset, banded):
    rel_max = q_offset + n_q - 1
    edge = np.clip(rel_max - np.arange(n_q + n_k - 1), -MAX_REL, MAX_REL) + MAX_REL
    vec = jnp.take(table.astype(F32), jnp.asarray(edge), axis=1)
    bias = jnp.stack([vec[:, n_q - 1 - i:n_q - 1 - i + n_k] for i in range(n_q)], axis=1)
    if not banded:
        return bias
    i = np.arange(n_q)[:, None]
    jj = np.arange(n_k)[None, :] - (i // CHUNK) * CHUNK
    inband = (jj >= 0) & (jj < A_WINDOW + CHUNK)
    return jnp.where(jnp.asarray(inband), bias, NEG_INF)


def _inv_unit_lower(mats, row, col):
    base = 8
    diag = _blk(row, base) == _blk(col, base)
    eye = jnp.where(row == col, 1.0, 0.0)
    a0 = [jnp.where(diag, a, 0.0) for a in mats]
    xs = [eye - a for a in a0]
    ps = [_dotx(a, a) for a in a0]
    xs = [x + _dotx(x, p) for x, p in zip(xs, ps)]
    ps = [_dotx(p, p) for p in ps]
    xs = [x + _dotx(x, p) for x, p in zip(xs, ps)]
    bs = base
    while bs < CHUNK:
        off = (_blk(row, 2 * bs) == _blk(col, 2 * bs)) & (_blk(row, bs) != _blk(col, bs))
        ts = [_dotx(jnp.where(off, a, 0.0), x) for a, x in zip(mats, xs)]
        xs = [x - _dotx(x, t) for x, t in zip(xs, ts)]
        bs *= 2
    return xs


def _delta_kernel(qkvb_ref, gate_ref, beta_ref, a_ref, cst_ref, s0_ref, cw_ref, cb_ref, alog_ref, dtb_ref, nw_ref,
                  o_ref, cout_ref, sout_ref, xbuf, s_ref, ob_ref, *, t_valid, n_rows):
    j = pl.program_id(1)
    n_blk = n_rows // PAIR
    hd = B_HEAD_DIM
    low = lax.broadcasted_iota(jnp.int32, (PAIR, LANES), 1) < hd
    low_rows = lax.broadcasted_iota(jnp.int32, (n_rows, LANES), 1) < hd
    low_s = lax.broadcasted_iota(jnp.int32, (hd, LANES), 1) < hd
    zeros_s = jnp.zeros((hd, LANES), F32)

    @pl.when(j == 0)
    def _():
        xbuf[0:SUBLANES, :] = cst_ref[0]
        for h in range(B_HEADS):
            s0 = s0_ref[0, h]
            s_ref[h] = jnp.concatenate([s0, jnp.zeros_like(s0)] if h % 2 == 0 else [jnp.zeros_like(s0), s0], axis=1)

    xbuf[SUBLANES:SUBLANES + n_rows, :] = qkvb_ref[...]
    conv = cb_ref[...]
    for w in range(CONV_W):
        start = SUBLANES - (CONV_W - 1) + w
        conv = conv + xbuf[start:start + n_rows, :] * cw_ref[w:w + 1, :]
    conv = _silu(conv)
    t_last = n_rows - PAIR + t_valid
    tail = xbuf[t_last:t_last + SUBLANES, :]
    cout_ref[0] = tail
    xbuf[0:SUBLANES, :] = tail

    hrow = lax.broadcasted_iota(jnp.int32, (B_WIDTH, B_WIDTH), 0)
    hcol = lax.broadcasted_iota(jnp.int32, (B_WIDTH, B_WIDTH), 1)
    head_ones = jnp.where(_blk(hrow, B_HEAD_DIM) == _blk(hcol, B_HEAD_DIM), 1.0, 0.0).astype(BF16)

    def l2n(t):
        return t * lax.rsqrt(_dot_f32_by_exact(t * t, head_ones) + RMS_EPS)

    qn = l2n(conv[:, 0:B_WIDTH]) * (B_HEAD_DIM ** -0.5)
    kn = l2n(conv[:, B_WIDTH:2 * B_WIDTH])
    vv = conv[:, 2 * B_WIDTH:3 * B_WIDTH]
    beta = jax.nn.sigmoid(beta_ref[...])
    g = -jnp.exp(alog_ref[...]) * _softplus(a_ref[...] + dtb_ref[...])
    if t_valid < PAIR:
        def live(t):
            return jnp.where(lax.broadcasted_iota(jnp.int32, t.shape, 0) < t_valid, t, 0.0)
        kn, vv, beta, g = live(kn), live(vv), live(beta), live(g)

    rr = lax.broadcasted_iota(jnp.int32, (n_rows, n_rows), 0)
    cc = lax.broadcasted_iota(jnp.int32, (n_rows, n_rows), 1)
    ltri = jnp.where((_blk(rr, CHUNK) == _blk(cc, CHUNK)) & (rr >= cc), 1.0, 0.0).astype(BF16)
    gc = _dot_exact_by_f32(ltri, g)
    n_chunks = n_rows // CHUNK
    g_last = [gc[(c + 1) * CHUNK - 1:(c + 1) * CHUNK, :] for c in range(n_chunks)]
    gl = jnp.concatenate([jnp.broadcast_to(t, (CHUNK, t.shape[1])) for t in g_last], axis=0)
    egc = jnp.exp(gc)
    erest = jnp.exp(gl - gc)
    egl = [jnp.exp(t) for t in g_last]

    def pair_lanes(t, p):
        return jnp.where(low_rows, t[:, (2 * p) * LANES:(2 * p + 1) * LANES], t[:, (2 * p + 1) * LANES:(2 * p + 2) * LANES])

    n_pairs = B_HEADS // 2
    k_p, q_p, kb_p, vb_p, kbg_r, qd_r, kd_p = [], [], [], [], [], [], []
    for p in range(n_pairs):
        ls = slice(p * LANES, (p + 1) * LANES)
        k, q, v, b = kn[:, ls], qn[:, ls], vv[:, ls], beta[:, ls]
        eg = pair_lanes(egc, p)
        k_p.append(k)
        q_p.append(q)
        kb_p.append(k * b)
        vb_p.append(v * b)
        kbg_r.append(pltpu.roll(k * b * eg, hd, axis=1))
        qd_r.append(pltpu.roll(q * eg, hd, axis=1))
        kd_p.append(k * pair_lanes(erest, p))

    row = lax.broadcasted_iota(jnp.int32, (PAIR, PAIR), 0)
    col = lax.broadcasted_iota(jnp.int32, (PAIR, PAIR), 1)
    same = _blk(row, CHUNK) == _blk(col, CHUNK)
    tril = same & (row >= col)
    strict = same & (row > col)
    first = row < CHUNK
    units = [(n, h) for n in range(n_blk) for h in range(B_HEADS)]

    def own(h, t, other):
        return jnp.where(low, t, other) if h % 2 == 0 else jnp.where(low, other, t)

    decays, a_lows = [], []
    for n, h in units:
        rs = slice(n * PAIR, (n + 1) * PAIR)
        gcol = gc[rs, h * LANES:(h + 1) * LANES]
        diff = gcol - gcol.T
        decays.append(jnp.where(tril, jnp.exp(jnp.where(tril, diff, 0.0)), 0.0))
    for (n, h), decay in zip(units, decays):
        rs = slice(n * PAIR, (n + 1) * PAIR)
        kk = _bdot_nt(own(h, kb_p[h // 2][rs], 0.0), k_p[h // 2][rs])
        a_lows.append(jnp.where(strict, kk * decay, 0.0))
    x_inv = _inv_unit_lower(a_lows, row, col)
    uws = [_dotx(x, own(h, vb_p[h // 2][n * PAIR:(n + 1) * PAIR], kbg_r[h // 2][n * PAIR:(n + 1) * PAIR]))
           for (n, h), x in zip(units, x_inv)]
    qus = []
    for (n, h), decay, uw in zip(units, decays, uws):
        rs = slice(n * PAIR, (n + 1) * PAIR)
        qk = jnp.where(tril, _bdot_nt(own(h, q_p[h // 2][rs], 0.0), k_p[h // 2][rs]) * decay, 0.0)
        qus.append(_bdot(qk, uw))
    kus = {}
    for n in range(n_blk):
        for p in range(n_pairs):
            uw_e = [uws[n * B_HEADS + 2 * p + e] for e in range(2)]
            rhs = jnp.concatenate([jnp.where(first if c == 0 else ~first, uw_e[e], 0.0)
                                   for e in range(2) for c in range(2)], axis=1)
            kus[(n, p)] = _bdot(kd_p[p][n * PAIR:(n + 1) * PAIR].T, rhs)

    states = [s_ref[h] for h in range(B_HEADS)]
    for n in range(n_blk):
        for c in range(PAIR // CHUNK):
            r0 = n * PAIR + c * CHUNK
            outs = []
            for h in range(B_HEADS):
                p, e = divmod(h, 2)
                ku = kus[(n, p)][e * hd:(e + 1) * hd, (2 * e + c) * LANES:(2 * e + c + 1) * LANES]
                qu = qus[n * B_HEADS + h][c * CHUNK:(c + 1) * CHUNK, :]
                kq = jnp.concatenate([ku, qd_r[p][r0:r0 + CHUNK] - qu], axis=0)
                s_ext = jnp.concatenate([zeros_s, states[h]] if e == 0 else [states[h], zeros_s], axis=0)
                r = _bdot(kq, s_ext)
                gt = egl[n * (PAIR // CHUNK) + c][:, h * LANES:(h + 1) * LANES]
                new = gt * states[h] - r[:hd] + ku
                states[h] = jnp.where(low_s, new, 0.0) if e == 0 else jnp.where(low_s, 0.0, new)
                outs.append(r[hd:] + qu)
            for p in range(n_pairs):
                ob_ref[r0:r0 + CHUNK, p * LANES:(p + 1) * LANES] = jnp.where(low_s, outs[2 * p], outs[2 * p + 1])
    for h in range(B_HEADS):
        s_ref[h] = states[h]

    ob = ob_ref[...]
    ms = _dot_f32_by_exact(ob * ob, head_ones) * (1.0 / hd)
    o_ref[...] = ob * lax.rsqrt(ms + RMS_EPS) * nw_ref[...] * _silu(gate_ref[...])

    @pl.when(j == pl.num_programs(1) - 1)
    def _():
        for h in range(B_HEADS):
            sout_ref[0, h] = s_ref[h][:, (h % 2) * hd:(h % 2 + 1) * hd]


def _delta(qkvb, gate, beta, a, conv_state, s0, lw, n_batch, t_valid, rows_per_step):
    n_rows = qkvb.shape[0]
    per_b = n_rows // n_batch // rows_per_step

    def rows(width):
        return pl.BlockSpec((rows_per_step, width), lambda b, j: (b * per_b + j, 0))

    cst = pl.BlockSpec((1, SUBLANES, 3 * B_WIDTH), lambda b, j: (b, 0, 0))
    st = pl.BlockSpec((1, B_HEADS, B_HEAD_DIM, B_HEAD_DIM), lambda b, j: (b, 0, 0, 0))
    consts = [lw["conv_w"], lw["conv_b"], lw["a_log"], lw["dt_bias"], lw["norm_w"]]
    return pl.pallas_call(
        functools.partial(_delta_kernel, t_valid=t_valid, n_rows=rows_per_step),
        grid=(n_batch, per_b),
        in_specs=[rows(3 * B_WIDTH), rows(B_WIDTH), rows(B_WIDTH), rows(B_HEADS * LANES), cst, st]
        + [_const_spec(c.shape) for c in consts],
        out_specs=[rows(B_WIDTH), cst, st],
        out_shape=[jax.ShapeDtypeStruct((n_rows, B_WIDTH), F32),
                   jax.ShapeDtypeStruct((n_batch, SUBLANES, 3 * B_WIDTH), F32),
                   jax.ShapeDtypeStruct((n_batch, B_HEADS, B_HEAD_DIM, B_HEAD_DIM), F32)],
        scratch_shapes=[pltpu.VMEM((SUBLANES + rows_per_step, 3 * B_WIDTH), F32),
                        pltpu.VMEM((B_HEADS, B_HEAD_DIM, LANES), F32),
                        pltpu.VMEM((rows_per_step, B_WIDTH), F32)],
        compiler_params=_params(("parallel", "arbitrary")),
        name="delta",
    )(qkvb, gate, beta, a, conv_state, s0, *consts)


def _s5_kernel(u_ref, h0r_ref, h0i_ref, lr_ref, li_ref, bre_ref, bim_ref, cre_ref, cim_ref, d_ref, gw_ref, gb_ref,
               o_ref, hro_ref, hio_ref, xr, xi, hs, *, t_block):
    tb = pl.program_id(1)

    @pl.when(tb == 0)
    def _():
        hs[0] = h0r_ref[0]
        hs[1] = h0i_ref[0]

    u = u_ref[0]
    ub = u.astype(BF16)
    xr[...] = jnp.dot(ub, bre_ref[...], preferred_element_type=F32)
    xi[...] = jnp.dot(ub, bim_ref[...], preferred_element_type=F32)
    lam_r = jnp.broadcast_to(lr_ref[...], (S5_BATCH, C_LANES))
    lam_i = jnp.broadcast_to(li_ref[...], (S5_BATCH, C_LANES))

    def step(t, carry):
        h_r, h_i = carry
        rows = pl.ds(pl.multiple_of(t * S5_BATCH, S5_BATCH), S5_BATCH)
        n_r = lam_r * h_r - lam_i * h_i + xr[rows, :]
        n_i = lam_r * h_i + lam_i * h_r + xi[rows, :]
        xr[rows, :] = n_r
        xi[rows, :] = n_i
        return n_r, n_i

    h_r, h_i = lax.fori_loop(0, t_block, step, (hs[0], hs[1]))
    hs[0] = h_r
    hs[1] = h_i
    hro_ref[0] = h_r
    hio_ref[0] = h_i
    y = (jnp.dot(xr[...].astype(BF16), cre_ref[...], preferred_element_type=F32)
         - jnp.dot(xi[...].astype(BF16), cim_ref[...], preferred_element_type=F32) + d_ref[...] * u)
    z = _gelu_tanh(y)
    o_ref[0] = z * jax.nn.sigmoid(jnp.dot(z.astype(BF16), gw_ref[...], preferred_element_type=F32) + gb_ref[...])


def _s5(u, h0_re, h0_im, lw, t_block):
    n_groups, n_rows, _ = u.shape
    rows_blk = t_block * S5_BATCH
    n_tb = n_rows // rows_blk
    u_spec = pl.BlockSpec((1, rows_blk, C_WIDTH), lambda g, t: (g, t, 0))
    h_spec = pl.BlockSpec((1, S5_BATCH, C_LANES), lambda g, t: (g, 0, 0))
    consts = [lw["lam_re"], lw["lam_im"], lw["b_re"], lw["b_im"], lw["c_re"], lw["c_im"], lw["c_d"],
              lw["glu_w"], lw["glu_b"]]
    return pl.pallas_call(
        functools.partial(_s5_kernel, t_block=t_block),
        grid=(n_groups, n_tb),
        in_specs=[u_spec, h_spec, h_spec] + [_const_spec(c.shape) for c in consts],
        out_specs=[u_spec, h_spec, h_spec],
        out_shape=[jax.ShapeDtypeStruct(u.shape, F32),
                   jax.ShapeDtypeStruct(h0_re.shape, F32), jax.ShapeDtypeStruct(h0_im.shape, F32)],
        scratch_shapes=[pltpu.VMEM((rows_blk, C_LANES), F32), pltpu.VMEM((rows_blk, C_LANES), F32),
                        pltpu.VMEM((2, S5_BATCH, C_LANES), F32)],
        compiler_params=_params(("parallel", "arbitrary")),
        name="s5",
    )(u, h0_re, h0_im, *consts)


def _post_kernel(x_ref, a_ref, b_ref, c_ref, wo_ref, g1_ref, b1_ref, wu_ref, bu_ref, wd_ref, g2_ref, b2_ref, o_ref):
    mix = (jnp.dot(a_ref[...].astype(BF16), wo_ref[0:A_WIDTH, :], preferred_element_type=F32)
           + jnp.dot(b_ref[...].astype(BF16), wo_ref[A_WIDTH:A_WIDTH + B_WIDTH, :], preferred_element_type=F32)
           + jnp.dot(c_ref[...].astype(BF16), wo_ref[A_WIDTH + B_WIDTH:, :], preferred_element_type=F32))
    x1 = _layer_norm(DN_ALPHA * x_ref[...] + mix, g1_ref[...], b1_ref[...])
    x1b = x1.astype(BF16)
    acc = jnp.zeros(x1.shape, F32)
    for s in range(D_FF // FF_SLAB):
        cs = slice(s * FF_SLAB, (s + 1) * FF_SLAB)
        hid = jnp.dot(x1b, wu_ref[:, cs], preferred_element_type=F32) + bu_ref[:, cs]
        hid = jnp.square(jnp.maximum(hid, 0.0))
        acc = acc + jnp.dot(hid.astype(BF16), wd_ref[cs, :], preferred_element_type=F32)
    o_ref[...] = _layer_norm(DN_ALPHA * x1 + acc, g2_ref[...], b2_ref[...])


def _post(x2d, out_a, out_b, out_c, lw, n_batch, c_time_major):
    n_rows = x2d.shape[0]
    tm = min(ROW_TILE, n_rows)

    def rows(width):
        return pl.BlockSpec((tm, width), lambda i: (i, 0))

    if c_time_major:
        per_b = n_rows // n_batch // tm
        c_spec = pl.BlockSpec((tm, C_WIDTH), lambda i: (i % per_b, i // per_b))
    else:
        c_spec = rows(C_WIDTH)
    consts = [lw["w_out"], lw["ln1_g"], lw["ln1_b"], lw["w_up"], lw["b_up"], lw["w_down"], lw["ln2_g"], lw["ln2_b"]]
    return pl.pallas_call(
        _post_kernel,
        grid=(n_rows // tm,),
        in_specs=[rows(D_MODEL), rows(A_WIDTH), rows(B_WIDTH), c_spec] + [_const_spec(c.shape) for c in consts],
        out_specs=rows(D_MODEL),
        out_shape=jax.ShapeDtypeStruct((n_rows, D_MODEL), F32),
        compiler_params=_params(("parallel",)),
        name="post",
    )(x2d, out_a, out_b, out_c, *consts)


def _layer_weights(l, w_in, a_rel_bias, b_conv_w, b_conv_b, b_a_log, b_dt_bias, b_norm_w, c_a_re, c_a_im, c_log_dt,
                   c_b_re, c_b_im, c_c_re, c_c_im, c_d, c_glu_w, c_glu_b, w_out, ln1_g, ln1_b, w_up, b_up, w_down,
                   ln2_g, ln2_b, s_len, n_cache):
    wi = w_in[l]
    o_q, o_k, o_v = 0, A_WIDTH, 2 * A_WIDTH
    o_qkvb = 3 * A_WIDTH
    o_beta = o_qkvb + 3 * B_WIDTH
    o_a = o_beta + B_HEADS
    o_gate = o_a + B_HEADS
    o_uc = o_gate + B_WIDTH
    per_head = lambda cols: jnp.repeat(cols, B_HEAD_DIM, axis=-1)
    per_group = lambda cols: jnp.repeat(cols, LANES, axis=-1)
    w_proj = jnp.concatenate([
        wi[:, o_q:o_q + A_WIDTH], wi[:, o_k:o_k + A_WIDTH], wi[:, o_v:o_v + A_WIDTH],
        wi[:, o_qkvb:o_qkvb + 3 * B_WIDTH], wi[:, o_gate:o_gate + B_WIDTH], wi[:, o_uc:o_uc + C_WIDTH],
        per_head(wi[:, o_beta:o_beta + B_HEADS]), per_group(wi[:, o_a:o_a + B_HEADS])], axis=1).astype(BF16)

    table = a_rel_bias[l]
    bias_sample = _rel_bias_blocks(table, s_len, n_cache + s_len, n_cache, banded=False)

    a_re, a_im = c_a_re[l].astype(F32), c_a_im[l].astype(F32)
    dt = jnp.exp(c_log_dt[l].astype(F32))[:, None]
    mag = jnp.exp(dt * a_re)
    lam_re, lam_im = mag * jnp.cos(dt * a_im), mag * jnp.sin(dt * a_im)
    den = a_re * a_re + a_im * a_im
    coef_re = ((lam_re - 1.0) * a_re + lam_im * a_im) / den
    coef_im = (lam_im * a_re - (lam_re - 1.0) * a_im) / den
    bre, bim = c_b_re[l].astype(F32), c_b_im[l].astype(F32)
    bb_re = coef_re[..., None] * bre - coef_im[..., None] * bim
    bb_im = coef_re[..., None] * bim + coef_im[..., None] * bre
    eye = jnp.eye(C_GROUPS, dtype=F32)
    in_bd = lambda t: jnp.einsum("gph,gk->ghkp", t, eye).reshape(C_WIDTH, C_LANES).astype(BF16)
    out_bd = lambda t: jnp.einsum("ghp,gk->gpkh", t.astype(F32), eye).reshape(C_LANES, C_WIDTH).astype(BF16)
    row = lambda t: t.astype(F32).reshape(1, -1)

    return {
        "w_proj": w_proj,
        "bias_prompt": _rel_bias_blocks(table, ATT_Q_SUB, ATT_K_SUB, A_WINDOW, banded=True),
        "bias_cache": bias_sample[:, :, :n_cache], "bias_new": bias_sample[:, :, n_cache:],
        "conv_w": b_conv_w[l].astype(F32), "conv_b": row(b_conv_b[l]),
        "a_log": row(per_group(b_a_log[l])), "dt_bias": row(per_group(b_dt_bias[l])),
        "norm_w": row(jnp.tile(b_norm_w[l], B_HEADS)),
        "lam_re": row(lam_re), "lam_im": row(lam_im), "b_re": in_bd(bb_re), "b_im": in_bd(bb_im),
        "c_re": out_bd(c_c_re[l]), "c_im": out_bd(c_c_im[l]), "c_d": row(c_d[l]),
        "glu_w": c_glu_w[l].astype(BF16), "glu_b": row(c_glu_b[l]),
        "w_out": w_out[l].astype(BF16), "ln1_g": row(ln1_g[l]), "ln1_b": row(ln1_b[l]),
        "w_up": w_up[l].astype(BF16), "b_up": row(b_up[l]), "w_down": w_down[l].astype(BF16),
        "ln2_g": row(ln2_g[l]), "ln2_b": row(ln2_b[l]),
    }


def _pad_conv_state(conv_buf):
    return jnp.pad(conv_buf.astype(F32), ((0, 0), (SUBLANES - (CONV_W - 1), 0), (0, 0)))


def _prompt_layer(x2d, lw, n_batch):
    t_len = x2d.shape[0] // n_batch
    q, k, v, qkvb, gate, uc, beta, a = _proj(x2d, lw["w_proj"], n_batch, uc_time_major=True)
    out_a = _attn_prompt(q, k, v, lw["bias_prompt"], n_batch)
    out_b, conv_o, s_new = _delta(
        qkvb, gate, beta, a, jnp.zeros((n_batch, SUBLANES, 3 * B_WIDTH), F32),
        jnp.zeros((n_batch, B_HEADS, B_HEAD_DIM, B_HEAD_DIM), F32), lw, n_batch, t_valid=PAIR, rows_per_step=DELTA_ROWS)
    zeros_h = jnp.zeros((1, S5_BATCH, C_LANES), F32)
    out_c, h_re, h_im = _s5(uc.reshape(1, t_len * n_batch, C_WIDTH), zeros_h, zeros_h, lw, S5_TIME_BLOCK)
    y = _post(x2d, out_a, out_b, out_c.reshape(t_len, n_batch * C_WIDTH), lw, n_batch, c_time_major=True)
    heads = lambda t: t.reshape(n_batch, t_len, A_HEADS, A_HEAD_DIM)[:, -A_WINDOW:]
    state = lambda t: t.reshape(n_batch, C_GROUPS, C_STATE)
    return y, (heads(k), heads(v), conv_o[:, -(CONV_W - 1):], s_new, state(h_re), state(h_im))


def _sample_layer(x2d, lw, n_batch, k_cache, v_cache, conv_buf, s0, h0_re, h0_im):
    s_len = x2d.shape[0] // n_batch
    n_groups = n_batch // S5_BATCH
    n_cache = k_cache.shape[1]
    q, k, v, qkvb, gate, uc, beta, a = _proj(x2d, lw["w_proj"], n_batch, uc_time_major=False)
    out_a = _attn_sample(q, k, v, k_cache.reshape(n_batch, n_cache, A_WIDTH), v_cache.reshape(n_batch, n_cache, A_WIDTH),
                         lw["bias_cache"], lw["bias_new"], n_batch)

    def pad_rows(t):
        t = t.reshape(n_batch, s_len, -1)
        return jnp.pad(t, ((0, 0), (0, PAIR - s_len), (0, 0))).reshape(n_batch * PAIR, -1)

    out_b, conv_o, s_new = _delta(pad_rows(qkvb), pad_rows(gate), pad_rows(beta), pad_rows(a),
                                  _pad_conv_state(conv_buf), s0.astype(F32), lw, n_batch, t_valid=s_len, rows_per_step=PAIR)
    out_b = out_b.reshape(n_batch, PAIR, B_WIDTH)[:, :s_len].reshape(n_batch * s_len, B_WIDTH)

    u = uc.reshape(n_groups, S5_BATCH, s_len, C_WIDTH).transpose(0, 2, 1, 3).reshape(n_groups, s_len * S5_BATCH, C_WIDTH)
    grp = lambda t: t.astype(F32).reshape(n_groups, S5_BATCH, C_LANES)
    out_c, h_re, h_im = _s5(u, grp(h0_re), grp(h0_im), lw, s_len)
    out_c = out_c.reshape(n_groups, s_len, S5_BATCH, C_WIDTH).transpose(0, 2, 1, 3).reshape(n_batch * s_len, C_WIDTH)
    y = _post(x2d, out_a, out_b, out_c, lw, n_batch, c_time_major=False)
    heads = lambda t: t.reshape(n_batch, s_len, A_HEADS, A_HEAD_DIM)
    state = lambda t: t.reshape(n_batch, C_GROUPS, C_STATE)
    return y, (heads(k), heads(v), conv_o[:, -(CONV_W - 1):], s_new, state(h_re), state(h_im))


def kernel(x_prompt, x_sample, cache_a_k, cache_a_v, state_b_conv, state_b_ssm, state_c_re, state_c_im, w_in, a_rel_bias, b_conv_w, b_conv_b, b_a_log, b_dt_bias, b_norm_w, c_a_re, c_a_im, c_log_dt, c_b_re, c_b_im, c_c_re, c_c_im, c_d, c_glu_w, c_glu_b, w_out, ln1_g, ln1_b, w_up, b_up, w_down, ln2_g, ln2_b):
    n_p, t_p, _ = x_prompt.shape
    n_s, t_s, _ = x_sample.shape
    n_cache = cache_a_k.shape[2]
    yp = x_prompt.reshape(n_p * t_p, D_MODEL)
    ys = x_sample.reshape(n_s * t_s, D_MODEL)
    p_out, s_out = [], []
    for l in range(DEPTH):
        lw = _layer_weights(l, w_in, a_rel_bias, b_conv_w, b_conv_b, b_a_log, b_dt_bias, b_norm_w, c_a_re, c_a_im,
                            c_log_dt, c_b_re, c_b_im, c_c_re, c_c_im, c_d, c_glu_w, c_glu_b, w_out, ln1_g, ln1_b,
                            w_up, b_up, w_down, ln2_g, ln2_b, t_s, n_cache)
        yp, st_p = _prompt_layer(yp, lw, n_p)
        ys, st_s = _sample_layer(ys, lw, n_s, cache_a_k[l], cache_a_v[l], state_b_conv[l], state_b_ssm[l],
                                 state_c_re[l], state_c_im[l])
        p_out.append(st_p)
        s_out.append(st_s)
    stack = lambda outs, i: jnp.stack([o[i] for o in outs])
    return (yp.reshape(n_p, t_p, D_MODEL), ys.reshape(n_s, t_s, D_MODEL),
            *[stack(p_out, i) for i in range(6)], *[stack(s_out, i) for i in range(6)])
```

```python
import functools

import jax
import jax.numpy as jnp
import numpy as np
from jax import lax
from jax.experimental import pallas as pl
from jax.experimental.pallas import tpu as pltpu

F32 = jnp.float32
BF16 = jnp.bfloat16

D_MODEL = 1024
DEPTH = 2
CHUNK = 64
A_HEADS = 8
A_HEAD_DIM = 64
A_WIDTH = A_HEADS * A_HEAD_DIM
A_WINDOW = 8 * CHUNK
MAX_REL = 256
B_HEADS = 4
B_HEAD_DIM = 64
B_WIDTH = B_HEADS * B_HEAD_DIM
CONV_W = 4
C_WIDTH = D_MODEL - A_WIDTH - B_WIDTH
C_GROUP = 16
C_GROUPS = C_WIDTH // C_GROUP
C_STATE = 64
C_LANES = C_GROUPS * C_STATE
D_FF = 4 * D_MODEL
DN_ALPHA = (2.0 * DEPTH) ** 0.25
LN_EPS = 1e-5
RMS_EPS = 1e-6
NEG_INF = -1e30

SUBLANES = 8
LANES = 128
VMEM_LIMIT_BYTES = 56 * 1024 * 1024

ROW_TILE = 512
ATT_Q_BLOCK = 512
ATT_Q_SUB = 2 * CHUNK
ATT_K_SUB = ATT_Q_SUB + A_WINDOW
PAIR = 2 * CHUNK
DELTA_ROWS = 2 * PAIR
S5_TIME_BLOCK = 128
S5_BATCH = SUBLANES
FF_SLAB = 1024

_SEG = {}
_off = 0
for _name, _w in (("q", A_WIDTH), ("k", A_WIDTH), ("v", A_WIDTH), ("qkvb", 3 * B_WIDTH), ("gate", B_WIDTH),
                  ("uc", C_WIDTH), ("beta", B_WIDTH), ("a", B_HEADS * LANES)):
    _SEG[_name] = (_off, _off + _w)
    _off += _w
PROJ_WIDTH = _off


def _bdot(a, b):
    return jnp.dot(a.astype(BF16), b.astype(BF16), preferred_element_type=F32)


def _bdot_nt(a, b):
    return lax.dot_general(a.astype(BF16), b.astype(BF16), (((1,), (1,)), ((), ())), preferred_element_type=F32)


def _split3(a):
    hi = a.astype(BF16)
    r = a - hi.astype(F32)
    mid = r.astype(BF16)
    lo = (r - mid.astype(F32)).astype(BF16)
    return hi, mid, lo


def _dot_f32_by_exact(a, e):
    lhs = jnp.concatenate(_split3(a), axis=1)
    return jnp.dot(lhs, jnp.concatenate([e, e, e], axis=0), preferred_element_type=F32)


def _dot_exact_by_f32(e, a):
    rhs = jnp.concatenate(_split3(a), axis=0)
    return jnp.dot(jnp.concatenate([e, e, e], axis=1), rhs, preferred_element_type=F32)


def _dotx(a, b):
    a_hi = a.astype(BF16)
    a_lo = (a - a_hi.astype(F32)).astype(BF16)
    b_hi = b.astype(BF16)
    b_lo = (b - b_hi.astype(F32)).astype(BF16)
    lhs = jnp.concatenate([a_hi, a_hi, a_lo], axis=1)
    rhs = jnp.concatenate([b_hi, b_lo, b_hi], axis=0)
    return jnp.dot(lhs, rhs, preferred_element_type=F32)


def _blk(idx, size):
    return jnp.bitwise_and(idx, -size)


def _layer_norm(x, g, b):
    mu = jnp.mean(x, -1, keepdims=True)
    xc = x - mu
    var = jnp.mean(xc * xc, -1, keepdims=True)
    return xc * lax.rsqrt(var + LN_EPS) * g + b


def _silu(x):
    return x * jax.nn.sigmoid(x)


def _softplus(x):
    return jnp.maximum(x, 0.0) + jnp.log1p(jnp.exp(-jnp.abs(x)))


def _gelu_tanh(x):
    c = float(np.sqrt(2.0 / np.pi))
    return x * (0.5 * (1.0 + jnp.tanh(c * (x + 0.044715 * (x * x * x)))))


def _const_spec(shape):
    nd = len(shape)
    return pl.BlockSpec(shape, lambda *_: (0,) * nd, pipeline_mode=pl.Buffered(1))


def _params(semantics):
    return pltpu.CompilerParams(dimension_semantics=semantics, vmem_limit_bytes=VMEM_LIMIT_BYTES)


def _proj_kernel(x_ref, w_ref, q_ref, k_ref, v_ref, qkvb_ref, gate_ref, uc_ref, beta_ref, a_ref):
    xb = x_ref[...].astype(BF16)
    outs = {"q": q_ref, "k": k_ref, "v": v_ref, "qkvb": qkvb_ref, "gate": gate_ref, "uc": uc_ref,
            "beta": beta_ref, "a": a_ref}
    for name, ref in outs.items():
        lo, hi = _SEG[name]
        ref[...] = jnp.dot(xb, w_ref[:, lo:hi], preferred_element_type=F32)


def _proj(x2d, w, n_batch, uc_time_major):
    n_rows = x2d.shape[0]
    tm = min(ROW_TILE, n_rows)
    n_steps = n_rows // tm
    widths = {name: hi - lo for name, (lo, hi) in _SEG.items()}
    order = ("q", "k", "v", "qkvb", "gate", "uc", "beta", "a")
    out_shape, out_specs = [], []
    for name in order:
        wd = widths[name]
        if name == "uc" and uc_time_major:
            t_len = n_rows // n_batch
            per_b = t_len // tm
            out_shape.append(jax.ShapeDtypeStruct((t_len, n_batch * wd), F32))
            out_specs.append(pl.BlockSpec((tm, wd), lambda i: (i % per_b, i // per_b)))
        else:
            out_shape.append(jax.ShapeDtypeStruct((n_rows, wd), F32))
            out_specs.append(pl.BlockSpec((tm, wd), lambda i: (i, 0)))
    return pl.pallas_call(
        _proj_kernel,
        grid=(n_steps,),
        in_specs=[pl.BlockSpec((tm, D_MODEL), lambda i: (i, 0)), _const_spec(w.shape)],
        out_specs=out_specs,
        out_shape=out_shape,
        compiler_params=_params(("parallel",)),
        name="proj",
    )(x2d, w)


def _softmax_pv(q2s, key_segs, val_segs, bias_segs, valid_segs):
    m_rows = q2s[0].shape[0]
    low = lax.broadcasted_iota(jnp.int32, q2s[0].shape, 1) < A_HEAD_DIM
    qs = [jnp.concatenate([jnp.where(low, q2, 0.0), jnp.where(low, 0.0, q2)], axis=0).astype(BF16) for q2 in q2s]
    scores = []
    for q, ksegs, biases in zip(qs, key_segs, bias_segs):
        sc = []
        for kseg, bias, valid in zip(ksegs, biases, valid_segs):
            s = lax.dot_general(q, kseg, (((1,), (1,)), ((), ())), preferred_element_type=F32) + bias
            if valid is not None:
                s = jnp.where(valid, s, NEG_INF)
            sc.append(s)
        scores.append(sc)
    tops = [functools.reduce(jnp.maximum, [jnp.max(s, -1, keepdims=True) for s in sc]) for sc in scores]
    probs = [[jnp.exp(s - m) for s in sc] for sc, m in zip(scores, tops)]
    denoms = [functools.reduce(jnp.add, [jnp.sum(p, -1, keepdims=True) for p in pr]) for pr in probs]
    pvs = [functools.reduce(jnp.add, [jnp.dot(p.astype(BF16), vseg, preferred_element_type=F32)
                                      for p, vseg in zip(pr, vsegs)]) for pr, vsegs in zip(probs, val_segs)]
    outs = []
    for pv, denom in zip(pvs, denoms):
        pvn = pv * (1.0 / denom)
        outs.append(jnp.where(low, pvn[:m_rows], pvn[m_rows:]))
    return outs


def _attn_prompt_kernel(q_ref, kp_ref, kc_ref, vp_ref, vc_ref, bias_ref, o_ref, kk_ref, vv_ref):
    jb = pl.program_id(1)
    kk_ref[0:ATT_Q_BLOCK, :] = kp_ref[...].astype(BF16)
    kk_ref[ATT_Q_BLOCK:, :] = kc_ref[...].astype(BF16)
    vv_ref[0:ATT_Q_BLOCK, :] = vp_ref[...].astype(BF16)
    vv_ref[ATT_Q_BLOCK:, :] = vc_ref[...].astype(BF16)
    scale = A_HEAD_DIM ** -0.5
    for sub in range(ATT_Q_BLOCK // ATT_Q_SUB):
        r0 = sub * ATT_Q_SUB
        pos = lax.broadcasted_iota(jnp.int32, (1, ATT_K_SUB), 1) + ((jb - 1) * ATT_Q_BLOCK + r0)
        valid = pos >= 0
        pairs = range(A_HEADS // 2)
        cols = [slice(hp * LANES, (hp + 1) * LANES) for hp in pairs]
        outs = _softmax_pv([q_ref[r0:r0 + ATT_Q_SUB, c] * scale for c in cols],
                           [[kk_ref[r0:r0 + ATT_K_SUB, c]] for c in cols],
                           [[vv_ref[r0:r0 + ATT_K_SUB, c]] for c in cols],
                           [[bias_ref[hp]] for hp in pairs], [valid])
        for c, out in zip(cols, outs):
            o_ref[r0:r0 + ATT_Q_SUB, c] = out


def _attn_prompt(q, k, v, bias, n_batch):
    n_rows = q.shape[0]
    per_b = n_rows // n_batch // ATT_Q_BLOCK
    blk = (ATT_Q_BLOCK, A_WIDTH)
    cur = pl.BlockSpec(blk, lambda b, j: (b * per_b + j, 0))
    prev = pl.BlockSpec(blk, lambda b, j: (b * per_b + jnp.maximum(j - 1, 0), 0))
    return pl.pallas_call(
        _attn_prompt_kernel,
        grid=(n_batch, per_b),
        in_specs=[cur, prev, cur, prev, cur, _const_spec(bias.shape)],
        out_specs=cur,
        out_shape=jax.ShapeDtypeStruct((n_rows, A_WIDTH), F32),
        scratch_shapes=[pltpu.VMEM((2 * ATT_Q_BLOCK, A_WIDTH), BF16), pltpu.VMEM((2 * ATT_Q_BLOCK, A_WIDTH), BF16)],
        compiler_params=_params(("parallel", "arbitrary")),
        name="attn_prompt",
    )(q, k, k, v, v, bias)


def _attn_sample_kernel(q_ref, kn_ref, vn_ref, kc_ref, vc_ref, bias_c_ref, bias_n_ref, o_ref):
    scale = A_HEAD_DIM ** -0.5
    pairs = range(A_HEADS // 2)
    cols = [slice(hp * LANES, (hp + 1) * LANES) for hp in pairs]
    outs = _softmax_pv([q_ref[:, c] * scale for c in cols],
                       [[kc_ref[:, c].astype(BF16), kn_ref[:, c].astype(BF16)] for c in cols],
                       [[vc_ref[:, c].astype(BF16), vn_ref[:, c].astype(BF16)] for c in cols],
                       [[bias_c_ref[hp], bias_n_ref[hp]] for hp in pairs], [None, None])
    for c, out in zip(cols, outs):
        o_ref[:, c] = out


def _attn_sample(q, k, v, k_cache, v_cache, bias_c, bias_n, n_batch):
    n_rows = q.shape[0]
    s_len = n_rows // n_batch
    n_cache = k_cache.shape[1]
    new = pl.BlockSpec((s_len, A_WIDTH), lambda b: (b, 0))
    cache = pl.BlockSpec((None, n_cache, A_WIDTH), lambda b: (b, 0, 0))
    return pl.pallas_call(
        _attn_sample_kernel,
        grid=(n_batch,),
        in_specs=[new, new, new, cache, cache, _const_spec(bias_c.shape), _const_spec(bias_n.shape)],
        out_specs=new,
        out_shape=jax.ShapeDtypeStruct((n_rows, A_WIDTH), F32),
        compiler_params=_params(("parallel",)),
        name="attn_sample",
    )(q, k, v, k_cache, v_cache, bias_c, bias_n)


def _rel_bias_blocks(table, n_q, n_k, q_offset, banded):
    period = n_q + n_k - 1
    rel = q_offset - ((np.arange(period) + n_q - 1) % period - (n_q - 1))
    vec = jnp.take(table.astype(F32), jnp.asarray(np.clip(rel, -MAX_REL, MAX_REL) + MAX_REL), axis=1)
    n_heads = table.shape[0]
    bias = jnp.tile(vec, (1, n_q))[:, :n_q * (period - 1)].reshape(n_heads, n_q, period - 1)[:, :, :n_k]
    if banded:
        i = np.arange(n_q)[:, None]
        jj = np.arange(n_k)[None, :] - (i // CHUNK) * CHUNK
        inband = (jj >= 0) & (jj < A_WINDOW + CHUNK)
        bias = jnp.where(jnp.asarray(inband), bias, NEG_INF)
    return bias.reshape(n_heads // 2, 2 * n_q, n_k)


def _inv_unit_lower(mats, row, col):
    base = 8
    diag = _blk(row, base) == _blk(col, base)
    eye = jnp.where(row == col, 1.0, 0.0)
    a0 = [jnp.where(diag, a, 0.0) for a in mats]
    xs = [eye - a for a in a0]
    ps = [_dotx(a, a) for a in a0]
    xs = [x + _dotx(x, p) for x, p in zip(xs, ps)]
    ps = [_dotx(p, p) for p in ps]
    xs = [x + _dotx(x, p) for x, p in zip(xs, ps)]
    bs = base
    while bs < CHUNK:
        off = (_blk(row, 2 * bs) == _blk(col, 2 * bs)) & (_blk(row, bs) != _blk(col, bs))
        ts = [_dotx(jnp.where(off, a, 0.0), x) for a, x in zip(mats, xs)]
        xs = [x - _dotx(x, t) for x, t in zip(xs, ts)]
        bs *= 2
    return xs


def _delta_kernel(qkvb_ref, gate_ref, beta_ref, a_ref, cst_ref, s0_ref, cw_ref, cb_ref, alog_ref, dtb_ref, nw_ref,
                  o_ref, cout_ref, sout_ref, xbuf, s_ref, ob_ref, *, t_valid, n_rows):
    j = pl.program_id(1)
    n_blk = n_rows // PAIR
    hd = B_HEAD_DIM
    low = lax.broadcasted_iota(jnp.int32, (PAIR, LANES), 1) < hd
    low_rows = lax.broadcasted_iota(jnp.int32, (n_rows, LANES), 1) < hd
    low_s = lax.broadcasted_iota(jnp.int32, (hd, LANES), 1) < hd
    zeros_s = jnp.zeros((hd, LANES), F32)

    @pl.when(j == 0)
    def _():
        xbuf[0:SUBLANES, :] = cst_ref[0]
        for h in range(B_HEADS):
            s0 = s0_ref[0, h]
            s_ref[h] = jnp.concatenate([s0, jnp.zeros_like(s0)] if h % 2 == 0 else [jnp.zeros_like(s0), s0], axis=1)

    xbuf[SUBLANES:SUBLANES + n_rows, :] = qkvb_ref[...]
    conv = cb_ref[...]
    for w in range(CONV_W):
        start = SUBLANES - (CONV_W - 1) + w
        conv = conv + xbuf[start:start + n_rows, :] * cw_ref[w:w + 1, :]
    conv = _silu(conv)
    t_last = n_rows - PAIR + t_valid
    tail = xbuf[t_last:t_last + SUBLANES, :]
    cout_ref[0] = tail
    xbuf[0:SUBLANES, :] = tail

    hrow = lax.broadcasted_iota(jnp.int32, (B_WIDTH, B_WIDTH), 0)
    hcol = lax.broadcasted_iota(jnp.int32, (B_WIDTH, B_WIDTH), 1)
    head_ones = jnp.where(_blk(hrow, B_HEAD_DIM) == _blk(hcol, B_HEAD_DIM), 1.0, 0.0).astype(BF16)

    def l2n(t):
        return t * lax.rsqrt(_dot_f32_by_exact(t * t, head_ones) + RMS_EPS)

    qn = l2n(conv[:, 0:B_WIDTH]) * (B_HEAD_DIM ** -0.5)
    kn = l2n(conv[:, B_WIDTH:2 * B_WIDTH])
    vv = conv[:, 2 * B_WIDTH:3 * B_WIDTH]
    beta = jax.nn.sigmoid(beta_ref[...])
    g = -jnp.exp(alog_ref[...]) * _softplus(a_ref[...] + dtb_ref[...])
    if t_valid < PAIR:
        def live(t):
            return jnp.where(lax.broadcasted_iota(jnp.int32, t.shape, 0) < t_valid, t, 0.0)
        kn, vv, beta, g = live(kn), live(vv), live(beta), live(g)

    rr = lax.broadcasted_iota(jnp.int32, (n_rows, n_rows), 0)
    cc = lax.broadcasted_iota(jnp.int32, (n_rows, n_rows), 1)
    ltri = jnp.where((_blk(rr, CHUNK) == _blk(cc, CHUNK)) & (rr >= cc), 1.0, 0.0).astype(BF16)
    gc = _dot_exact_by_f32(ltri, g)
    n_chunks = n_rows // CHUNK
    g_last = [gc[(c + 1) * CHUNK - 1:(c + 1) * CHUNK, :] for c in range(n_chunks)]
    gl = jnp.concatenate([jnp.broadcast_to(t, (CHUNK, t.shape[1])) for t in g_last], axis=0)
    egc = jnp.exp(gc)
    erest = jnp.exp(gl - gc)
    egl = [jnp.exp(t) for t in g_last]

    def pair_lanes(t, p):
        return jnp.where(low_rows, t[:, (2 * p) * LANES:(2 * p + 1) * LANES], t[:, (2 * p + 1) * LANES:(2 * p + 2) * LANES])

    n_pairs = B_HEADS // 2
    k_p, q_p, kb_p, vb_p, kbg_r, qd_r, kd_p = [], [], [], [], [], [], []
    for p in range(n_pairs):
        ls = slice(p * LANES, (p + 1) * LANES)
        k, q, v, b = kn[:, ls], qn[:, ls], vv[:, ls], beta[:, ls]
        eg = pair_lanes(egc, p)
        k_p.append(k)
        q_p.append(q)
        kb_p.append(k * b)
        vb_p.append(v * b)
        kbg_r.append(pltpu.roll(k * b * eg, hd, axis=1))
        qd_r.append(pltpu.roll(q * eg, hd, axis=1))
        kd_p.append(k * pair_lanes(erest, p))

    row = lax.broadcasted_iota(jnp.int32, (PAIR, PAIR), 0)
    col = lax.broadcasted_iota(jnp.int32, (PAIR, PAIR), 1)
    same = _blk(row, CHUNK) == _blk(col, CHUNK)
    tril = same & (row >= col)
    strict = same & (row > col)
    first = row < CHUNK
    units = [(n, h) for n in range(n_blk) for h in range(B_HEADS)]

    def own(h, t, other):
        return jnp.where(low, t, other) if h % 2 == 0 else jnp.where(low, other, t)

    decays, a_lows = [], []
    for n, h in units:
        rs = slice(n * PAIR, (n + 1) * PAIR)
        gcol = gc[rs, h * LANES:(h + 1) * LANES]
        diff = gcol - gcol.T
        decays.append(jnp.where(tril, jnp.exp(jnp.where(tril, diff, 0.0)), 0.0))
    for (n, h), decay in zip(units, decays):
        rs = slice(n * PAIR, (n + 1) * PAIR)
        kk = _bdot_nt(own(h, kb_p[h // 2][rs], 0.0), k_p[h // 2][rs])
        a_lows.append(jnp.where(strict, kk * decay, 0.0))
    x_inv = _inv_unit_lower(a_lows, row, col)
    uws = [_dotx(x, own(h, vb_p[h // 2][n * PAIR:(n + 1) * PAIR], kbg_r[h // 2][n * PAIR:(n + 1) * PAIR]))
           for (n, h), x in zip(units, x_inv)]
    qus = []
    for (n, h), decay, uw in zip(units, decays, uws):
        rs = slice(n * PAIR, (n + 1) * PAIR)
        qk = jnp.where(tril, _bdot_nt(own(h, q_p[h // 2][rs], 0.0), k_p[h // 2][rs]) * decay, 0.0)
        qus.append(_bdot(qk, uw))
    kus = {}
    for n in range(n_blk):
        for p in range(n_pairs):
            uw_e = [uws[n * B_HEADS + 2 * p + e] for e in range(2)]
            rhs = jnp.concatenate([jnp.where(first if c == 0 else ~first, uw_e[e], 0.0)
                                   for e in range(2) for c in range(2)], axis=1)
            kus[(n, p)] = _bdot(kd_p[p][n * PAIR:(n + 1) * PAIR].T, rhs)

    states = [s_ref[h] for h in range(B_HEADS)]
    for n in range(n_blk):
        for c in range(PAIR // CHUNK):
            r0 = n * PAIR + c * CHUNK
            outs = []
            for h in range(B_HEADS):
                p, e = divmod(h, 2)
                ku = kus[(n, p)][e * hd:(e + 1) * hd, (2 * e + c) * LANES:(2 * e + c + 1) * LANES]
                qu = qus[n * B_HEADS + h][c * CHUNK:(c + 1) * CHUNK, :]
                kq = jnp.concatenate([ku, qd_r[p][r0:r0 + CHUNK] - qu], axis=0)
                s_ext = jnp.concatenate([zeros_s, states[h]] if e == 0 else [states[h], zeros_s], axis=0)
                r = _bdot(kq, s_ext)
                gt = egl[n * (PAIR // CHUNK) + c][:, h * LANES:(h + 1) * LANES]
                new = gt * states[h] - r[:hd] + ku
                states[h] = jnp.where(low_s, new, 0.0) if e == 0 else jnp.where(low_s, 0.0, new)
                outs.append(r[hd:] + qu)
            for p in range(n_pairs):
                ob_ref[r0:r0 + CHUNK, p * LANES:(p + 1) * LANES] = jnp.where(low_s, outs[2 * p], outs[2 * p + 1])
    for h in range(B_HEADS):
        s_ref[h] = states[h]

    ob = ob_ref[...]
    ms = _dot_f32_by_exact(ob * ob, head_ones) * (1.0 / hd)
    o_ref[...] = ob * lax.rsqrt(ms + RMS_EPS) * nw_ref[...] * _silu(gate_ref[...])

    @pl.when(j == pl.num_programs(1) - 1)
    def _():
        for h in range(B_HEADS):
            sout_ref[0, h] = s_ref[h][:, (h % 2) * hd:(h % 2 + 1) * hd]


def _delta(qkvb, gate, beta, a, conv_state, s0, lw, n_batch, t_valid, rows_per_step):
    n_rows = qkvb.shape[0]
    per_b = n_rows // n_batch // rows_per_step

    def rows(width):
        return pl.BlockSpec((rows_per_step, width), lambda b, j: (b * per_b + j, 0))

    cst = pl.BlockSpec((1, SUBLANES, 3 * B_WIDTH), lambda b, j: (b, 0, 0))
    st = pl.BlockSpec((1, B_HEADS, B_HEAD_DIM, B_HEAD_DIM), lambda b, j: (b, 0, 0, 0))
    consts = [lw["conv_w"], lw["conv_b"], lw["a_log"], lw["dt_bias"], lw["norm_w"]]
    return pl.pallas_call(
        functools.partial(_delta_kernel, t_valid=t_valid, n_rows=rows_per_step),
        grid=(n_batch, per_b),
        in_specs=[rows(3 * B_WIDTH), rows(B_WIDTH), rows(B_WIDTH), rows(B_HEADS * LANES), cst, st]
        + [_const_spec(c.shape) for c in consts],
        out_specs=[rows(B_WIDTH), cst, st],
        out_shape=[jax.ShapeDtypeStruct((n_rows, B_WIDTH), F32),
                   jax.ShapeDtypeStruct((n_batch, SUBLANES, 3 * B_WIDTH), F32),
                   jax.ShapeDtypeStruct((n_batch, B_HEADS, B_HEAD_DIM, B_HEAD_DIM), F32)],
        scratch_shapes=[pltpu.VMEM((SUBLANES + rows_per_step, 3 * B_WIDTH), F32),
                        pltpu.VMEM((B_HEADS, B_HEAD_DIM, LANES), F32),
                        pltpu.VMEM((rows_per_step, B_WIDTH), F32)],
        compiler_params=_params(("parallel", "arbitrary")),
        name="delta",
    )(qkvb, gate, beta, a, conv_state, s0, *consts)


def _s5_kernel(u_ref, h0r_ref, h0i_ref, lr_ref, li_ref, bre_ref, bim_ref, cre_ref, cim_ref, d_ref, gw_ref, gb_ref,
               o_ref, hro_ref, hio_ref, xr, xi, hs, *, t_block):
    tb = pl.program_id(1)

    @pl.when(tb == 0)
    def _():
        hs[0] = h0r_ref[0]
        hs[1] = h0i_ref[0]

    u = u_ref[0]
    ub = u.astype(BF16)
    xr[...] = jnp.dot(ub, bre_ref[...], preferred_element_type=F32)
    xi[...] = jnp.dot(ub, bim_ref[...], preferred_element_type=F32)
    lam_r = jnp.broadcast_to(lr_ref[...], (S5_BATCH, C_LANES))
    lam_i = jnp.broadcast_to(li_ref[...], (S5_BATCH, C_LANES))

    def step(t, carry):
        h_r, h_i = carry
        rows = pl.ds(pl.multiple_of(t * S5_BATCH, S5_BATCH), S5_BATCH)
        n_r = lam_r * h_r - lam_i * h_i + xr[rows, :]
        n_i = lam_r * h_i + lam_i * h_r + xi[rows, :]
        xr[rows, :] = n_r
        xi[rows, :] = n_i
        return n_r, n_i

    h_r, h_i = lax.fori_loop(0, t_block, step, (hs[0], hs[1]))
    hs[0] = h_r
    hs[1] = h_i
    hro_ref[0] = h_r
    hio_ref[0] = h_i
    y = (jnp.dot(xr[...].astype(BF16), cre_ref[...], preferred_element_type=F32)
         - jnp.dot(xi[...].astype(BF16), cim_ref[...], preferred_element_type=F32) + d_ref[...] * u)
    z = _gelu_tanh(y)
    o_ref[0] = z * jax.nn.sigmoid(jnp.dot(z.astype(BF16), gw_ref[...], preferred_element_type=F32) + gb_ref[...])


def _s5(u, h0_re, h0_im, lw, t_block):
    n_groups, n_rows, _ = u.shape
    rows_blk = t_block * S5_BATCH
    n_tb = n_rows // rows_blk
    u_spec = pl.BlockSpec((1, rows_blk, C_WIDTH), lambda g, t: (g, t, 0))
    h_spec = pl.BlockSpec((1, S5_BATCH, C_LANES), lambda g, t: (g, 0, 0))
    consts = [lw["lam_re"], lw["lam_im"], lw["b_re"], lw["b_im"], lw["c_re"], lw["c_im"], lw["c_d"],
              lw["glu_w"], lw["glu_b"]]
    return pl.pallas_call(
        functools.partial(_s5_kernel, t_block=t_block),
        grid=(n_groups, n_tb),
        in_specs=[u_spec, h_spec, h_spec] + [_const_spec(c.shape) for c in consts],
        out_specs=[u_spec, h_spec, h_spec],
        out_shape=[jax.ShapeDtypeStruct(u.shape, F32),
                   jax.ShapeDtypeStruct(h0_re.shape, F32), jax.ShapeDtypeStruct(h0_im.shape, F32)],
        scratch_shapes=[pltpu.VMEM((rows_blk, C_LANES), F32), pltpu.VMEM((rows_blk, C_LANES), F32),
                        pltpu.VMEM((2, S5_BATCH, C_LANES), F32)],
        compiler_params=_params(("parallel", "arbitrary")),
        name="s5",
    )(u, h0_re, h0_im, *consts)


def _post_kernel(x_ref, a_ref, b_ref, c_ref, wo_ref, g1_ref, b1_ref, wu_ref, bu_ref, wd_ref, g2_ref, b2_ref, o_ref):
    mix = (jnp.dot(a_ref[...].astype(BF16), wo_ref[0:A_WIDTH, :], preferred_element_type=F32)
           + jnp.dot(b_ref[...].astype(BF16), wo_ref[A_WIDTH:A_WIDTH + B_WIDTH, :], preferred_element_type=F32)
           + jnp.dot(c_ref[...].astype(BF16), wo_ref[A_WIDTH + B_WIDTH:, :], preferred_element_type=F32))
    x1 = _layer_norm(DN_ALPHA * x_ref[...] + mix, g1_ref[...], b1_ref[...])
    x1b = x1.astype(BF16)
    acc = jnp.zeros(x1.shape, F32)
    for s in range(D_FF // FF_SLAB):
        cs = slice(s * FF_SLAB, (s + 1) * FF_SLAB)
        hid = jnp.dot(x1b, wu_ref[:, cs], preferred_element_type=F32) + bu_ref[:, cs]
        hid = jnp.square(jnp.maximum(hid, 0.0))
        acc = acc + jnp.dot(hid.astype(BF16), wd_ref[cs, :], preferred_element_type=F32)
    o_ref[...] = _layer_norm(DN_ALPHA * x1 + acc, g2_ref[...], b2_ref[...])


def _post(x2d, out_a, out_b, out_c, lw, n_batch, c_time_major):
    n_rows = x2d.shape[0]
    tm = min(ROW_TILE, n_rows)

    def rows(width):
        return pl.BlockSpec((tm, width), lambda i: (i, 0))

    if c_time_major:
        per_b = n_rows // n_batch // tm
        c_spec = pl.BlockSpec((tm, C_WIDTH), lambda i: (i % per_b, i // per_b))
    else:
        c_spec = rows(C_WIDTH)
    consts = [lw["w_out"], lw["ln1_g"], lw["ln1_b"], lw["w_up"], lw["b_up"], lw["w_down"], lw["ln2_g"], lw["ln2_b"]]
    return pl.pallas_call(
        _post_kernel,
        grid=(n_rows // tm,),
        in_specs=[rows(D_MODEL), rows(A_WIDTH), rows(B_WIDTH), c_spec] + [_const_spec(c.shape) for c in consts],
        out_specs=rows(D_MODEL),
        out_shape=jax.ShapeDtypeStruct((n_rows, D_MODEL), F32),
        compiler_params=_params(("parallel",)),
        name="post",
    )(x2d, out_a, out_b, out_c, *consts)


def _layer_weights(l, w_in, a_rel_bias, b_conv_w, b_conv_b, b_a_log, b_dt_bias, b_norm_w, c_a_re, c_a_im, c_log_dt,
                   c_b_re, c_b_im, c_c_re, c_c_im, c_d, c_glu_w, c_glu_b, w_out, ln1_g, ln1_b, w_up, b_up, w_down,
                   ln2_g, ln2_b, s_len, n_cache):
    wi = w_in[l]
    o_q, o_k, o_v = 0, A_WIDTH, 2 * A_WIDTH
    o_qkvb = 3 * A_WIDTH
    o_beta = o_qkvb + 3 * B_WIDTH
    o_a = o_beta + B_HEADS
    o_gate = o_a + B_HEADS
    o_uc = o_gate + B_WIDTH
    per_head = lambda cols: jnp.repeat(cols, B_HEAD_DIM, axis=-1)
    per_group = lambda cols: jnp.repeat(cols, LANES, axis=-1)
    w_proj = jnp.concatenate([
        wi[:, o_q:o_q + A_WIDTH], wi[:, o_k:o_k + A_WIDTH], wi[:, o_v:o_v + A_WIDTH],
        wi[:, o_qkvb:o_qkvb + 3 * B_WIDTH], wi[:, o_gate:o_gate + B_WIDTH], wi[:, o_uc:o_uc + C_WIDTH],
        per_head(wi[:, o_beta:o_beta + B_HEADS]), per_group(wi[:, o_a:o_a + B_HEADS])], axis=1).astype(BF16)

    table = a_rel_bias[l]
    bias_sample = _rel_bias_blocks(table, s_len, n_cache + s_len, n_cache, banded=False)

    a_re, a_im = c_a_re[l].astype(F32), c_a_im[l].astype(F32)
    dt = jnp.exp(c_log_dt[l].astype(F32))[:, None]
    mag = jnp.exp(dt * a_re)
    lam_re, lam_im = mag * jnp.cos(dt * a_im), mag * jnp.sin(dt * a_im)
    den = a_re * a_re + a_im * a_im
    coef_re = ((lam_re - 1.0) * a_re + lam_im * a_im) / den
    coef_im = (lam_im * a_re - (lam_re - 1.0) * a_im) / den
    bre, bim = c_b_re[l].astype(F32), c_b_im[l].astype(F32)
    bb_re = coef_re[..., None] * bre - coef_im[..., None] * bim
    bb_im = coef_re[..., None] * bim + coef_im[..., None] * bre
    eye = jnp.eye(C_GROUPS, dtype=F32)
    in_bd = lambda t: jnp.einsum("gph,gk->ghkp", t, eye).reshape(C_WIDTH, C_LANES).astype(BF16)
    out_bd = lambda t: jnp.einsum("ghp,gk->gpkh", t.astype(F32), eye).reshape(C_LANES, C_WIDTH).astype(BF16)
    row = lambda t: t.astype(F32).reshape(1, -1)

    return {
        "w_proj": w_proj,
        "bias_prompt": _rel_bias_blocks(table, ATT_Q_SUB, ATT_K_SUB, A_WINDOW, banded=True),
        "bias_cache": bias_sample[:, :, :n_cache], "bias_new": bias_sample[:, :, n_cache:],
        "conv_w": b_conv_w[l].astype(F32), "conv_b": row(b_conv_b[l]),
        "a_log": row(per_group(b_a_log[l])), "dt_bias": row(per_group(b_dt_bias[l])),
        "norm_w": row(jnp.tile(b_norm_w[l], B_HEADS)),
        "lam_re": row(lam_re), "lam_im": row(lam_im), "b_re": in_bd(bb_re), "b_im": in_bd(bb_im),
        "c_re": out_bd(c_c_re[l]), "c_im": out_bd(c_c_im[l]), "c_d": row(c_d[l]),
        "glu_w": c_glu_w[l].astype(BF16), "glu_b": row(c_glu_b[l]),
        "w_out": w_out[l].astype(BF16), "ln1_g": row(ln1_g[l]), "ln1_b": row(ln1_b[l]),
        "w_up": w_up[l].astype(BF16), "b_up": row(b_up[l]), "w_down": w_down[l].astype(BF16),
        "ln2_g": row(ln2_g[l]), "ln2_b": row(ln2_b[l]),
    }


def _pad_conv_state(conv_buf):
    return jnp.pad(conv_buf.astype(F32), ((0, 0), (SUBLANES - (CONV_W - 1), 0), (0, 0)))


def _prompt_layer(x2d, lw, n_batch):
    t_len = x2d.shape[0] // n_batch
    q, k, v, qkvb, gate, uc, beta, a = _proj(x2d, lw["w_proj"], n_batch, uc_time_major=True)
    out_a = _attn_prompt(q, k, v, lw["bias_prompt"], n_batch)
    out_b, conv_o, s_new = _delta(
        qkvb, gate, beta, a, jnp.zeros((n_batch, SUBLANES, 3 * B_WIDTH), F32),
        jnp.zeros((n_batch, B_HEADS, B_HEAD_DIM, B_HEAD_DIM), F32), lw, n_batch, t_valid=PAIR, rows_per_step=DELTA_ROWS)
    zeros_h = jnp.zeros((1, S5_BATCH, C_LANES), F32)
    out_c, h_re, h_im = _s5(uc.reshape(1, t_len * n_batch, C_WIDTH), zeros_h, zeros_h, lw, S5_TIME_BLOCK)
    y = _post(x2d, out_a, out_b, out_c.reshape(t_len, n_batch * C_WIDTH), lw, n_batch, c_time_major=True)
    heads = lambda t: t.reshape(n_batch, t_len, A_HEADS, A_HEAD_DIM)[:, -A_WINDOW:]
    state = lambda t: t.reshape(n_batch, C_GROUPS, C_STATE)
    return y, (heads(k), heads(v), conv_o[:, -(CONV_W - 1):], s_new, state(h_re), state(h_im))


def _sample_layer(x2d, lw, n_batch, k_cache, v_cache, conv_buf, s0, h0_re, h0_im):
    s_len = x2d.shape[0] // n_batch
    n_groups = n_batch // S5_BATCH
    n_cache = k_cache.shape[1]
    q, k, v, qkvb, gate, uc, beta, a = _proj(x2d, lw["w_proj"], n_batch, uc_time_major=False)
    out_a = _attn_sample(q, k, v, k_cache.reshape(n_batch, n_cache, A_WIDTH), v_cache.reshape(n_batch, n_cache, A_WIDTH),
                         lw["bias_cache"], lw["bias_new"], n_batch)

    def pad_rows(t):
        t = t.reshape(n_batch, s_len, -1)
        return jnp.pad(t, ((0, 0), (0, PAIR - s_len), (0, 0))).reshape(n_batch * PAIR, -1)

    out_b, conv_o, s_new = _delta(pad_rows(qkvb), pad_rows(gate), pad_rows(beta), pad_rows(a),
                                  _pad_conv_state(conv_buf), s0.astype(F32), lw, n_batch, t_valid=s_len, rows_per_step=PAIR)
    out_b = out_b.reshape(n_batch, PAIR, B_WIDTH)[:, :s_len].reshape(n_batch * s_len, B_WIDTH)

    u = uc.reshape(n_groups, S5_BATCH, s_len, C_WIDTH).transpose(0, 2, 1, 3).reshape(n_groups, s_len * S5_BATCH, C_WIDTH)
    grp = lambda t: t.astype(F32).reshape(n_groups, S5_BATCH, C_LANES)
    out_c, h_re, h_im = _s5(u, grp(h0_re), grp(h0_im), lw, s_len)
    out_c = out_c.reshape(n_groups, s_len, S5_BATCH, C_WIDTH).transpose(0, 2, 1, 3).reshape(n_batch * s_len, C_WIDTH)
    y = _post(x2d, out_a, out_b, out_c, lw, n_batch, c_time_major=False)
    heads = lambda t: t.reshape(n_batch, s_len, A_HEADS, A_HEAD_DIM)
    state = lambda t: t.reshape(n_batch, C_GROUPS, C_STATE)
    return y, (heads(k), heads(v), conv_o[:, -(CONV_W - 1):], s_new, state(h_re), state(h_im))


def kernel(x_prompt, x_sample, cache_a_k, cache_a_v, state_b_conv, state_b_ssm, state_c_re, state_c_im, w_in, a_rel_bias, b_conv_w, b_conv_b, b_a_log, b_dt_bias, b_norm_w, c_a_re, c_a_im, c_log_dt, c_b_re, c_b_im, c_c_re, c_c_im, c_d, c_glu_w, c_glu_b, w_out, ln1_g, ln1_b, w_up, b_up, w_down, ln2_g, ln2_b):
    n_p, t_p, _ = x_prompt.shape
    n_s, t_s, _ = x_sample.shape
    n_cache = cache_a_k.shape[2]
    yp = x_prompt.reshape(n_p * t_p, D_MODEL)
    ys = x_sample.reshape(n_s * t_s, D_MODEL)
    p_out, s_out = [], []
    for l in range(DEPTH):
        lw = _layer_weights(l, w_in, a_rel_bias, b_conv_w, b_conv_b, b_a_log, b_dt_bias, b_norm_w, c_a_re, c_a_im,
                            c_log_dt, c_b_re, c_b_im, c_c_re, c_c_im, c_d, c_glu_w, c_glu_b, w_out, ln1_g, ln1_b,
                            w_up, b_up, w_down, ln2_g, ln2_b, t_s, n_cache)
        yp, st_p = _prompt_layer(yp, lw, n_p)
        ys, st_s = _sample_layer(ys, lw, n_s, cache_a_k[l], cache_a_v[l], state_b_conv[l], state_b_ssm[l],
                                 state_c_re[l], state_c_im[l])
        p_out.append(st_p)
        s_out.append(st_s)
    stack = lambda outs, i: jnp.stack([o[i] for o in outs])
    return (yp.reshape(n_p, t_p, D_MODEL), ys.reshape(n_s, t_s, D_MODEL),
            *[stack(p_out, i) for i in range(6)], *[stack(s_out, i) for i in range(6)])
```

```python
import functools

import jax
import jax.numpy as jnp
import numpy as np
from jax import lax
from jax.experimental import pallas as pl
from jax.experimental.pallas import tpu as pltpu

F32 = jnp.float32
BF16 = jnp.bfloat16

D_MODEL = 1024
DEPTH = 2
CHUNK = 64
A_HEADS = 8
A_HEAD_DIM = 64
A_WIDTH = A_HEADS * A_HEAD_DIM
A_WINDOW = 8 * CHUNK
MAX_REL = 256
B_HEADS = 4
B_HEAD_DIM = 64
B_WIDTH = B_HEADS * B_HEAD_DIM
CONV_W = 4
C_WIDTH = D_MODEL - A_WIDTH - B_WIDTH
C_GROUP = 16
C_GROUPS = C_WIDTH // C_GROUP
C_STATE = 64
C_LANES = C_GROUPS * C_STATE
D_FF = 4 * D_MODEL
DN_ALPHA = (2.0 * DEPTH) ** 0.25
LN_EPS = 1e-5
RMS_EPS = 1e-6
NEG_INF = -1e30

SUBLANES = 8
LANES = 128
VMEM_LIMIT_BYTES = 56 * 1024 * 1024

ROW_TILE = 512
ATT_Q_BLOCK = 512
ATT_Q_SUB = 2 * CHUNK
ATT_K_SUB = ATT_Q_SUB + A_WINDOW
PAIR = 2 * CHUNK
DELTA_ROWS = 2 * PAIR
S5_TIME_BLOCK = 128
S5_BATCH = SUBLANES
FF_SLAB = 1024

_SEG = {}
_off = 0
for _name, _w in (("q", A_WIDTH), ("k", A_WIDTH), ("v", A_WIDTH), ("qkvb", 3 * B_WIDTH), ("gate", B_WIDTH),
                  ("uc", C_WIDTH), ("beta", B_WIDTH), ("a", B_HEADS * LANES)):
    _SEG[_name] = (_off, _off + _w)
    _off += _w
PROJ_WIDTH = _off


def _bdot(a, b):
    return jnp.dot(a.astype(BF16), b.astype(BF16), preferred_element_type=F32)


def _bdot_nt(a, b):
    return lax.dot_general(a.astype(BF16), b.astype(BF16), (((1,), (1,)), ((), ())), preferred_element_type=F32)


def _split3(a):
    hi = a.astype(BF16)
    r = a - hi.astype(F32)
    mid = r.astype(BF16)
    lo = (r - mid.astype(F32)).astype(BF16)
    return hi, mid, lo


def _dot_f32_by_exact(a, e):
    lhs = jnp.concatenate(_split3(a), axis=1)
    return jnp.dot(lhs, jnp.concatenate([e, e, e], axis=0), preferred_element_type=F32)


def _dot_exact_by_f32(e, a):
    rhs = jnp.concatenate(_split3(a), axis=0)
    return jnp.dot(jnp.concatenate([e, e, e], axis=1), rhs, preferred_element_type=F32)


def _dotx(a, b):
    a_hi = a.astype(BF16)
    a_lo = (a - a_hi.astype(F32)).astype(BF16)
    b_hi = b.astype(BF16)
    b_lo = (b - b_hi.astype(F32)).astype(BF16)
    lhs = jnp.concatenate([a_hi, a_hi, a_lo], axis=1)
    rhs = jnp.concatenate([b_hi, b_lo, b_hi], axis=0)
    return jnp.dot(lhs, rhs, preferred_element_type=F32)


def _blk(idx, size):
    return jnp.bitwise_and(idx, -size)


def _layer_norm(x, g, b):
    mu = jnp.mean(x, -1, keepdims=True)
    xc = x - mu
    var = jnp.mean(xc * xc, -1, keepdims=True)
    return xc * lax.rsqrt(var + LN_EPS) * g + b


def _silu(x):
    return x * jax.nn.sigmoid(x)


def _softplus(x):
    return jnp.maximum(x, 0.0) + jnp.log1p(jnp.exp(-jnp.abs(x)))


def _gelu_tanh(x):
    c = float(np.sqrt(2.0 / np.pi))
    return x * (0.5 * (1.0 + jnp.tanh(c * (x + 0.044715 * (x * x * x)))))


def _const_spec(shape):
    nd = len(shape)
    return pl.BlockSpec(shape, lambda *_: (0,) * nd, pipeline_mode=pl.Buffered(1))


def _params(semantics):
    return pltpu.CompilerParams(dimension_semantics=semantics, vmem_limit_bytes=VMEM_LIMIT_BYTES)


_PROJ_ORDER = ("q", "k", "v", "qkvb", "gate", "uc", "beta", "a")


def _proj_kernel(x_ref, w_ref, *out_refs, tiles_per_stream):
    xb = x_ref[...].astype(BF16)
    vals = {}
    for name, ref in zip(_PROJ_ORDER, out_refs):
        lo, hi = _SEG[name]
        vals[name] = jnp.dot(xb, w_ref[:, lo:hi], preferred_element_type=F32)
        ref[...] = vals[name]
    if len(out_refs) > len(_PROJ_ORDER):
        kt_ref, vt_ref = out_refs[len(_PROJ_ORDER):]

        @pl.when(pl.program_id(0) % tiles_per_stream == tiles_per_stream - 1)
        def _():
            kt_ref[...] = vals["k"].T
            vt_ref[...] = vals["v"].T


def _proj(x2d, w, n_batch, prompt):
    n_rows = x2d.shape[0]
    tm = min(ROW_TILE, n_rows)
    n_steps = n_rows // tm
    per_b = n_rows // n_batch // tm if prompt else 1
    widths = {name: hi - lo for name, (lo, hi) in _SEG.items()}
    out_shape, out_specs = [], []
    for name in _PROJ_ORDER:
        wd = widths[name]
        if name == "uc" and prompt:
            out_shape.append(jax.ShapeDtypeStruct((n_rows // n_batch, n_batch * wd), F32))
            out_specs.append(pl.BlockSpec((tm, wd), lambda i: (i % per_b, i // per_b)))
        else:
            out_shape.append(jax.ShapeDtypeStruct((n_rows, wd), F32))
            out_specs.append(pl.BlockSpec((tm, wd), lambda i: (i, 0)))
    if prompt:
        assert tm == A_WINDOW
        for _ in range(2):
            out_shape.append(jax.ShapeDtypeStruct((n_batch, A_WIDTH, A_WINDOW), F32))
            out_specs.append(pl.BlockSpec((None, A_WIDTH, A_WINDOW), lambda i: (i // per_b, 0, 0)))
    return pl.pallas_call(
        functools.partial(_proj_kernel, tiles_per_stream=per_b),
        grid=(n_steps,),
        in_specs=[pl.BlockSpec((tm, D_MODEL), lambda i: (i, 0)), _const_spec(w.shape)],
        out_specs=out_specs,
        out_shape=out_shape,
        compiler_params=_params(("arbitrary",)),
        name="proj",
    )(x2d, w)


def _softmax_pv(q2s, key_segs, val_segs, bias_segs, valid_segs, feature_major=None):
    feature_major = feature_major or [False] * len(valid_segs)
    nn = (((1,), (0,)), ((), ()))
    nt = (((1,), (1,)), ((), ()))
    m_rows = q2s[0].shape[0]
    low = lax.broadcasted_iota(jnp.int32, q2s[0].shape, 1) < A_HEAD_DIM
    qs = [jnp.concatenate([jnp.where(low, q2, 0.0), jnp.where(low, 0.0, q2)], axis=0).astype(BF16) for q2 in q2s]
    scores = []
    for q, ksegs, biases in zip(qs, key_segs, bias_segs):
        sc = []
        for kseg, bias, valid, fm in zip(ksegs, biases, valid_segs, feature_major):
            s = lax.dot_general(q, kseg, nn if fm else nt, preferred_element_type=F32) + bias
            if valid is not None:
                s = jnp.where(valid, s, NEG_INF)
            sc.append(s)
        scores.append(sc)
    tops = [functools.reduce(jnp.maximum, [jnp.max(s, -1, keepdims=True) for s in sc]) for sc in scores]
    probs = [[jnp.exp(s - m) for s in sc] for sc, m in zip(scores, tops)]
    denoms = [functools.reduce(jnp.add, [jnp.sum(p, -1, keepdims=True) for p in pr]) for pr in probs]
    pvs = [functools.reduce(jnp.add, [lax.dot_general(p.astype(BF16), vseg, nt if fm else nn,
                                                      preferred_element_type=F32)
                                      for p, vseg, fm in zip(pr, vsegs, feature_major)])
           for pr, vsegs in zip(probs, val_segs)]
    outs = []
    for pv, denom in zip(pvs, denoms):
        pvn = pv * (1.0 / denom)
        outs.append(jnp.where(low, pvn[:m_rows], pvn[m_rows:]))
    return outs


def _attn_prompt_kernel(q_ref, kp_ref, kc_ref, vp_ref, vc_ref, bias_ref, o_ref, kk_ref, vv_ref):
    jb = pl.program_id(1)
    kk_ref[0:ATT_Q_BLOCK, :] = kp_ref[...].astype(BF16)
    kk_ref[ATT_Q_BLOCK:, :] = kc_ref[...].astype(BF16)
    vv_ref[0:ATT_Q_BLOCK, :] = vp_ref[...].astype(BF16)
    vv_ref[ATT_Q_BLOCK:, :] = vc_ref[...].astype(BF16)
    scale = A_HEAD_DIM ** -0.5
    for sub in range(ATT_Q_BLOCK // ATT_Q_SUB):
        r0 = sub * ATT_Q_SUB
        pos = lax.broadcasted_iota(jnp.int32, (1, ATT_K_SUB), 1) + ((jb - 1) * ATT_Q_BLOCK + r0)
        valid = pos >= 0
        pairs = range(A_HEADS // 2)
        cols = [slice(hp * LANES, (hp + 1) * LANES) for hp in pairs]
        outs = _softmax_pv([q_ref[r0:r0 + ATT_Q_SUB, c] * scale for c in cols],
                           [[kk_ref[r0:r0 + ATT_K_SUB, c]] for c in cols],
                           [[vv_ref[r0:r0 + ATT_K_SUB, c]] for c in cols],
                           [[bias_ref[hp]] for hp in pairs], [valid])
        for c, out in zip(cols, outs):
            o_ref[r0:r0 + ATT_Q_SUB, c] = out


def _attn_prompt(q, k, v, bias, n_batch):
    n_rows = q.shape[0]
    per_b = n_rows // n_batch // ATT_Q_BLOCK
    blk = (ATT_Q_BLOCK, A_WIDTH)
    cur = pl.BlockSpec(blk, lambda b, j: (b * per_b + j, 0))
    prev = pl.BlockSpec(blk, lambda b, j: (b * per_b + jnp.maximum(j - 1, 0), 0))
    return pl.pallas_call(
        _attn_prompt_kernel,
        grid=(n_batch, per_b),
        in_specs=[cur, prev, cur, prev, cur, _const_spec(bias.shape)],
        out_specs=cur,
        out_shape=jax.ShapeDtypeStruct((n_rows, A_WIDTH), F32),
        scratch_shapes=[pltpu.VMEM((2 * ATT_Q_BLOCK, A_WIDTH), BF16), pltpu.VMEM((2 * ATT_Q_BLOCK, A_WIDTH), BF16)],
        compiler_params=_params(("parallel", "arbitrary")),
        name="attn_prompt",
    )(q, k, k, v, v, bias)


def _attn_sample_kernel(q_ref, kn_ref, vn_ref, kc_ref, vc_ref, bias_c_ref, bias_n_ref, o_ref):
    scale = A_HEAD_DIM ** -0.5
    pairs = range(A_HEADS // 2)
    cols = [slice(hp * LANES, (hp + 1) * LANES) for hp in pairs]
    outs = _softmax_pv([q_ref[:, c] * scale for c in cols],
                       [[kc_ref[c, :].astype(BF16), kn_ref[:, c].astype(BF16)] for c in cols],
                       [[vc_ref[c, :].astype(BF16), vn_ref[:, c].astype(BF16)] for c in cols],
                       [[bias_c_ref[hp], bias_n_ref[hp]] for hp in pairs], [None, None], [True, False])
    for c, out in zip(cols, outs):
        o_ref[:, c] = out


def _attn_sample(q, k, v, k_cache, v_cache, bias_c, bias_n, n_batch):
    n_rows = q.shape[0]
    s_len = n_rows // n_batch
    n_cache = k_cache.shape[2]
    new = pl.BlockSpec((s_len, A_WIDTH), lambda b: (b, 0))
    cache = pl.BlockSpec((None, A_WIDTH, n_cache), lambda b: (b, 0, 0))
    return pl.pallas_call(
        _attn_sample_kernel,
        grid=(n_batch,),
        in_specs=[new, new, new, cache, cache, _const_spec(bias_c.shape), _const_spec(bias_n.shape)],
        out_specs=new,
        out_shape=jax.ShapeDtypeStruct((n_rows, A_WIDTH), F32),
        compiler_params=_params(("parallel",)),
        name="attn_sample",
    )(q, k, v, k_cache, v_cache, bias_c, bias_n)


def _rel_bias_blocks(table, n_q, n_k, q_offset, banded):
    period = n_q + n_k - 1
    rel = q_offset - ((np.arange(period) + n_q - 1) % period - (n_q - 1))
    vec = jnp.take(table.astype(F32), jnp.asarray(np.clip(rel, -MAX_REL, MAX_REL) + MAX_REL), axis=1)
    n_heads = table.shape[0]
    bias = jnp.tile(vec, (1, n_q))[:, :n_q * (period - 1)].reshape(n_heads, n_q, period - 1)[:, :, :n_k]
    if banded:
        i = np.arange(n_q)[:, None]
        jj = np.arange(n_k)[None, :] - (i // CHUNK) * CHUNK
        inband = (jj >= 0) & (jj < A_WINDOW + CHUNK)
        bias = jnp.where(jnp.asarray(inband), bias, NEG_INF)
    return bias.reshape(n_heads // 2, 2 * n_q, n_k)


def _inv_unit_lower(mats, row, col):
    base = 8
    diag = _blk(row, base) == _blk(col, base)
    eye = jnp.where(row == col, 1.0, 0.0)
    a0 = [jnp.where(diag, a, 0.0) for a in mats]
    xs = [eye - a for a in a0]
    ps = [_dotx(a, a) for a in a0]
    xs = [x + _dotx(x, p) for x, p in zip(xs, ps)]
    ps = [_dotx(p, p) for p in ps]
    xs = [x + _dotx(x, p) for x, p in zip(xs, ps)]
    bs = base
    while bs < CHUNK:
        off = (_blk(row, 2 * bs) == _blk(col, 2 * bs)) & (_blk(row, bs) != _blk(col, bs))
        ts = [_dotx(jnp.where(off, a, 0.0), x) for a, x in zip(mats, xs)]
        xs = [x - _dotx(x, t) for x, t in zip(xs, ts)]
        bs *= 2
    return xs


def _delta_kernel(qkvb_ref, gate_ref, beta_ref, a_ref, cst_ref, s0_ref, cw_ref, cb_ref, alog_ref, dtb_ref, nw_ref,
                  o_ref, cout_ref, sout_ref, xbuf, s_ref, ob_ref, *, t_valid, n_rows):
    j = pl.program_id(1)
    n_blk = n_rows // PAIR
    hd = B_HEAD_DIM
    low = lax.broadcasted_iota(jnp.int32, (PAIR, LANES), 1) < hd
    low_rows = lax.broadcasted_iota(jnp.int32, (n_rows, LANES), 1) < hd
    low_s = lax.broadcasted_iota(jnp.int32, (hd, LANES), 1) < hd
    zeros_s = jnp.zeros((hd, LANES), F32)

    @pl.when(j == 0)
    def _():
        xbuf[0:SUBLANES, :] = cst_ref[0]
        for h in range(B_HEADS):
            s0 = s0_ref[0, h]
            s_ref[h] = jnp.concatenate([s0, jnp.zeros_like(s0)] if h % 2 == 0 else [jnp.zeros_like(s0), s0], axis=1)

    xbuf[SUBLANES:SUBLANES + n_rows, :] = qkvb_ref[...]
    conv = cb_ref[...]
    for w in range(CONV_W):
        start = SUBLANES - (CONV_W - 1) + w
        conv = conv + xbuf[start:start + n_rows, :] * cw_ref[w:w + 1, :]
    conv = _silu(conv)
    t_last = n_rows - PAIR + t_valid
    tail = xbuf[t_last:t_last + SUBLANES, :]
    cout_ref[0] = tail
    xbuf[0:SUBLANES, :] = tail

    hrow = lax.broadcasted_iota(jnp.int32, (B_WIDTH, B_WIDTH), 0)
    hcol = lax.broadcasted_iota(jnp.int32, (B_WIDTH, B_WIDTH), 1)
    head_ones = jnp.where(_blk(hrow, B_HEAD_DIM) == _blk(hcol, B_HEAD_DIM), 1.0, 0.0).astype(BF16)

    def l2n(t):
        return t * lax.rsqrt(_dot_f32_by_exact(t * t, head_ones) + RMS_EPS)

    qn = l2n(conv[:, 0:B_WIDTH]) * (B_HEAD_DIM ** -0.5)
    kn = l2n(conv[:, B_WIDTH:2 * B_WIDTH])
    vv = conv[:, 2 * B_WIDTH:3 * B_WIDTH]
    beta = jax.nn.sigmoid(beta_ref[...])
    g = -jnp.exp(alog_ref[...]) * _softplus(a_ref[...] + dtb_ref[...])
    if t_valid < PAIR:
        def live(t):
            return jnp.where(lax.broadcasted_iota(jnp.int32, t.shape, 0) < t_valid, t, 0.0)
        kn, vv, beta, g = live(kn), live(vv), live(beta), live(g)

    rr = lax.broadcasted_iota(jnp.int32, (n_rows, n_rows), 0)
    cc = lax.broadcasted_iota(jnp.int32, (n_rows, n_rows), 1)
    ltri = jnp.where((_blk(rr, CHUNK) == _blk(cc, CHUNK)) & (rr >= cc), 1.0, 0.0).astype(BF16)
    gc = _dot_exact_by_f32(ltri, g)
    n_chunks = n_rows // CHUNK
    g_last = [gc[(c + 1) * CHUNK - 1:(c + 1) * CHUNK, :] for c in range(n_chunks)]
    gl = jnp.concatenate([jnp.broadcast_to(t, (CHUNK, t.shape[1])) for t in g_last], axis=0)
    egc = jnp.exp(gc)
    erest = jnp.exp(gl - gc)
    egl = [jnp.exp(t) for t in g_last]

    def pair_lanes(t, p):
        return jnp.where(low_rows, t[:, (2 * p) * LANES:(2 * p + 1) * LANES], t[:, (2 * p + 1) * LANES:(2 * p + 2) * LANES])

    n_pairs = B_HEADS // 2
    k_p, q_p, kb_p, vb_p, kbg_r, qd_r, kd_p = [], [], [], [], [], [], []
    for p in range(n_pairs):
        ls = slice(p * LANES, (p + 1) * LANES)
        k, q, v, b = kn[:, ls], qn[:, ls], vv[:, ls], beta[:, ls]
        eg = pair_lanes(egc, p)
        k_p.append(k)
        q_p.append(q)
        kb_p.append(k * b)
        vb_p.append(v * b)
        kbg_r.append(pltpu.roll(k * b * eg, hd, axis=1))
        qd_r.append(pltpu.roll(q * eg, hd, axis=1))
        kd_p.append(k * pair_lanes(erest, p))

    row = lax.broadcasted_iota(jnp.int32, (PAIR, PAIR), 0)
    col = lax.broadcasted_iota(jnp.int32, (PAIR, PAIR), 1)
    same = _blk(row, CHUNK) == _blk(col, CHUNK)
    tril = same & (row >= col)
    strict = same & (row > col)
    first = row < CHUNK
    units = [(n, h) for n in range(n_blk) for h in range(B_HEADS)]

    def own(h, t, other):
        return jnp.where(low, t, other) if h % 2 == 0 else jnp.where(low, other, t)

    decays, a_lows = [], []
    for n, h in units:
        rs = slice(n * PAIR, (n + 1) * PAIR)
        gcol = gc[rs, h * LANES:(h + 1) * LANES]
        diff = gcol - gcol.T
        decays.append(jnp.where(tril, jnp.exp(jnp.where(tril, diff, 0.0)), 0.0))
    for (n, h), decay in zip(units, decays):
        rs = slice(n * PAIR, (n + 1) * PAIR)
        kk = _bdot_nt(own(h, kb_p[h // 2][rs], 0.0), k_p[h // 2][rs])
        a_lows.append(jnp.where(strict, kk * decay, 0.0))
    x_inv = _inv_unit_lower(a_lows, row, col)
    uws = [_dotx(x, own(h, vb_p[h // 2][n * PAIR:(n + 1) * PAIR], kbg_r[h // 2][n * PAIR:(n + 1) * PAIR]))
           for (n, h), x in zip(units, x_inv)]
    qus = []
    for (n, h), decay, uw in zip(units, decays, uws):
        rs = slice(n * PAIR, (n + 1) * PAIR)
        qk = jnp.where(tril, _bdot_nt(own(h, q_p[h // 2][rs], 0.0), k_p[h // 2][rs]) * decay, 0.0)
        qus.append(_bdot(qk, uw))
    kus = {}
    for n in range(n_blk):
        for p in range(n_pairs):
            uw_e = [uws[n * B_HEADS + 2 * p + e] for e in range(2)]
            rhs = jnp.concatenate([jnp.where(first if c == 0 else ~first, uw_e[e], 0.0)
                                   for e in range(2) for c in range(2)], axis=1)
            kus[(n, p)] = _bdot(kd_p[p][n * PAIR:(n + 1) * PAIR].T, rhs)

    states = [s_ref[h] for h in range(B_HEADS)]
    for n in range(n_blk):
        for c in range(PAIR // CHUNK):
            r0 = n * PAIR + c * CHUNK
            outs = []
            for h in range(B_HEADS):
                p, e = divmod(h, 2)
                ku = kus[(n, p)][e * hd:(e + 1) * hd, (2 * e + c) * LANES:(2 * e + c + 1) * LANES]
                qu = qus[n * B_HEADS + h][c * CHUNK:(c + 1) * CHUNK, :]
                kq = jnp.concatenate([ku, qd_r[p][r0:r0 + CHUNK] - qu], axis=0)
                s_ext = jnp.concatenate([zeros_s, states[h]] if e == 0 else [states[h], zeros_s], axis=0)
                r = _bdot(kq, s_ext)
                gt = egl[n * (PAIR // CHUNK) + c][:, h * LANES:(h + 1) * LANES]
                new = gt * states[h] - r[:hd] + ku
                states[h] = jnp.where(low_s, new, 0.0) if e == 0 else jnp.where(low_s, 0.0, new)
                outs.append(r[hd:] + qu)
            for p in range(n_pairs):
                ob_ref[r0:r0 + CHUNK, p * LANES:(p + 1) * LANES] = jnp.where(low_s, outs[2 * p], outs[2 * p + 1])
    for h in range(B_HEADS):
        s_ref[h] = states[h]

    ob = ob_ref[...]
    ms = _dot_f32_by_exact(ob * ob, head_ones) * (1.0 / hd)
    o_ref[...] = ob * lax.rsqrt(ms + RMS_EPS) * nw_ref[...] * _silu(gate_ref[...])

    @pl.when(j == pl.num_programs(1) - 1)
    def _():
        for h in range(B_HEADS):
            sout_ref[0, h] = s_ref[h][:, (h % 2) * hd:(h % 2 + 1) * hd]


def _delta(qkvb, gate, beta, a, conv_state, s0, lw, n_batch, t_valid, rows_per_step):
    n_rows = qkvb.shape[0]
    per_b = n_rows // n_batch // rows_per_step

    def rows(width):
        return pl.BlockSpec((rows_per_step, width), lambda b, j: (b * per_b + j, 0))

    cst = pl.BlockSpec((1, SUBLANES, 3 * B_WIDTH), lambda b, j: (b, 0, 0))
    st = pl.BlockSpec((1, B_HEADS, B_HEAD_DIM, B_HEAD_DIM), lambda b, j: (b, 0, 0, 0))
    consts = [lw["conv_w"], lw["conv_b"], lw["a_log"], lw["dt_bias"], lw["norm_w"]]
    return pl.pallas_call(
        functools.partial(_delta_kernel, t_valid=t_valid, n_rows=rows_per_step),
        grid=(n_batch, per_b),
        in_specs=[rows(3 * B_WIDTH), rows(B_WIDTH), rows(B_WIDTH), rows(B_HEADS * LANES), cst, st]
        + [_const_spec(c.shape) for c in consts],
        out_specs=[rows(B_WIDTH), cst, st],
        out_shape=[jax.ShapeDtypeStruct((n_rows, B_WIDTH), F32),
                   jax.ShapeDtypeStruct((n_batch, SUBLANES, 3 * B_WIDTH), F32),
                   jax.ShapeDtypeStruct((n_batch, B_HEADS, B_HEAD_DIM, B_HEAD_DIM), F32)],
        scratch_shapes=[pltpu.VMEM((SUBLANES + rows_per_step, 3 * B_WIDTH), F32),
                        pltpu.VMEM((B_HEADS, B_HEAD_DIM, LANES), F32),
                        pltpu.VMEM((rows_per_step, B_WIDTH), F32)],
        compiler_params=_params(("parallel", "arbitrary")),
        name="delta",
    )(qkvb, gate, beta, a, conv_state, s0, *consts)


def _s5_kernel(u_ref, h0r_ref, h0i_ref, lr_ref, li_ref, bre_ref, bim_ref, cre_ref, cim_ref, d_ref, gw_ref, gb_ref,
               o_ref, hro_ref, hio_ref, xr, xi, hs, *, t_block):
    tb = pl.program_id(1)

    @pl.when(tb == 0)
    def _():
        hs[0] = h0r_ref[0]
        hs[1] = h0i_ref[0]

    u = u_ref[0]
    ub = u.astype(BF16)
    xr[...] = jnp.dot(ub, bre_ref[...], preferred_element_type=F32)
    xi[...] = jnp.dot(ub, bim_ref[...], preferred_element_type=F32)
    lam_r = jnp.broadcast_to(lr_ref[...], (S5_BATCH, C_LANES))
    lam_i = jnp.broadcast_to(li_ref[...], (S5_BATCH, C_LANES))

    def step(t, carry):
        h_r, h_i = carry
        rows = pl.ds(pl.multiple_of(t * S5_BATCH, S5_BATCH), S5_BATCH)
        n_r = lam_r * h_r - lam_i * h_i + xr[rows, :]
        n_i = lam_r * h_i + lam_i * h_r + xi[rows, :]
        xr[rows, :] = n_r
        xi[rows, :] = n_i
        return n_r, n_i

    h_r, h_i = lax.fori_loop(0, t_block, step, (hs[0], hs[1]))
    hs[0] = h_r
    hs[1] = h_i
    hro_ref[0] = h_r
    hio_ref[0] = h_i
    y = (jnp.dot(xr[...].astype(BF16), cre_ref[...], preferred_element_type=F32)
         - jnp.dot(xi[...].astype(BF16), cim_ref[...], preferred_element_type=F32) + d_ref[...] * u)
    z = _gelu_tanh(y)
    o_ref[0] = z * jax.nn.sigmoid(jnp.dot(z.astype(BF16), gw_ref[...], preferred_element_type=F32) + gb_ref[...])


def _s5(u, h0_re, h0_im, lw, t_block):
    n_groups, n_rows, _ = u.shape
    rows_blk = t_block * S5_BATCH
    n_tb = n_rows // rows_blk
    u_spec = pl.BlockSpec((1, rows_blk, C_WIDTH), lambda g, t: (g, t, 0))
    h_spec = pl.BlockSpec((1, S5_BATCH, C_LANES), lambda g, t: (g, 0, 0))
    consts = [lw["lam_re"], lw["lam_im"], lw["b_re"], lw["b_im"], lw["c_re"], lw["c_im"], lw["c_d"],
              lw["glu_w"], lw["glu_b"]]
    return pl.pallas_call(
        functools.partial(_s5_kernel, t_block=t_block),
        grid=(n_groups, n_tb),
        in_specs=[u_spec, h_spec, h_spec] + [_const_spec(c.shape) for c in consts],
        out_specs=[u_spec, h_spec, h_spec],
        out_shape=[jax.ShapeDtypeStruct(u.shape, F32),
                   jax.ShapeDtypeStruct(h0_re.shape, F32), jax.ShapeDtypeStruct(h0_im.shape, F32)],
        scratch_shapes=[pltpu.VMEM((rows_blk, C_LANES), F32), pltpu.VMEM((rows_blk, C_LANES), F32),
                        pltpu.VMEM((2, S5_BATCH, C_LANES), F32)],
        compiler_params=_params(("parallel", "arbitrary")),
        name="s5",
    )(u, h0_re, h0_im, *consts)


def _post_kernel(x_ref, a_ref, b_ref, c_ref, wo_ref, g1_ref, b1_ref, wu_ref, bu_ref, wd_ref, g2_ref, b2_ref, o_ref):
    mix = (jnp.dot(a_ref[...].astype(BF16), wo_ref[0:A_WIDTH, :], preferred_element_type=F32)
           + jnp.dot(b_ref[...].astype(BF16), wo_ref[A_WIDTH:A_WIDTH + B_WIDTH, :], preferred_element_type=F32)
           + jnp.dot(c_ref[...].astype(BF16), wo_ref[A_WIDTH + B_WIDTH:, :], preferred_element_type=F32))
    x1 = _layer_norm(DN_ALPHA * x_ref[...] + mix, g1_ref[...], b1_ref[...])
    x1b = x1.astype(BF16)
    acc = jnp.zeros(x1.shape, F32)
    for s in range(D_FF // FF_SLAB):
        cs = slice(s * FF_SLAB, (s + 1) * FF_SLAB)
        hid = jnp.dot(x1b, wu_ref[:, cs], preferred_element_type=F32) + bu_ref[:, cs]
        hid = jnp.square(jnp.maximum(hid, 0.0))
        acc = acc + jnp.dot(hid.astype(BF16), wd_ref[cs, :], preferred_element_type=F32)
    o_ref[...] = _layer_norm(DN_ALPHA * x1 + acc, g2_ref[...], b2_ref[...])


def _post(x2d, out_a, out_b, out_c, lw, n_batch, c_time_major):
    n_rows = x2d.shape[0]
    tm = min(ROW_TILE, n_rows)

    def rows(width):
        return pl.BlockSpec((tm, width), lambda i: (i, 0))

    if c_time_major:
        per_b = n_rows // n_batch // tm
        c_spec = pl.BlockSpec((tm, C_WIDTH), lambda i: (i % per_b, i // per_b))
    else:
        c_spec = rows(C_WIDTH)
    consts = [lw["w_out"], lw["ln1_g"], lw["ln1_b"], lw["w_up"], lw["b_up"], lw["w_down"], lw["ln2_g"], lw["ln2_b"]]
    return pl.pallas_call(
        _post_kernel,
        grid=(n_rows // tm,),
        in_specs=[rows(D_MODEL), rows(A_WIDTH), rows(B_WIDTH), c_spec] + [_const_spec(c.shape) for c in consts],
        out_specs=rows(D_MODEL),
        out_shape=jax.ShapeDtypeStruct((n_rows, D_MODEL), F32),
        compiler_params=_params(("parallel",)),
        name="post",
    )(x2d, out_a, out_b, out_c, *consts)


def _layer_weights(l, w_in, a_rel_bias, b_conv_w, b_conv_b, b_a_log, b_dt_bias, b_norm_w, c_a_re, c_a_im, c_log_dt,
                   c_b_re, c_b_im, c_c_re, c_c_im, c_d, c_glu_w, c_glu_b, w_out, ln1_g, ln1_b, w_up, b_up, w_down,
                   ln2_g, ln2_b, s_len, n_cache):
    wi = w_in[l]
    o_q, o_k, o_v = 0, A_WIDTH, 2 * A_WIDTH
    o_qkvb = 3 * A_WIDTH
    o_beta = o_qkvb + 3 * B_WIDTH
    o_a = o_beta + B_HEADS
    o_gate = o_a + B_HEADS
    o_uc = o_gate + B_WIDTH
    per_head = lambda cols: jnp.repeat(cols, B_HEAD_DIM, axis=-1)
    per_group = lambda cols: jnp.repeat(cols, LANES, axis=-1)
    w_proj = jnp.concatenate([
        wi[:, o_q:o_q + A_WIDTH], wi[:, o_k:o_k + A_WIDTH], wi[:, o_v:o_v + A_WIDTH],
        wi[:, o_qkvb:o_qkvb + 3 * B_WIDTH], wi[:, o_gate:o_gate + B_WIDTH], wi[:, o_uc:o_uc + C_WIDTH],
        per_head(wi[:, o_beta:o_beta + B_HEADS]), per_group(wi[:, o_a:o_a + B_HEADS])], axis=1).astype(BF16)

    table = a_rel_bias[l]
    bias_sample = _rel_bias_blocks(table, s_len, n_cache + s_len, n_cache, banded=False)

    a_re, a_im = c_a_re[l].astype(F32), c_a_im[l].astype(F32)
    dt = jnp.exp(c_log_dt[l].astype(F32))[:, None]
    mag = jnp.exp(dt * a_re)
    lam_re, lam_im = mag * jnp.cos(dt * a_im), mag * jnp.sin(dt * a_im)
    den = a_re * a_re + a_im * a_im
    coef_re = ((lam_re - 1.0) * a_re + lam_im * a_im) / den
    coef_im = (lam_im * a_re - (lam_re - 1.0) * a_im) / den
    bre, bim = c_b_re[l].astype(F32), c_b_im[l].astype(F32)
    bb_re = coef_re[..., None] * bre - coef_im[..., None] * bim
    bb_im = coef_re[..., None] * bim + coef_im[..., None] * bre
    eye = jnp.eye(C_GROUPS, dtype=F32)
    in_bd = lambda t: jnp.einsum("gph,gk->ghkp", t, eye).reshape(C_WIDTH, C_LANES).astype(BF16)
    out_bd = lambda t: jnp.einsum("ghp,gk->gpkh", t.astype(F32), eye).reshape(C_LANES, C_WIDTH).astype(BF16)
    row = lambda t: t.astype(F32).reshape(1, -1)

    return {
        "w_proj": w_proj,
        "bias_prompt": _rel_bias_blocks(table, ATT_Q_SUB, ATT_K_SUB, A_WINDOW, banded=True),
        "bias_cache": bias_sample[:, :, :n_cache], "bias_new": bias_sample[:, :, n_cache:],
        "conv_w": b_conv_w[l].astype(F32), "conv_b": row(b_conv_b[l]),
        "a_log": row(per_group(b_a_log[l])), "dt_bias": row(per_group(b_dt_bias[l])),
        "norm_w": row(jnp.tile(b_norm_w[l], B_HEADS)),
        "lam_re": row(lam_re), "lam_im": row(lam_im), "b_re": in_bd(bb_re), "b_im": in_bd(bb_im),
        "c_re": out_bd(c_c_re[l]), "c_im": out_bd(c_c_im[l]), "c_d": row(c_d[l]),
        "glu_w": c_glu_w[l].astype(BF16), "glu_b": row(c_glu_b[l]),
        "w_out": w_out[l].astype(BF16), "ln1_g": row(ln1_g[l]), "ln1_b": row(ln1_b[l]),
        "w_up": w_up[l].astype(BF16), "b_up": row(b_up[l]), "w_down": w_down[l].astype(BF16),
        "ln2_g": row(ln2_g[l]), "ln2_b": row(ln2_b[l]),
    }


def _pad_conv_state(conv_buf):
    return jnp.pad(conv_buf.astype(F32), ((0, 0), (SUBLANES - (CONV_W - 1), 0), (0, 0)))


def _prompt_layer(x2d, lw, n_batch):
    t_len = x2d.shape[0] // n_batch
    q, k, v, qkvb, gate, uc, beta, a, k_t, v_t = _proj(x2d, lw["w_proj"], n_batch, prompt=True)
    out_a = _attn_prompt(q, k, v, lw["bias_prompt"], n_batch)
    out_b, conv_o, s_new = _delta(
        qkvb, gate, beta, a, jnp.zeros((n_batch, SUBLANES, 3 * B_WIDTH), F32),
        jnp.zeros((n_batch, B_HEADS, B_HEAD_DIM, B_HEAD_DIM), F32), lw, n_batch, t_valid=PAIR, rows_per_step=DELTA_ROWS)
    zeros_h = jnp.zeros((1, S5_BATCH, C_LANES), F32)
    out_c, h_re, h_im = _s5(uc.reshape(1, t_len * n_batch, C_WIDTH), zeros_h, zeros_h, lw, S5_TIME_BLOCK)
    y = _post(x2d, out_a, out_b, out_c.reshape(t_len, n_batch * C_WIDTH), lw, n_batch, c_time_major=True)
    heads = lambda t: t.reshape(n_batch, A_HEADS, A_HEAD_DIM, A_WINDOW).transpose(0, 3, 1, 2)
    state = lambda t: t.reshape(n_batch, C_GROUPS, C_STATE)
    return y, (heads(k_t), heads(v_t), conv_o[:, -(CONV_W - 1):], s_new, state(h_re), state(h_im))


def _sample_layer(x2d, lw, n_batch, k_cache, v_cache, conv_buf, s0, h0_re, h0_im):
    s_len = x2d.shape[0] // n_batch
    n_groups = n_batch // S5_BATCH
    n_cache = k_cache.shape[1]
    q, k, v, qkvb, gate, uc, beta, a = _proj(x2d, lw["w_proj"], n_batch, prompt=False)
    feat_major = lambda t: t.astype(F32).transpose(0, 2, 3, 1).reshape(n_batch, A_WIDTH, n_cache)
    out_a = _attn_sample(q, k, v, feat_major(k_cache), feat_major(v_cache), lw["bias_cache"], lw["bias_new"], n_batch)

    def pad_rows(t):
        t = t.reshape(n_batch, s_len, -1)
        return jnp.pad(t, ((0, 0), (0, PAIR - s_len), (0, 0))).reshape(n_batch * PAIR, -1)

    out_b, conv_o, s_new = _delta(pad_rows(qkvb), pad_rows(gate), pad_rows(beta), pad_rows(a),
                                  _pad_conv_state(conv_buf), s0.astype(F32), lw, n_batch, t_valid=s_len, rows_per_step=PAIR)
    out_b = out_b.reshape(n_batch, PAIR, B_WIDTH)[:, :s_len].reshape(n_batch * s_len, B_WIDTH)

    u = uc.reshape(n_groups, S5_BATCH, s_len, C_WIDTH).transpose(0, 2, 1, 3).reshape(n_groups, s_len * S5_BATCH, C_WIDTH)
    grp = lambda t: t.astype(F32).reshape(n_groups, S5_BATCH, C_LANES)
    out_c, h_re, h_im = _s5(u, grp(h0_re), grp(h0_im), lw, s_len)
    out_c = out_c.reshape(n_groups, s_len, S5_BATCH, C_WIDTH).transpose(0, 2, 1, 3).reshape(n_batch * s_len, C_WIDTH)
    y = _post(x2d, out_a, out_b, out_c, lw, n_batch, c_time_major=False)
    heads = lambda t: t.reshape(n_batch, s_len, A_HEADS, A_HEAD_DIM)
    state = lambda t: t.reshape(n_batch, C_GROUPS, C_STATE)
    return y, (heads(k), heads(v), conv_o[:, -(CONV_W - 1):], s_new, state(h_re), state(h_im))


def kernel(x_prompt, x_sample, cache_a_k, cache_a_v, state_b_conv, state_b_ssm, state_c_re, state_c_im, w_in, a_rel_bias, b_conv_w, b_conv_b, b_a_log, b_dt_bias, b_norm_w, c_a_re, c_a_im, c_log_dt, c_b_re, c_b_im, c_c_re, c_c_im, c_d, c_glu_w, c_glu_b, w_out, ln1_g, ln1_b, w_up, b_up, w_down, ln2_g, ln2_b):
    n_p, t_p, _ = x_prompt.shape
    n_s, t_s, _ = x_sample.shape
    n_cache = cache_a_k.shape[2]
    yp = x_prompt.reshape(n_p * t_p, D_MODEL)
    ys = x_sample.reshape(n_s * t_s, D_MODEL)
    p_out, s_out = [], []
    for l in range(DEPTH):
        lw = _layer_weights(l, w_in, a_rel_bias, b_conv_w, b_conv_b, b_a_log, b_dt_bias, b_norm_w, c_a_re, c_a_im,
                            c_log_dt, c_b_re, c_b_im, c_c_re, c_c_im, c_d, c_glu_w, c_glu_b, w_out, ln1_g, ln1_b,
                            w_up, b_up, w_down, ln2_g, ln2_b, t_s, n_cache)
        yp, st_p = _prompt_layer(yp, lw, n_p)
        ys, st_s = _sample_layer(ys, lw, n_s, cache_a_k[l], cache_a_v[l], state_b_conv[l], state_b_ssm[l],
                                 state_c_re[l], state_c_im[l])
        p_out.append(st_p)
        s_out.append(st_s)
    stack = lambda outs, i: jnp.stack([o[i] for o in outs])
    return (yp.reshape(n_p, t_p, D_MODEL), ys.reshape(n_s, t_s, D_MODEL),
            *[stack(p_out, i) for i in range(6)], *[stack(s_out, i) for i in range(6)])
```

```python
import functools

import jax
import jax.numpy as jnp
import numpy as np
from jax import lax
from jax.experimental import pallas as pl
from jax.experimental.pallas import tpu as pltpu

F32 = jnp.float32
BF16 = jnp.bfloat16

D_MODEL = 1024
DEPTH = 2
CHUNK = 64
A_HEADS = 8
A_HEAD_DIM = 64
A_WIDTH = A_HEADS * A_HEAD_DIM
A_WINDOW = 8 * CHUNK
MAX_REL = 256
B_HEADS = 4
B_HEAD_DIM = 64
B_WIDTH = B_HEADS * B_HEAD_DIM
CONV_W = 4
C_WIDTH = D_MODEL - A_WIDTH - B_WIDTH
C_GROUP = 16
C_GROUPS = C_WIDTH // C_GROUP
C_STATE = 64
C_LANES = C_GROUPS * C_STATE
D_FF = 4 * D_MODEL
DN_ALPHA = (2.0 * DEPTH) ** 0.25
LN_EPS = 1e-5
RMS_EPS = 1e-6
NEG_INF = -1e30

SUBLANES = 8
LANES = 128
VMEM_LIMIT_BYTES = 56 * 1024 * 1024

ROW_TILE = 512
ATT_Q_BLOCK = 512
ATT_Q_SUB = 2 * CHUNK
ATT_K_SUB = ATT_Q_SUB + A_WINDOW
PAIR = 2 * CHUNK
DELTA_ROWS = 8 * PAIR
S5_TIME_BLOCK = 128
S5_BATCH = SUBLANES
FF_SLAB = 1024

_SEG = {}
_off = 0
for _name, _w in (("q", A_WIDTH), ("k", A_WIDTH), ("v", A_WIDTH), ("qkvb", 3 * B_WIDTH), ("gate", B_WIDTH),
                  ("uc", C_WIDTH), ("beta", B_WIDTH), ("a", B_HEADS * LANES)):
    _SEG[_name] = (_off, _off + _w)
    _off += _w
PROJ_WIDTH = _off


def _bdot(a, b):
    return jnp.dot(a.astype(BF16), b.astype(BF16), preferred_element_type=F32)


def _bdot_nt(a, b):
    return lax.dot_general(a.astype(BF16), b.astype(BF16), (((1,), (1,)), ((), ())), preferred_element_type=F32)


def _split3(a):
    hi = a.astype(BF16)
    r = a - hi.astype(F32)
    mid = r.astype(BF16)
    lo = (r - mid.astype(F32)).astype(BF16)
    return hi, mid, lo


def _dot_f32_by_exact(a, e):
    lhs = jnp.concatenate(_split3(a), axis=1)
    return jnp.dot(lhs, jnp.concatenate([e, e, e], axis=0), preferred_element_type=F32)


def _dot_exact_by_f32(e, a):
    rhs = jnp.concatenate(_split3(a), axis=0)
    return jnp.dot(jnp.concatenate([e, e, e], axis=1), rhs, preferred_element_type=F32)


def _blk(idx, size):
    return jnp.bitwise_and(idx, -size)


def _layer_norm(x, g, b):
    mu = jnp.mean(x, -1, keepdims=True)
    xc = x - mu
    var = jnp.mean(xc * xc, -1, keepdims=True)
    return xc * lax.rsqrt(var + LN_EPS) * g + b


def _silu(x):
    return x * jax.nn.sigmoid(x)


def _softplus(x):
    return jnp.maximum(x, 0.0) + jnp.log1p(jnp.exp(-jnp.abs(x)))


def _gelu_tanh(x):
    c = float(np.sqrt(2.0 / np.pi))
    return x * (0.5 * (1.0 + jnp.tanh(c * (x + 0.044715 * (x * x * x)))))


def _const_spec(shape):
    nd = len(shape)
    return pl.BlockSpec(shape, lambda *_: (0,) * nd, pipeline_mode=pl.Buffered(1))


def _params(semantics):
    return pltpu.CompilerParams(dimension_semantics=semantics, vmem_limit_bytes=VMEM_LIMIT_BYTES)


_PROJ_ORDER = ("q", "k", "v", "qkvb", "gate", "uc", "beta", "a")


def _proj_kernel(x_ref, w_ref, *out_refs, tiles_per_stream):
    xb = x_ref[...].astype(BF16)
    vals = {}
    for name, ref in zip(_PROJ_ORDER, out_refs):
        lo, hi = _SEG[name]
        vals[name] = jnp.dot(xb, w_ref[:, lo:hi], preferred_element_type=F32)
        ref[...] = vals[name]
    if len(out_refs) > len(_PROJ_ORDER):
        kt_ref, vt_ref = out_refs[len(_PROJ_ORDER):]

        @pl.when(pl.program_id(0) % tiles_per_stream == tiles_per_stream - 1)
        def _():
            kt_ref[...] = vals["k"].T
            vt_ref[...] = vals["v"].T


def _proj(x2d, w, n_batch, prompt):
    n_rows = x2d.shape[0]
    tm = min(ROW_TILE, n_rows)
    n_steps = n_rows // tm
    per_b = n_rows // n_batch // tm if prompt else 1
    widths = {name: hi - lo for name, (lo, hi) in _SEG.items()}
    out_shape, out_specs = [], []
    for name in _PROJ_ORDER:
        wd = widths[name]
        if name == "uc" and prompt:
            out_shape.append(jax.ShapeDtypeStruct((n_rows // n_batch, n_batch * wd), F32))
            out_specs.append(pl.BlockSpec((tm, wd), lambda i: (i % per_b, i // per_b)))
        else:
            out_shape.append(jax.ShapeDtypeStruct((n_rows, wd), F32))
            out_specs.append(pl.BlockSpec((tm, wd), lambda i: (i, 0)))
    if prompt:
        assert tm == A_WINDOW
        for _ in range(2):
            out_shape.append(jax.ShapeDtypeStruct((n_batch, A_WIDTH, A_WINDOW), F32))
            out_specs.append(pl.BlockSpec((None, A_WIDTH, A_WINDOW), lambda i: (i // per_b, 0, 0)))
    return pl.pallas_call(
        functools.partial(_proj_kernel, tiles_per_stream=per_b),
        grid=(n_steps,),
        in_specs=[pl.BlockSpec((tm, D_MODEL), lambda i: (i, 0)), _const_spec(w.shape)],
        out_specs=out_specs,
        out_shape=out_shape,
        compiler_params=_params(("arbitrary",)),
        name="proj",
    )(x2d, w)


def _softmax_pv(q2s, key_segs, val_segs, bias_segs, valid_segs, feature_major=None):
    feature_major = feature_major or [False] * len(valid_segs)
    nn = (((1,), (0,)), ((), ()))
    nt = (((1,), (1,)), ((), ()))
    m_rows = q2s[0].shape[0]
    low = lax.broadcasted_iota(jnp.int32, q2s[0].shape, 1) < A_HEAD_DIM
    qs = [jnp.concatenate([jnp.where(low, q2, 0.0), jnp.where(low, 0.0, q2)], axis=0).astype(BF16) for q2 in q2s]
    scores = []
    for q, ksegs, biases in zip(qs, key_segs, bias_segs):
        sc = []
        for kseg, bias, valid, fm in zip(ksegs, biases, valid_segs, feature_major):
            s = lax.dot_general(q, kseg, nn if fm else nt, preferred_element_type=F32) + bias
            if valid is not None:
                s = jnp.where(valid, s, NEG_INF)
            sc.append(s)
        scores.append(sc)
    tops = [functools.reduce(jnp.maximum, [jnp.max(s, -1, keepdims=True) for s in sc]) for sc in scores]
    probs = [[jnp.exp(s - m) for s in sc] for sc, m in zip(scores, tops)]
    denoms = [functools.reduce(jnp.add, [jnp.sum(p, -1, keepdims=True) for p in pr]) for pr in probs]
    pvs = [functools.reduce(jnp.add, [lax.dot_general(p.astype(BF16), vseg, nt if fm else nn,
                                                      preferred_element_type=F32)
                                      for p, vseg, fm in zip(pr, vsegs, feature_major)])
           for pr, vsegs in zip(probs, val_segs)]
    outs = []
    for pv, denom in zip(pvs, denoms):
        pvn = pv * (1.0 / denom)
        outs.append(jnp.where(low, pvn[:m_rows], pvn[m_rows:]))
    return outs


def _attn_prompt_kernel(q_ref, kp_ref, kc_ref, vp_ref, vc_ref, bias_ref, o_ref, kk_ref, vv_ref):
    jb = pl.program_id(1)
    kk_ref[0:ATT_Q_BLOCK, :] = kp_ref[...].astype(BF16)
    kk_ref[ATT_Q_BLOCK:, :] = kc_ref[...].astype(BF16)
    vv_ref[0:ATT_Q_BLOCK, :] = vp_ref[...].astype(BF16)
    vv_ref[ATT_Q_BLOCK:, :] = vc_ref[...].astype(BF16)
    scale = A_HEAD_DIM ** -0.5
    for sub in range(ATT_Q_BLOCK // ATT_Q_SUB):
        r0 = sub * ATT_Q_SUB
        pos = lax.broadcasted_iota(jnp.int32, (1, ATT_K_SUB), 1) + ((jb - 1) * ATT_Q_BLOCK + r0)
        valid = pos >= 0
        pairs = range(A_HEADS // 2)
        cols = [slice(hp * LANES, (hp + 1) * LANES) for hp in pairs]
        outs = _softmax_pv([q_ref[r0:r0 + ATT_Q_SUB, c] * scale for c in cols],
                           [[kk_ref[r0:r0 + ATT_K_SUB, c]] for c in cols],
                           [[vv_ref[r0:r0 + ATT_K_SUB, c]] for c in cols],
                           [[bias_ref[hp]] for hp in pairs], [valid])
        for c, out in zip(cols, outs):
            o_ref[r0:r0 + ATT_Q_SUB, c] = out


def _attn_prompt(q, k, v, bias, n_batch):
    n_rows = q.shape[0]
    per_b = n_rows // n_batch // ATT_Q_BLOCK
    blk = (ATT_Q_BLOCK, A_WIDTH)
    cur = pl.BlockSpec(blk, lambda b, j: (b * per_b + j, 0))
    prev = pl.BlockSpec(blk, lambda b, j: (b * per_b + jnp.maximum(j - 1, 0), 0))
    return pl.pallas_call(
        _attn_prompt_kernel,
        grid=(n_batch, per_b),
        in_specs=[cur, prev, cur, prev, cur, _const_spec(bias.shape)],
        out_specs=cur,
        out_shape=jax.ShapeDtypeStruct((n_rows, A_WIDTH), F32),
        scratch_shapes=[pltpu.VMEM((2 * ATT_Q_BLOCK, A_WIDTH), BF16), pltpu.VMEM((2 * ATT_Q_BLOCK, A_WIDTH), BF16)],
        compiler_params=_params(("parallel", "arbitrary")),
        name="attn_prompt",
    )(q, k, k, v, v, bias)


def _attn_sample_kernel(q_ref, kn_ref, vn_ref, kc_ref, vc_ref, bias_c_ref, bias_n_ref, o_ref):
    scale = A_HEAD_DIM ** -0.5
    pairs = range(A_HEADS // 2)
    cols = [slice(hp * LANES, (hp + 1) * LANES) for hp in pairs]
    outs = _softmax_pv([q_ref[:, c] * scale for c in cols],
                       [[kc_ref[c, :].astype(BF16), kn_ref[:, c].astype(BF16)] for c in cols],
                       [[vc_ref[c, :].astype(BF16), vn_ref[:, c].astype(BF16)] for c in cols],
                       [[bias_c_ref[hp], bias_n_ref[hp]] for hp in pairs], [None, None], [True, False])
    for c, out in zip(cols, outs):
        o_ref[:, c] = out


def _attn_sample(q, k, v, k_cache, v_cache, layer, bias_c, bias_n, n_batch):
    n_rows = q.shape[0]
    s_len = n_rows // n_batch
    n_cache = k_cache.shape[3]
    new = pl.BlockSpec((s_len, A_WIDTH), lambda b: (b, 0))
    cache = pl.BlockSpec((None, None, A_WIDTH, n_cache), lambda b: (layer, b, 0, 0))
    return pl.pallas_call(
        _attn_sample_kernel,
        grid=(n_batch,),
        in_specs=[new, new, new, cache, cache, _const_spec(bias_c.shape), _const_spec(bias_n.shape)],
        out_specs=new,
        out_shape=jax.ShapeDtypeStruct((n_rows, A_WIDTH), F32),
        compiler_params=_params(("parallel",)),
        name="attn_sample",
    )(q, k, v, k_cache, v_cache, bias_c, bias_n)


def _rel_bias_blocks(table, n_q, n_k, q_offset, banded):
    period = n_q + n_k - 1
    rel = q_offset - ((np.arange(period) + n_q - 1) % period - (n_q - 1))
    vec = jnp.take(table.astype(F32), jnp.asarray(np.clip(rel, -MAX_REL, MAX_REL) + MAX_REL), axis=1)
    n_heads = table.shape[0]
    bias = jnp.tile(vec, (1, n_q))[:, :n_q * (period - 1)].reshape(n_heads, n_q, period - 1)[:, :, :n_k]
    if banded:
        i = np.arange(n_q)[:, None]
        jj = np.arange(n_k)[None, :] - (i // CHUNK) * CHUNK
        inband = (jj >= 0) & (jj < A_WINDOW + CHUNK)
        bias = jnp.where(jnp.asarray(inband), bias, NEG_INF)
    return bias.reshape(n_heads // 2, 2 * n_q, n_k)


def _delta_kernel(qkvb_ref, gate_ref, beta_ref, a_ref, cst_ref, s0_ref, cw_ref, cb_ref, alog_ref, dtb_ref, nw_ref,
                  o_ref, cout_ref, sout_ref, xbuf, s_ref, ob_ref, *, t_valid, n_rows):
    j = pl.program_id(1)
    n_blk = n_rows // PAIR
    hd = B_HEAD_DIM
    low = lax.broadcasted_iota(jnp.int32, (PAIR, LANES), 1) < hd
    low_s = lax.broadcasted_iota(jnp.int32, (hd, LANES), 1) < hd
    zeros_s = jnp.zeros((hd, LANES), F32)

    @pl.when(j == 0)
    def _():
        xbuf[0:SUBLANES, :] = cst_ref[0]
        for h in range(B_HEADS):
            s0 = s0_ref[0, h]
            s_ref[h] = jnp.concatenate([s0, jnp.zeros_like(s0)] if h % 2 == 0 else [jnp.zeros_like(s0), s0], axis=1)

    xbuf[SUBLANES:SUBLANES + n_rows, :] = qkvb_ref[...]
    t_last = n_rows - PAIR + t_valid
    cout_ref[0] = xbuf[t_last:t_last + SUBLANES, :]

    hrow = lax.broadcasted_iota(jnp.int32, (B_WIDTH, B_WIDTH), 0)
    hcol = lax.broadcasted_iota(jnp.int32, (B_WIDTH, B_WIDTH), 1)
    head_ones = jnp.where(_blk(hrow, B_HEAD_DIM) == _blk(hcol, B_HEAD_DIM), 1.0, 0.0).astype(BF16)

    def l2n(t):
        return t * lax.rsqrt(_dot_f32_by_exact(t * t, head_ones) + RMS_EPS)

    row = lax.broadcasted_iota(jnp.int32, (PAIR, PAIR), 0)
    col = lax.broadcasted_iota(jnp.int32, (PAIR, PAIR), 1)
    same = _blk(row, CHUNK) == _blk(col, CHUNK)
    tril = same & (row >= col)
    strict = same & (row > col)
    first = row < CHUNK
    ltri = jnp.where(tril, 1.0, 0.0).astype(BF16)
    n_pairs = B_HEADS // 2
    chunks = range(PAIR // CHUNK)

    def pair_lanes(t, p):
        return jnp.where(low, t[:, (2 * p) * LANES:(2 * p + 1) * LANES], t[:, (2 * p + 1) * LANES:(2 * p + 2) * LANES])

    def own(h, t, other):
        return jnp.where(low, t, other) if h % 2 == 0 else jnp.where(low, other, t)

    prepared, solved = {}, {}

    def prepare(n):
        r0 = n * PAIR
        conv = cb_ref[...]
        for w in range(CONV_W):
            start = SUBLANES - (CONV_W - 1) + w + r0
            conv = conv + xbuf[start:start + PAIR, :] * cw_ref[w:w + 1, :]
        conv = _silu(conv)
        yield
        qn = l2n(conv[:, 0:B_WIDTH]) * (B_HEAD_DIM ** -0.5)
        yield
        kn = l2n(conv[:, B_WIDTH:2 * B_WIDTH])
        vv = conv[:, 2 * B_WIDTH:3 * B_WIDTH]
        yield
        beta = jax.nn.sigmoid(beta_ref[r0:r0 + PAIR, :])
        g = -jnp.exp(alog_ref[...]) * _softplus(a_ref[r0:r0 + PAIR, :] + dtb_ref[...])
        if t_valid < PAIR and n == n_blk - 1:
            def live(t):
                return jnp.where(lax.broadcasted_iota(jnp.int32, t.shape, 0) < t_valid, t, 0.0)
            kn, vv, beta, g = live(kn), live(vv), live(beta), live(g)
        yield
        gc = _dot_exact_by_f32(ltri, g)
        g_last = [gc[(c + 1) * CHUNK - 1:(c + 1) * CHUNK, :] for c in chunks]
        gl = jnp.concatenate([jnp.broadcast_to(t, (CHUNK, t.shape[1])) for t in g_last], axis=0)
        egc = jnp.exp(gc)
        erest = jnp.exp(gl - gc)
        d = {"egl": [jnp.exp(t) for t in g_last], "k": [], "q": [], "kb": [], "vb": [], "kbg_r": [], "qd_r": [],
             "kd": [], "decay": [], "decay_strict": []}
        yield
        for p in range(n_pairs):
            ls = slice(p * LANES, (p + 1) * LANES)
            k, q, v, b = kn[:, ls], qn[:, ls], vv[:, ls], beta[:, ls]
            eg = pair_lanes(egc, p)
            d["k"].append(k)
            d["q"].append(q)
            d["kb"].append(k * b)
            d["vb"].append(v * b)
            d["kbg_r"].append(pltpu.roll(k * b * eg, hd, axis=1))
            d["qd_r"].append(pltpu.roll(q * eg, hd, axis=1))
            d["kd"].append(k * pair_lanes(erest, p))
            yield
        for h in range(B_HEADS):
            gcol = gc[:, h * LANES:(h + 1) * LANES]
            e = jnp.exp(gcol - gcol.T)
            d["decay"].append(jnp.where(tril, e, 0.0))
            d["decay_strict"].append(jnp.where(strict, e, 0.0))
            yield
        prepared[n] = d
    heads = range(B_HEADS)

    def solve(n):
        d = prepared[n]
        a_low = [_bdot_nt(own(h, d["kb"][h // 2], 0.0), d["k"][h // 2]) * d["decay_strict"][h] for h in heads]
        yield
        base = 8
        diag = _blk(row, base) == _blk(col, base)
        eye = jnp.where(row == col, 1.0, 0.0)
        a0 = [jnp.where(diag, a, 0.0) for a in a_low]
        xs = [eye - a for a in a0]
        ps = [_bdot(a, a) for a in a0]
        yield
        xs = [x + _bdot(x, p) for x, p in zip(xs, ps)]
        ps = [_bdot(p, p) for p in ps]
        yield
        xs = [x + _bdot(x, p) for x, p in zip(xs, ps)]
        yield
        bs = base
        while bs < CHUNK:
            off = (_blk(row, 2 * bs) == _blk(col, 2 * bs)) & (_blk(row, bs) != _blk(col, bs))
            ts = [_bdot(jnp.where(off, a, 0.0), x) for a, x in zip(a_low, xs)]
            yield
            xs = [x - _bdot(x, t) for x, t in zip(xs, ts)]
            yield
            bs *= 2
        uws = [_bdot(x, own(h, d["vb"][h // 2], d["kbg_r"][h // 2])) for h, x in zip(heads, xs)]
        yield
        qus = []
        for h in heads:
            qk = _bdot_nt(own(h, d["q"][h // 2], 0.0), d["k"][h // 2]) * d["decay"][h]
            qus.append(_bdot(qk, uws[h]))
        yield
        kus = []
        for p in range(n_pairs):
            rhs = jnp.concatenate([jnp.where(first if c == 0 else ~first, uws[2 * p + e], 0.0)
                                   for e in range(2) for c in chunks], axis=1)
            kus.append(_bdot(d["kd"][p].T, rhs))
            yield
        solved[n] = (qus, kus)

    states = [s_ref[h] for h in heads]

    def advance(n):
        d = prepared[n]
        qus, kus = solved[n]
        for c in chunks:
            outs = []
            for h in heads:
                p, e = divmod(h, 2)
                ku = kus[p][e * hd:(e + 1) * hd, (2 * e + c) * LANES:(2 * e + c + 1) * LANES]
                qu = qus[h][c * CHUNK:(c + 1) * CHUNK, :]
                kq = jnp.concatenate([ku, d["qd_r"][p][c * CHUNK:(c + 1) * CHUNK] - qu], axis=0)
                s_ext = jnp.concatenate([zeros_s, states[h]] if e == 0 else [states[h], zeros_s], axis=0)
                r = _bdot(kq, s_ext)
                gt = d["egl"][c][:, h * LANES:(h + 1) * LANES]
                new = gt * states[h] - r[:hd] + ku
                states[h] = jnp.where(low_s, new, 0.0) if e == 0 else jnp.where(low_s, 0.0, new)
                outs.append(r[hd:] + qu)
            r0 = n * PAIR + c * CHUNK
            for p in range(n_pairs):
                ob_ref[r0:r0 + CHUNK, p * LANES:(p + 1) * LANES] = jnp.where(low_s, outs[2 * p], outs[2 * p + 1])
            yield
        rs = slice(n * PAIR, (n + 1) * PAIR)
        ob = ob_ref[rs, :]
        ms = _dot_f32_by_exact(ob * ob, head_ones) * (1.0 / hd)
        o_ref[rs, :] = ob * lax.rsqrt(ms + RMS_EPS) * nw_ref[...] * _silu(gate_ref[rs, :])
        yield

    def advance_all(blocks):
        for n in blocks:
            yield from advance(n)

    group = 2 if n_blk % 2 == 0 else 1
    n_groups = n_blk // group
    for slot in range(n_groups + 2):
        active = []
        if 0 <= slot - 2 < n_groups:
            active.append(advance_all(range((slot - 2) * group, (slot - 1) * group)))
        if 0 <= slot - 1 < n_groups:
            active += [solve(n) for n in range((slot - 1) * group, slot * group)]
        if slot < n_groups:
            active += [prepare(n) for n in range(slot * group, (slot + 1) * group)]
        while active:
            for gen in list(active):
                if next(gen, "done") == "done":
                    active.remove(gen)

    for h in heads:
        s_ref[h] = states[h]
    xbuf[0:SUBLANES, :] = xbuf[n_rows:n_rows + SUBLANES, :]

    @pl.when(j == pl.num_programs(1) - 1)
    def _():
        for h in range(B_HEADS):
            sout_ref[0, h] = s_ref[h][:, (h % 2) * hd:(h % 2 + 1) * hd]


def _delta(qkvb, gate, beta, a, conv_state, s0, lw, n_batch, t_valid, rows_per_step):
    n_rows = qkvb.shape[0]
    per_b = n_rows // n_batch // rows_per_step

    def rows(width):
        return pl.BlockSpec((rows_per_step, width), lambda b, j: (b * per_b + j, 0))

    cst = pl.BlockSpec((1, SUBLANES, 3 * B_WIDTH), lambda b, j: (b, 0, 0))
    st = pl.BlockSpec((1, B_HEADS, B_HEAD_DIM, B_HEAD_DIM), lambda b, j: (b, 0, 0, 0))
    consts = [lw["conv_w"], lw["conv_b"], lw["a_log"], lw["dt_bias"], lw["norm_w"]]
    return pl.pallas_call(
        functools.partial(_delta_kernel, t_valid=t_valid, n_rows=rows_per_step),
        grid=(n_batch, per_b),
        in_specs=[rows(3 * B_WIDTH), rows(B_WIDTH), rows(B_WIDTH), rows(B_HEADS * LANES), cst, st]
        + [_const_spec(c.shape) for c in consts],
        out_specs=[rows(B_WIDTH), cst, st],
        out_shape=[jax.ShapeDtypeStruct((n_rows, B_WIDTH), F32),
                   jax.ShapeDtypeStruct((n_batch, SUBLANES, 3 * B_WIDTH), F32),
                   jax.ShapeDtypeStruct((n_batch, B_HEADS, B_HEAD_DIM, B_HEAD_DIM), F32)],
        scratch_shapes=[pltpu.VMEM((SUBLANES + rows_per_step, 3 * B_WIDTH), F32),
                        pltpu.VMEM((B_HEADS, B_HEAD_DIM, LANES), F32),
                        pltpu.VMEM((rows_per_step, B_WIDTH), F32)],
        compiler_params=_params(("parallel", "arbitrary")),
        name="delta",
    )(qkvb, gate, beta, a, conv_state, s0, *consts)


def _s5_kernel(u_ref, h0r_ref, h0i_ref, lr_ref, li_ref, bre_ref, bim_ref, cre_ref, cim_ref, d_ref, gw_ref, gb_ref,
               o_ref, hro_ref, hio_ref, xr, xi, hs, *, t_block):
    tb = pl.program_id(1)

    @pl.when(tb == 0)
    def _():
        hs[0] = h0r_ref[0]
        hs[1] = h0i_ref[0]

    u = u_ref[0]
    ub = u.astype(BF16)
    xr[...] = jnp.dot(ub, bre_ref[...], preferred_element_type=F32)
    xi[...] = jnp.dot(ub, bim_ref[...], preferred_element_type=F32)
    lam_r = jnp.broadcast_to(lr_ref[...], (S5_BATCH, C_LANES))
    lam_i = jnp.broadcast_to(li_ref[...], (S5_BATCH, C_LANES))

    def step(t, carry):
        h_r, h_i = carry
        rows = pl.ds(pl.multiple_of(t * S5_BATCH, S5_BATCH), S5_BATCH)
        n_r = lam_r * h_r - lam_i * h_i + xr[rows, :]
        n_i = lam_r * h_i + lam_i * h_r + xi[rows, :]
        xr[rows, :] = n_r
        xi[rows, :] = n_i
        return n_r, n_i

    h_r, h_i = lax.fori_loop(0, t_block, step, (hs[0], hs[1]))
    hs[0] = h_r
    hs[1] = h_i
    hro_ref[0] = h_r
    hio_ref[0] = h_i
    y = (jnp.dot(xr[...].astype(BF16), cre_ref[...], preferred_element_type=F32)
         - jnp.dot(xi[...].astype(BF16), cim_ref[...], preferred_element_type=F32) + d_ref[...] * u)
    z = _gelu_tanh(y)
    o_ref[0] = z * jax.nn.sigmoid(jnp.dot(z.astype(BF16), gw_ref[...], preferred_element_type=F32) + gb_ref[...])


def _s5(u, h0_re, h0_im, lw, t_block):
    n_groups, n_rows, _ = u.shape
    rows_blk = t_block * S5_BATCH
    n_tb = n_rows // rows_blk
    u_spec = pl.BlockSpec((1, rows_blk, C_WIDTH), lambda g, t: (g, t, 0))
    h_spec = pl.BlockSpec((1, S5_BATCH, C_LANES), lambda g, t: (g, 0, 0))
    consts = [lw["lam_re"], lw["lam_im"], lw["b_re"], lw["b_im"], lw["c_re"], lw["c_im"], lw["c_d"],
              lw["glu_w"], lw["glu_b"]]
    return pl.pallas_call(
        functools.partial(_s5_kernel, t_block=t_block),
        grid=(n_groups, n_tb),
        in_specs=[u_spec, h_spec, h_spec] + [_const_spec(c.shape) for c in consts],
        out_specs=[u_spec, h_spec, h_spec],
        out_shape=[jax.ShapeDtypeStruct(u.shape, F32),
                   jax.ShapeDtypeStruct(h0_re.shape, F32), jax.ShapeDtypeStruct(h0_im.shape, F32)],
        scratch_shapes=[pltpu.VMEM((rows_blk, C_LANES), F32), pltpu.VMEM((rows_blk, C_LANES), F32),
                        pltpu.VMEM((2, S5_BATCH, C_LANES), F32)],
        compiler_params=_params(("parallel", "arbitrary")),
        name="s5",
    )(u, h0_re, h0_im, *consts)


def _post_kernel(x_ref, a_ref, b_ref, c_ref, wo_ref, g1_ref, b1_ref, wu_ref, bu_ref, wd_ref, g2_ref, b2_ref, o_ref):
    mix = (jnp.dot(a_ref[...].astype(BF16), wo_ref[0:A_WIDTH, :], preferred_element_type=F32)
           + jnp.dot(b_ref[...].astype(BF16), wo_ref[A_WIDTH:A_WIDTH + B_WIDTH, :], preferred_element_type=F32)
           + jnp.dot(c_ref[...].astype(BF16), wo_ref[A_WIDTH + B_WIDTH:, :], preferred_element_type=F32))
    x1 = _layer_norm(DN_ALPHA * x_ref[...] + mix, g1_ref[...], b1_ref[...])
    x1b = x1.astype(BF16)
    acc = jnp.zeros(x1.shape, F32)
    for s in range(D_FF // FF_SLAB):
        cs = slice(s * FF_SLAB, (s + 1) * FF_SLAB)
        hid = jnp.dot(x1b, wu_ref[:, cs], preferred_element_type=F32) + bu_ref[:, cs]
        hid = jnp.square(jnp.maximum(hid, 0.0))
        acc = acc + jnp.dot(hid.astype(BF16), wd_ref[cs, :], preferred_element_type=F32)
    o_ref[...] = _layer_norm(DN_ALPHA * x1 + acc, g2_ref[...], b2_ref[...])


def _post(x2d, out_a, out_b, out_c, lw, n_batch, c_time_major):
    n_rows = x2d.shape[0]
    tm = min(ROW_TILE, n_rows)

    def rows(width):
        return pl.BlockSpec((tm, width), lambda i: (i, 0))

    if c_time_major:
        per_b = n_rows // n_batch // tm
        c_spec = pl.BlockSpec((tm, C_WIDTH), lambda i: (i % per_b, i // per_b))
    else:
        c_spec = rows(C_WIDTH)
    consts = [lw["w_out"], lw["ln1_g"], lw["ln1_b"], lw["w_up"], lw["b_up"], lw["w_down"], lw["ln2_g"], lw["ln2_b"]]
    return pl.pallas_call(
        _post_kernel,
        grid=(n_rows // tm,),
        in_specs=[rows(D_MODEL), rows(A_WIDTH), rows(B_WIDTH), c_spec] + [_const_spec(c.shape) for c in consts],
        out_specs=rows(D_MODEL),
        out_shape=jax.ShapeDtypeStruct((n_rows, D_MODEL), F32),
        compiler_params=_params(("parallel",)),
        name="post",
    )(x2d, out_a, out_b, out_c, *consts)


def _layer_weights(l, w_in, a_rel_bias, b_conv_w, b_conv_b, b_a_log, b_dt_bias, b_norm_w, c_a_re, c_a_im, c_log_dt,
                   c_b_re, c_b_im, c_c_re, c_c_im, c_d, c_glu_w, c_glu_b, w_out, ln1_g, ln1_b, w_up, b_up, w_down,
                   ln2_g, ln2_b, s_len, n_cache):
    wi = w_in[l]
    o_q, o_k, o_v = 0, A_WIDTH, 2 * A_WIDTH
    o_qkvb = 3 * A_WIDTH
    o_beta = o_qkvb + 3 * B_WIDTH
    o_a = o_beta + B_HEADS
    o_gate = o_a + B_HEADS
    o_uc = o_gate + B_WIDTH
    per_head = lambda cols: jnp.repeat(cols, B_HEAD_DIM, axis=-1)
    per_group = lambda cols: jnp.repeat(cols, LANES, axis=-1)
    w_proj = jnp.concatenate([
        wi[:, o_q:o_q + A_WIDTH], wi[:, o_k:o_k + A_WIDTH], wi[:, o_v:o_v + A_WIDTH],
        wi[:, o_qkvb:o_qkvb + 3 * B_WIDTH], wi[:, o_gate:o_gate + B_WIDTH], wi[:, o_uc:o_uc + C_WIDTH],
        per_head(wi[:, o_beta:o_beta + B_HEADS]), per_group(wi[:, o_a:o_a + B_HEADS])], axis=1).astype(BF16)

    table = a_rel_bias[l]
    bias_sample = _rel_bias_blocks(table, s_len, n_cache + s_len, n_cache, banded=False)

    a_re, a_im = c_a_re[l].astype(F32), c_a_im[l].astype(F32)
    dt = jnp.exp(c_log_dt[l].astype(F32))[:, None]
    mag = jnp.exp(dt * a_re)
    lam_re, lam_im = mag * jnp.cos(dt * a_im), mag * jnp.sin(dt * a_im)
    den = a_re * a_re + a_im * a_im
    coef_re = ((lam_re - 1.0) * a_re + lam_im * a_im) / den
    coef_im = (lam_im * a_re - (lam_re - 1.0) * a_im) / den
    bre, bim = c_b_re[l].astype(F32), c_b_im[l].astype(F32)
    bb_re = coef_re[..., None] * bre - coef_im[..., None] * bim
    bb_im = coef_re[..., None] * bim + coef_im[..., None] * bre
    eye = jnp.eye(C_GROUPS, dtype=F32)
    in_bd = lambda t: jnp.einsum("gph,gk->ghkp", t, eye).reshape(C_WIDTH, C_LANES).astype(BF16)
    out_bd = lambda t: jnp.einsum("ghp,gk->gpkh", t.astype(F32), eye).reshape(C_LANES, C_WIDTH).astype(BF16)
    row = lambda t: t.astype(F32).reshape(1, -1)

    return {
        "w_proj": w_proj,
        "bias_prompt": _rel_bias_blocks(table, ATT_Q_SUB, ATT_K_SUB, A_WINDOW, banded=True),
        "bias_cache": bias_sample[:, :, :n_cache], "bias_new": bias_sample[:, :, n_cache:],
        "conv_w": b_conv_w[l].astype(F32), "conv_b": row(b_conv_b[l]),
        "a_log": row(per_group(b_a_log[l])), "dt_bias": row(per_group(b_dt_bias[l])),
        "norm_w": row(jnp.tile(b_norm_w[l], B_HEADS)),
        "lam_re": row(lam_re), "lam_im": row(lam_im), "b_re": in_bd(bb_re), "b_im": in_bd(bb_im),
        "c_re": out_bd(c_c_re[l]), "c_im": out_bd(c_c_im[l]), "c_d": row(c_d[l]),
        "glu_w": c_glu_w[l].astype(BF16), "glu_b": row(c_glu_b[l]),
        "w_out": w_out[l].astype(BF16), "ln1_g": row(ln1_g[l]), "ln1_b": row(ln1_b[l]),
        "w_up": w_up[l].astype(BF16), "b_up": row(b_up[l]), "w_down": w_down[l].astype(BF16),
        "ln2_g": row(ln2_g[l]), "ln2_b": row(ln2_b[l]),
    }


def _pad_conv_state(conv_buf):
    return jnp.pad(conv_buf.astype(F32), ((0, 0), (SUBLANES - (CONV_W - 1), 0), (0, 0)))


def _prompt_layer(x2d, lw, n_batch):
    t_len = x2d.shape[0] // n_batch
    q, k, v, qkvb, gate, uc, beta, a, k_t, v_t = _proj(x2d, lw["w_proj"], n_batch, prompt=True)
    out_a = _attn_prompt(q, k, v, lw["bias_prompt"], n_batch)
    out_b, conv_o, s_new = _delta(
        qkvb, gate, beta, a, jnp.zeros((n_batch, SUBLANES, 3 * B_WIDTH), F32),
        jnp.zeros((n_batch, B_HEADS, B_HEAD_DIM, B_HEAD_DIM), F32), lw, n_batch, t_valid=PAIR, rows_per_step=DELTA_ROWS)
    zeros_h = jnp.zeros((1, S5_BATCH, C_LANES), F32)
    out_c, h_re, h_im = _s5(uc.reshape(1, t_len * n_batch, C_WIDTH), zeros_h, zeros_h, lw, S5_TIME_BLOCK)
    y = _post(x2d, out_a, out_b, out_c.reshape(t_len, n_batch * C_WIDTH), lw, n_batch, c_time_major=True)
    heads = lambda t: t.reshape(n_batch, A_HEADS, A_HEAD_DIM, A_WINDOW).transpose(0, 3, 1, 2)
    state = lambda t: t.reshape(n_batch, C_GROUPS, C_STATE)
    return y, (heads(k_t), heads(v_t), conv_o[:, -(CONV_W - 1):], s_new, state(h_re), state(h_im))


def _sample_layer(x2d, lw, n_batch, layer, k_cache, v_cache, conv_buf, s0, h0_re, h0_im):
    s_len = x2d.shape[0] // n_batch
    n_groups = n_batch // S5_BATCH
    q, k, v, qkvb, gate, uc, beta, a = _proj(x2d, lw["w_proj"], n_batch, prompt=False)
    out_a = _attn_sample(q, k, v, k_cache, v_cache, layer, lw["bias_cache"], lw["bias_new"], n_batch)

    def pad_rows(t):
        t = t.reshape(n_batch, s_len, -1)
        return jnp.pad(t, ((0, 0), (0, PAIR - s_len), (0, 0))).reshape(n_batch * PAIR, -1)

    out_b, conv_o, s_new = _delta(pad_rows(qkvb), pad_rows(gate), pad_rows(beta), pad_rows(a),
                                  _pad_conv_state(conv_buf), s0.astype(F32), lw, n_batch, t_valid=s_len, rows_per_step=PAIR)
    out_b = out_b.reshape(n_batch, PAIR, B_WIDTH)[:, :s_len].reshape(n_batch * s_len, B_WIDTH)

    u = uc.reshape(n_groups, S5_BATCH, s_len, C_WIDTH).transpose(0, 2, 1, 3).reshape(n_groups, s_len * S5_BATCH, C_WIDTH)
    grp = lambda t: t.astype(F32).reshape(n_groups, S5_BATCH, C_LANES)
    out_c, h_re, h_im = _s5(u, grp(h0_re), grp(h0_im), lw, s_len)
    out_c = out_c.reshape(n_groups, s_len, S5_BATCH, C_WIDTH).transpose(0, 2, 1, 3).reshape(n_batch * s_len, C_WIDTH)
    y = _post(x2d, out_a, out_b, out_c, lw, n_batch, c_time_major=False)
    heads = lambda t: t.reshape(n_batch, s_len, A_HEADS, A_HEAD_DIM)
    state = lambda t: t.reshape(n_batch, C_GROUPS, C_STATE)
    return y, (heads(k), heads(v), conv_o[:, -(CONV_W - 1):], s_new, state(h_re), state(h_im))


def kernel(x_prompt, x_sample, cache_a_k, cache_a_v, state_b_conv, state_b_ssm, state_c_re, state_c_im, w_in, a_rel_bias, b_conv_w, b_conv_b, b_a_log, b_dt_bias, b_norm_w, c_a_re, c_a_im, c_log_dt, c_b_re, c_b_im, c_c_re, c_c_im, c_d, c_glu_w, c_glu_b, w_out, ln1_g, ln1_b, w_up, b_up, w_down, ln2_g, ln2_b):
    n_p, t_p, _ = x_prompt.shape
    n_s, t_s, _ = x_sample.shape
    n_cache = cache_a_k.shape[2]
    yp = x_prompt.reshape(n_p * t_p, D_MODEL)
    ys = x_sample.reshape(n_s * t_s, D_MODEL)
    feat_major = lambda t: t.astype(F32).transpose(0, 1, 3, 4, 2).reshape(DEPTH, n_s, A_WIDTH, n_cache)
    k_cache, v_cache = feat_major(cache_a_k), feat_major(cache_a_v)
    p_out, s_out = [], []
    for l in range(DEPTH):
        lw = _layer_weights(l, w_in, a_rel_bias, b_conv_w, b_conv_b, b_a_log, b_dt_bias, b_norm_w, c_a_re, c_a_im,
                            c_log_dt, c_b_re, c_b_im, c_c_re, c_c_im, c_d, c_glu_w, c_glu_b, w_out, ln1_g, ln1_b,
                            w_up, b_up, w_down, ln2_g, ln2_b, t_s, n_cache)
        yp, st_p = _prompt_layer(yp, lw, n_p)
        ys, st_s = _sample_layer(ys, lw, n_s, l, k_cache, v_cache, state_b_conv[l], state_b_ssm[l],
                                 state_c_re[l], state_c_im[l])
        p_out.append(st_p)
        s_out.append(st_s)
    stack = lambda outs, i: jnp.stack([o[i] for o in outs])
    return (yp.reshape(n_p, t_p, D_MODEL), ys.reshape(n_s, t_s, D_MODEL),
            *[stack(p_out, i) for i in range(6)], *[stack(s_out, i) for i in range(6)])
```

```python
import functools

import jax
import jax.numpy as jnp
import numpy as np
from jax import lax
from jax.experimental import pallas as pl
from jax.experimental.pallas import tpu as pltpu

F32 = jnp.float32
BF16 = jnp.bfloat16

D_MODEL = 1024
DEPTH = 2
CHUNK = 64
A_HEADS = 8
A_HEAD_DIM = 64
A_WIDTH = A_HEADS * A_HEAD_DIM
A_WINDOW = 8 * CHUNK
MAX_REL = 256
B_HEADS = 4
B_HEAD_DIM = 64
B_WIDTH = B_HEADS * B_HEAD_DIM
CONV_W = 4
C_WIDTH = D_MODEL - A_WIDTH - B_WIDTH
C_GROUP = 16
C_GROUPS = C_WIDTH // C_GROUP
C_STATE = 64
C_LANES = C_GROUPS * C_STATE
D_FF = 4 * D_MODEL
DN_ALPHA = (2.0 * DEPTH) ** 0.25
LN_EPS = 1e-5
RMS_EPS = 1e-6
NEG_INF = -1e30

SUBLANES = 8
LANES = 128
VMEM_LIMIT_BYTES = 56 * 1024 * 1024

ROW_TILE = 512
ATT_Q_BLOCK = 512
ATT_Q_SUB = 2 * CHUNK
ATT_K_SUB = ATT_Q_SUB + A_WINDOW
PAIR = 2 * CHUNK
DELTA_ROWS = 8 * PAIR
S5_TIME_BLOCK = 128
S5_BATCH = SUBLANES
FF_SLAB = 1024

_SEG = {}
_off = 0
for _name, _w in (("q", A_WIDTH), ("k", A_WIDTH), ("v", A_WIDTH), ("qkvb", 3 * B_WIDTH), ("gate", B_WIDTH),
                  ("uc", C_WIDTH), ("beta", B_WIDTH), ("a", B_HEADS * LANES)):
    _SEG[_name] = (_off, _off + _w)
    _off += _w
PROJ_WIDTH = _off


def _bdot(a, b):
    return jnp.dot(a.astype(BF16), b.astype(BF16), preferred_element_type=F32)


def _bdot_nt(a, b):
    return lax.dot_general(a.astype(BF16), b.astype(BF16), (((1,), (1,)), ((), ())), preferred_element_type=F32)


def _split3(a):
    hi = a.astype(BF16)
    r = a - hi.astype(F32)
    mid = r.astype(BF16)
    lo = (r - mid.astype(F32)).astype(BF16)
    return hi, mid, lo


def _dot_f32_by_exact(a, e):
    lhs = jnp.concatenate(_split3(a), axis=1)
    return jnp.dot(lhs, jnp.concatenate([e, e, e], axis=0), preferred_element_type=F32)


def _dot_exact_by_f32(e, a):
    rhs = jnp.concatenate(_split3(a), axis=0)
    return jnp.dot(jnp.concatenate([e, e, e], axis=1), rhs, preferred_element_type=F32)


def _blk(idx, size):
    return jnp.bitwise_and(idx, -size)


def _layer_norm(x, g, b):
    mu = jnp.mean(x, -1, keepdims=True)
    xc = x - mu
    var = jnp.mean(xc * xc, -1, keepdims=True)
    return xc * lax.rsqrt(var + LN_EPS) * g + b


def _silu(x):
    return x * jax.nn.sigmoid(x)


def _softplus(x):
    return jnp.maximum(x, 0.0) + jnp.log1p(jnp.exp(-jnp.abs(x)))


def _gelu_tanh(x):
    c = float(np.sqrt(2.0 / np.pi))
    return x * (0.5 * (1.0 + jnp.tanh(c * (x + 0.044715 * (x * x * x)))))


def _const_spec(shape):
    nd = len(shape)
    return pl.BlockSpec(shape, lambda *_: (0,) * nd, pipeline_mode=pl.Buffered(1))


def _params(semantics):
    return pltpu.CompilerParams(dimension_semantics=semantics, vmem_limit_bytes=VMEM_LIMIT_BYTES)


_PROJ_ORDER = ("q", "k", "v", "qkvb", "gate", "uc", "beta", "a")


def _proj_kernel(x_ref, w_ref, *out_refs, tiles_per_stream):
    xb = x_ref[...].astype(BF16)
    vals = {}
    for name, ref in zip(_PROJ_ORDER, out_refs):
        lo, hi = _SEG[name]
        vals[name] = jnp.dot(xb, w_ref[:, lo:hi], preferred_element_type=F32)
        ref[...] = vals[name]
    if len(out_refs) > len(_PROJ_ORDER):
        kt_ref, vt_ref = out_refs[len(_PROJ_ORDER):]

        @pl.when(pl.program_id(0) % tiles_per_stream == tiles_per_stream - 1)
        def _():
            kt_ref[...] = vals["k"].T
            vt_ref[...] = vals["v"].T


def _proj(x2d, w, n_batch, prompt):
    n_rows = x2d.shape[0]
    tm = min(ROW_TILE, n_rows)
    n_steps = n_rows // tm
    per_b = n_rows // n_batch // tm if prompt else 1
    widths = {name: hi - lo for name, (lo, hi) in _SEG.items()}
    out_shape, out_specs = [], []
    for name in _PROJ_ORDER:
        wd = widths[name]
        out_shape.append(jax.ShapeDtypeStruct((n_rows, wd), F32))
        out_specs.append(pl.BlockSpec((tm, wd), lambda i: (i, 0)))
    if prompt:
        assert tm == A_WINDOW
        for _ in range(2):
            out_shape.append(jax.ShapeDtypeStruct((n_batch, A_WIDTH, A_WINDOW), F32))
            out_specs.append(pl.BlockSpec((None, A_WIDTH, A_WINDOW), lambda i: (i // per_b, 0, 0)))
    return pl.pallas_call(
        functools.partial(_proj_kernel, tiles_per_stream=per_b),
        grid=(n_steps,),
        in_specs=[pl.BlockSpec((tm, D_MODEL), lambda i: (i, 0)), _const_spec(w.shape)],
        out_specs=out_specs,
        out_shape=out_shape,
        compiler_params=_params(("arbitrary",)),
        name="proj",
    )(x2d, w)


def _softmax_pv(q2s, key_segs, val_segs, bias_segs, valid_segs, feature_major=None):
    feature_major = feature_major or [False] * len(valid_segs)
    nn = (((1,), (0,)), ((), ()))
    nt = (((1,), (1,)), ((), ()))
    m_rows = q2s[0].shape[0]
    low = lax.broadcasted_iota(jnp.int32, q2s[0].shape, 1) < A_HEAD_DIM
    qs = [jnp.concatenate([jnp.where(low, q2, 0.0), jnp.where(low, 0.0, q2)], axis=0).astype(BF16) for q2 in q2s]
    scores = []
    for q, ksegs, biases in zip(qs, key_segs, bias_segs):
        sc = []
        for kseg, bias, valid, fm in zip(ksegs, biases, valid_segs, feature_major):
            s = lax.dot_general(q, kseg, nn if fm else nt, preferred_element_type=F32) + bias
            if valid is not None:
                s = jnp.where(valid, s, NEG_INF)
            sc.append(s)
        scores.append(sc)
    tops = [functools.reduce(jnp.maximum, [jnp.max(s, -1, keepdims=True) for s in sc]) for sc in scores]
    probs = [[jnp.exp(s - m) for s in sc] for sc, m in zip(scores, tops)]
    denoms = [functools.reduce(jnp.add, [jnp.sum(p, -1, keepdims=True) for p in pr]) for pr in probs]
    pvs = [functools.reduce(jnp.add, [lax.dot_general(p.astype(BF16), vseg, nt if fm else nn,
                                                      preferred_element_type=F32)
                                      for p, vseg, fm in zip(pr, vsegs, feature_major)])
           for pr, vsegs in zip(probs, val_segs)]
    outs = []
    for pv, denom in zip(pvs, denoms):
        pvn = pv * (1.0 / denom)
        outs.append(jnp.where(low, pvn[:m_rows], pvn[m_rows:]))
    return outs


def _attn_prompt_kernel(q_ref, kp_ref, kc_ref, vp_ref, vc_ref, bias_ref, o_ref, kk_ref, vv_ref):
    jb = pl.program_id(1)
    kk_ref[0:ATT_Q_BLOCK, :] = kp_ref[...].astype(BF16)
    kk_ref[ATT_Q_BLOCK:, :] = kc_ref[...].astype(BF16)
    vv_ref[0:ATT_Q_BLOCK, :] = vp_ref[...].astype(BF16)
    vv_ref[ATT_Q_BLOCK:, :] = vc_ref[...].astype(BF16)
    scale = A_HEAD_DIM ** -0.5
    for sub in range(ATT_Q_BLOCK // ATT_Q_SUB):
        r0 = sub * ATT_Q_SUB
        pos = lax.broadcasted_iota(jnp.int32, (1, ATT_K_SUB), 1) + ((jb - 1) * ATT_Q_BLOCK + r0)
        valid = pos >= 0
        pairs = range(A_HEADS // 2)
        cols = [slice(hp * LANES, (hp + 1) * LANES) for hp in pairs]
        outs = _softmax_pv([q_ref[r0:r0 + ATT_Q_SUB, c] * scale for c in cols],
                           [[kk_ref[r0:r0 + ATT_K_SUB, c]] for c in cols],
                           [[vv_ref[r0:r0 + ATT_K_SUB, c]] for c in cols],
                           [[bias_ref[hp]] for hp in pairs], [valid])
        for c, out in zip(cols, outs):
            o_ref[r0:r0 + ATT_Q_SUB, c] = out


def _attn_prompt(q, k, v, bias, n_batch):
    n_rows = q.shape[0]
    per_b = n_rows // n_batch // ATT_Q_BLOCK
    blk = (ATT_Q_BLOCK, A_WIDTH)
    cur = pl.BlockSpec(blk, lambda b, j: (b * per_b + j, 0))
    prev = pl.BlockSpec(blk, lambda b, j: (b * per_b + jnp.maximum(j - 1, 0), 0))
    return pl.pallas_call(
        _attn_prompt_kernel,
        grid=(n_batch, per_b),
        in_specs=[cur, prev, cur, prev, cur, _const_spec(bias.shape)],
        out_specs=cur,
        out_shape=jax.ShapeDtypeStruct((n_rows, A_WIDTH), F32),
        scratch_shapes=[pltpu.VMEM((2 * ATT_Q_BLOCK, A_WIDTH), BF16), pltpu.VMEM((2 * ATT_Q_BLOCK, A_WIDTH), BF16)],
        compiler_params=_params(("parallel", "arbitrary")),
        name="attn_prompt",
    )(q, k, k, v, v, bias)


def _attn_sample_kernel(q_ref, kn_ref, vn_ref, kc_ref, vc_ref, bias_c_ref, bias_n_ref, o_ref):
    scale = A_HEAD_DIM ** -0.5
    pairs = range(A_HEADS // 2)
    cols = [slice(hp * LANES, (hp + 1) * LANES) for hp in pairs]
    outs = _softmax_pv([q_ref[:, c] * scale for c in cols],
                       [[kc_ref[c, :].astype(BF16), kn_ref[:, c].astype(BF16)] for c in cols],
                       [[vc_ref[c, :].astype(BF16), vn_ref[:, c].astype(BF16)] for c in cols],
                       [[bias_c_ref[hp], bias_n_ref[hp]] for hp in pairs], [None, None], [True, False])
    for c, out in zip(cols, outs):
        o_ref[:, c] = out


def _attn_sample(q, k, v, k_cache, v_cache, layer, bias_c, bias_n, n_batch):
    n_rows = q.shape[0]
    s_len = n_rows // n_batch
    n_cache = k_cache.shape[3]
    new = pl.BlockSpec((s_len, A_WIDTH), lambda b: (b, 0))
    cache = pl.BlockSpec((None, None, A_WIDTH, n_cache), lambda b: (layer, b, 0, 0))
    return pl.pallas_call(
        _attn_sample_kernel,
        grid=(n_batch,),
        in_specs=[new, new, new, cache, cache, _const_spec(bias_c.shape), _const_spec(bias_n.shape)],
        out_specs=new,
        out_shape=jax.ShapeDtypeStruct((n_rows, A_WIDTH), F32),
        compiler_params=_params(("parallel",)),
        name="attn_sample",
    )(q, k, v, k_cache, v_cache, bias_c, bias_n)


def _rel_bias_blocks(table, n_q, n_k, q_offset, banded):
    period = n_q + n_k - 1
    rel = q_offset - ((np.arange(period) + n_q - 1) % period - (n_q - 1))
    vec = jnp.take(table.astype(F32), jnp.asarray(np.clip(rel, -MAX_REL, MAX_REL) + MAX_REL), axis=1)
    n_heads = table.shape[0]
    bias = jnp.tile(vec, (1, n_q))[:, :n_q * (period - 1)].reshape(n_heads, n_q, period - 1)[:, :, :n_k]
    if banded:
        i = np.arange(n_q)[:, None]
        jj = np.arange(n_k)[None, :] - (i // CHUNK) * CHUNK
        inband = (jj >= 0) & (jj < A_WINDOW + CHUNK)
        bias = jnp.where(jnp.asarray(inband), bias, NEG_INF)
    return bias.reshape(n_heads // 2, 2 * n_q, n_k)


def _delta_kernel(qkvb_ref, gate_ref, beta_ref, a_ref, cst_ref, s0_ref, cw_ref, cb_ref, alog_ref, dtb_ref, nw_ref,
                  o_ref, cout_ref, sout_ref, xbuf, s_ref, ob_ref, *, t_valid, n_rows):
    j = pl.program_id(1)
    n_blk = n_rows // PAIR
    hd = B_HEAD_DIM
    low = lax.broadcasted_iota(jnp.int32, (PAIR, LANES), 1) < hd
    low_s = lax.broadcasted_iota(jnp.int32, (hd, LANES), 1) < hd
    zeros_s = jnp.zeros((hd, LANES), F32)

    @pl.when(j == 0)
    def _():
        xbuf[0:SUBLANES, :] = cst_ref[0]
        for h in range(B_HEADS):
            s0 = s0_ref[0, h]
            s_ref[h] = jnp.concatenate([s0, jnp.zeros_like(s0)] if h % 2 == 0 else [jnp.zeros_like(s0), s0], axis=1)

    xbuf[SUBLANES:SUBLANES + n_rows, :] = qkvb_ref[...]
    t_last = n_rows - PAIR + t_valid
    cout_ref[0] = xbuf[t_last:t_last + SUBLANES, :]

    hrow = lax.broadcasted_iota(jnp.int32, (B_WIDTH, B_WIDTH), 0)
    hcol = lax.broadcasted_iota(jnp.int32, (B_WIDTH, B_WIDTH), 1)
    head_ones = jnp.where(_blk(hrow, B_HEAD_DIM) == _blk(hcol, B_HEAD_DIM), 1.0, 0.0).astype(BF16)

    def l2n(t):
        return t * lax.rsqrt(_dot_f32_by_exact(t * t, head_ones) + RMS_EPS)

    row = lax.broadcasted_iota(jnp.int32, (PAIR, PAIR), 0)
    col = lax.broadcasted_iota(jnp.int32, (PAIR, PAIR), 1)
    same = _blk(row, CHUNK) == _blk(col, CHUNK)
    tril = same & (row >= col)
    strict = same & (row > col)
    first = row < CHUNK
    ltri = jnp.where(tril, 1.0, 0.0).astype(BF16)
    n_pairs = B_HEADS // 2
    chunks = range(PAIR // CHUNK)

    def pair_lanes(t, p):
        return jnp.where(low, t[:, (2 * p) * LANES:(2 * p + 1) * LANES], t[:, (2 * p + 1) * LANES:(2 * p + 2) * LANES])

    def own(h, t, other):
        return jnp.where(low, t, other) if h % 2 == 0 else jnp.where(low, other, t)

    prepared, solved = {}, {}

    def prepare(n):
        r0 = n * PAIR
        conv = cb_ref[...]
        for w in range(CONV_W):
            start = SUBLANES - (CONV_W - 1) + w + r0
            conv = conv + xbuf[start:start + PAIR, :] * cw_ref[w:w + 1, :]
        conv = _silu(conv)
        yield
        qn = l2n(conv[:, 0:B_WIDTH]) * (B_HEAD_DIM ** -0.5)
        yield
        kn = l2n(conv[:, B_WIDTH:2 * B_WIDTH])
        vv = conv[:, 2 * B_WIDTH:3 * B_WIDTH]
        yield
        beta = jax.nn.sigmoid(beta_ref[r0:r0 + PAIR, :])
        g = -jnp.exp(alog_ref[...]) * _softplus(a_ref[r0:r0 + PAIR, :] + dtb_ref[...])
        if t_valid < PAIR and n == n_blk - 1:
            def live(t):
                return jnp.where(lax.broadcasted_iota(jnp.int32, t.shape, 0) < t_valid, t, 0.0)
            kn, vv, beta, g = live(kn), live(vv), live(beta), live(g)
        yield
        gc = _dot_exact_by_f32(ltri, g)
        g_last = [gc[(c + 1) * CHUNK - 1:(c + 1) * CHUNK, :] for c in chunks]
        gl = jnp.concatenate([jnp.broadcast_to(t, (CHUNK, t.shape[1])) for t in g_last], axis=0)
        egc = jnp.exp(gc)
        erest = jnp.exp(gl - gc)
        d = {"egl": [jnp.exp(t) for t in g_last], "k": [], "q": [], "kb": [], "vb": [], "kbg_r": [], "qd_r": [],
             "kd": [], "decay": [], "decay_strict": []}
        yield
        for p in range(n_pairs):
            ls = slice(p * LANES, (p + 1) * LANES)
            k, q, v, b = kn[:, ls], qn[:, ls], vv[:, ls], beta[:, ls]
            eg = pair_lanes(egc, p)
            d["k"].append(k)
            d["q"].append(q)
            d["kb"].append(k * b)
            d["vb"].append(v * b)
            d["kbg_r"].append(pltpu.roll(k * b * eg, hd, axis=1))
            d["qd_r"].append(pltpu.roll(q * eg, hd, axis=1))
            d["kd"].append(k * pair_lanes(erest, p))
            yield
        for h in range(B_HEADS):
            gcol = gc[:, h * LANES:(h + 1) * LANES]
            e = jnp.exp(gcol - gcol.T)
            d["decay"].append(jnp.where(tril, e, 0.0))
            d["decay_strict"].append(jnp.where(strict, e, 0.0))
            yield
        prepared[n] = d
    heads = range(B_HEADS)

    def solve(n):
        d = prepared[n]
        a_low = [_bdot_nt(own(h, d["kb"][h // 2], 0.0), d["k"][h // 2]) * d["decay_strict"][h] for h in heads]
        yield
        base = 8
        diag = _blk(row, base) == _blk(col, base)
        eye = jnp.where(row == col, 1.0, 0.0)
        a0 = [jnp.where(diag, a, 0.0) for a in a_low]
        xs = [eye - a for a in a0]
        ps = [_bdot(a, a) for a in a0]
        yield
        xs = [x + _bdot(x, p) for x, p in zip(xs, ps)]
        ps = [_bdot(p, p) for p in ps]
        yield
        xs = [x + _bdot(x, p) for x, p in zip(xs, ps)]
        yield
        bs = base
        while bs < CHUNK:
            off = (_blk(row, 2 * bs) == _blk(col, 2 * bs)) & (_blk(row, bs) != _blk(col, bs))
            ts = [_bdot(jnp.where(off, a, 0.0), x) for a, x in zip(a_low, xs)]
            yield
            xs = [x - _bdot(x, t) for x, t in zip(xs, ts)]
            yield
            bs *= 2
        uws = [_bdot(x, own(h, d["vb"][h // 2], d["kbg_r"][h // 2])) for h, x in zip(heads, xs)]
        yield
        qus = []
        for h in heads:
            qk = _bdot_nt(own(h, d["q"][h // 2], 0.0), d["k"][h // 2]) * d["decay"][h]
            qus.append(_bdot(qk, uws[h]))
        yield
        kus = []
        for p in range(n_pairs):
            rhs = jnp.concatenate([jnp.where(first if c == 0 else ~first, uws[2 * p + e], 0.0)
                                   for e in range(2) for c in chunks], axis=1)
            kus.append(_bdot(d["kd"][p].T, rhs))
            yield
        solved[n] = (qus, kus)

    states = [s_ref[h] for h in heads]

    def advance(n):
        d = prepared[n]
        qus, kus = solved[n]
        for c in chunks:
            outs = []
            for h in heads:
                p, e = divmod(h, 2)
                ku = kus[p][e * hd:(e + 1) * hd, (2 * e + c) * LANES:(2 * e + c + 1) * LANES]
                qu = qus[h][c * CHUNK:(c + 1) * CHUNK, :]
                kq = jnp.concatenate([ku, d["qd_r"][p][c * CHUNK:(c + 1) * CHUNK] - qu], axis=0)
                s_ext = jnp.concatenate([zeros_s, states[h]] if e == 0 else [states[h], zeros_s], axis=0)
                r = _bdot(kq, s_ext)
                gt = d["egl"][c][:, h * LANES:(h + 1) * LANES]
                new = gt * states[h] - r[:hd] + ku
                states[h] = jnp.where(low_s, new, 0.0) if e == 0 else jnp.where(low_s, 0.0, new)
                outs.append(r[hd:] + qu)
            r0 = n * PAIR + c * CHUNK
            for p in range(n_pairs):
                ob_ref[r0:r0 + CHUNK, p * LANES:(p + 1) * LANES] = jnp.where(low_s, outs[2 * p], outs[2 * p + 1])
            yield
        rs = slice(n * PAIR, (n + 1) * PAIR)
        ob = ob_ref[rs, :]
        ms = _dot_f32_by_exact(ob * ob, head_ones) * (1.0 / hd)
        o_ref[rs, :] = ob * lax.rsqrt(ms + RMS_EPS) * nw_ref[...] * _silu(gate_ref[rs, :])
        yield

    def advance_all(blocks):
        for n in blocks:
            yield from advance(n)

    group = 2 if n_blk % 2 == 0 else 1
    n_groups = n_blk // group
    for slot in range(n_groups + 2):
        active = []
        if 0 <= slot - 2 < n_groups:
            active.append(advance_all(range((slot - 2) * group, (slot - 1) * group)))
        if 0 <= slot - 1 < n_groups:
            active += [solve(n) for n in range((slot - 1) * group, slot * group)]
        if slot < n_groups:
            active += [prepare(n) for n in range(slot * group, (slot + 1) * group)]
        while active:
            for gen in list(active):
                if next(gen, "done") == "done":
                    active.remove(gen)

    for h in heads:
        s_ref[h] = states[h]
    xbuf[0:SUBLANES, :] = xbuf[n_rows:n_rows + SUBLANES, :]

    @pl.when(j == pl.num_programs(1) - 1)
    def _():
        for h in range(B_HEADS):
            sout_ref[0, h] = s_ref[h][:, (h % 2) * hd:(h % 2 + 1) * hd]


def _delta(qkvb, gate, beta, a, conv_state, s0, lw, n_batch, t_valid, rows_per_step):
    n_rows = qkvb.shape[0]
    per_b = n_rows // n_batch // rows_per_step

    def rows(width):
        return pl.BlockSpec((rows_per_step, width), lambda b, j: (b * per_b + j, 0))

    cst = pl.BlockSpec((1, SUBLANES, 3 * B_WIDTH), lambda b, j: (b, 0, 0))
    st = pl.BlockSpec((1, B_HEADS, B_HEAD_DIM, B_HEAD_DIM), lambda b, j: (b, 0, 0, 0))
    consts = [lw["conv_w"], lw["conv_b"], lw["a_log"], lw["dt_bias"], lw["norm_w"]]
    return pl.pallas_call(
        functools.partial(_delta_kernel, t_valid=t_valid, n_rows=rows_per_step),
        grid=(n_batch, per_b),
        in_specs=[rows(3 * B_WIDTH), rows(B_WIDTH), rows(B_WIDTH), rows(B_HEADS * LANES), cst, st]
        + [_const_spec(c.shape) for c in consts],
        out_specs=[rows(B_WIDTH), cst, st],
        out_shape=[jax.ShapeDtypeStruct((n_rows, B_WIDTH), F32),
                   jax.ShapeDtypeStruct((n_batch, SUBLANES, 3 * B_WIDTH), F32),
                   jax.ShapeDtypeStruct((n_batch, B_HEADS, B_HEAD_DIM, B_HEAD_DIM), F32)],
        scratch_shapes=[pltpu.VMEM((SUBLANES + rows_per_step, 3 * B_WIDTH), F32),
                        pltpu.VMEM((B_HEADS, B_HEAD_DIM, LANES), F32),
                        pltpu.VMEM((rows_per_step, B_WIDTH), F32)],
        compiler_params=_params(("parallel", "arbitrary")),
        name="delta",
    )(qkvb, gate, beta, a, conv_state, s0, *consts)


def _s5_kernel(u_ref, h0r_ref, h0i_ref, lr_ref, li_ref, bre_ref, bim_ref, cre_ref, cim_ref, d_ref, gw_ref, gb_ref,
               o_ref, hro_ref, hio_ref, xr, xi, hs, *, t_block):
    tb = pl.program_id(1)

    @pl.when(tb == 0)
    def _():
        hs[0] = h0r_ref[0]
        hs[1] = h0i_ref[0]

    u = jnp.swapaxes(u_ref[0], 0, 1).reshape(t_block * S5_BATCH, C_WIDTH)
    ub = u.astype(BF16)
    xr[...] = jnp.dot(ub, bre_ref[...], preferred_element_type=F32)
    xi[...] = jnp.dot(ub, bim_ref[...], preferred_element_type=F32)
    lam_r = jnp.broadcast_to(lr_ref[...], (S5_BATCH, C_LANES))
    lam_i = jnp.broadcast_to(li_ref[...], (S5_BATCH, C_LANES))

    def step(t, carry):
        h_r, h_i = carry
        rows = pl.ds(pl.multiple_of(t * S5_BATCH, S5_BATCH), S5_BATCH)
        n_r = lam_r * h_r - lam_i * h_i + xr[rows, :]
        n_i = lam_r * h_i + lam_i * h_r + xi[rows, :]
        xr[rows, :] = n_r
        xi[rows, :] = n_i
        return n_r, n_i

    h_r, h_i = lax.fori_loop(0, t_block, step, (hs[0], hs[1]), unroll=4)
    hs[0] = h_r
    hs[1] = h_i
    hro_ref[0] = h_r
    hio_ref[0] = h_i
    y = (jnp.dot(xr[...].astype(BF16), cre_ref[...], preferred_element_type=F32)
         - jnp.dot(xi[...].astype(BF16), cim_ref[...], preferred_element_type=F32) + d_ref[...] * u)
    z = _gelu_tanh(y)
    out = z * jax.nn.sigmoid(jnp.dot(z.astype(BF16), gw_ref[...], preferred_element_type=F32) + gb_ref[...])
    o_ref[0] = jnp.swapaxes(out.reshape(t_block, S5_BATCH, C_WIDTH), 0, 1)


def _s5(u, h0_re, h0_im, lw, t_block):
    n_groups, _, t_len, _ = u.shape
    rows_blk = t_block * S5_BATCH
    n_tb = t_len // t_block
    u_spec = pl.BlockSpec((1, S5_BATCH, t_block, C_WIDTH), lambda g, t: (g, 0, t, 0))
    h_spec = pl.BlockSpec((1, S5_BATCH, C_LANES), lambda g, t: (g, 0, 0))
    consts = [lw["lam_re"], lw["lam_im"], lw["b_re"], lw["b_im"], lw["c_re"], lw["c_im"], lw["c_d"],
              lw["glu_w"], lw["glu_b"]]
    return pl.pallas_call(
        functools.partial(_s5_kernel, t_block=t_block),
        grid=(n_groups, n_tb),
        in_specs=[u_spec, h_spec, h_spec] + [_const_spec(c.shape) for c in consts],
        out_specs=[u_spec, h_spec, h_spec],
        out_shape=[jax.ShapeDtypeStruct(u.shape, F32),
                   jax.ShapeDtypeStruct(h0_re.shape, F32), jax.ShapeDtypeStruct(h0_im.shape, F32)],
        scratch_shapes=[pltpu.VMEM((rows_blk, C_LANES), F32), pltpu.VMEM((rows_blk, C_LANES), F32),
                        pltpu.VMEM((2, S5_BATCH, C_LANES), F32)],
        compiler_params=_params(("parallel", "arbitrary")),
        name="s5",
    )(u, h0_re, h0_im, *consts)


def _post_kernel(x_ref, a_ref, b_ref, c_ref, wo_ref, g1_ref, b1_ref, wu_ref, bu_ref, wd_ref, g2_ref, b2_ref, o_ref):
    mix = (jnp.dot(a_ref[...].astype(BF16), wo_ref[0:A_WIDTH, :], preferred_element_type=F32)
           + jnp.dot(b_ref[...].astype(BF16), wo_ref[A_WIDTH:A_WIDTH + B_WIDTH, :], preferred_element_type=F32)
           + jnp.dot(c_ref[...].astype(BF16), wo_ref[A_WIDTH + B_WIDTH:, :], preferred_element_type=F32))
    x1 = _layer_norm(DN_ALPHA * x_ref[...] + mix, g1_ref[...], b1_ref[...])
    x1b = x1.astype(BF16)
    acc = jnp.zeros(x1.shape, F32)
    for s in range(D_FF // FF_SLAB):
        cs = slice(s * FF_SLAB, (s + 1) * FF_SLAB)
        hid = jnp.dot(x1b, wu_ref[:, cs], preferred_element_type=F32) + bu_ref[:, cs]
        hid = jnp.square(jnp.maximum(hid, 0.0))
        acc = acc + jnp.dot(hid.astype(BF16), wd_ref[cs, :], preferred_element_type=F32)
    o_ref[...] = _layer_norm(DN_ALPHA * x1 + acc, g2_ref[...], b2_ref[...])


def _post(x2d, out_a, out_b, out_c, lw):
    n_rows = x2d.shape[0]
    tm = min(ROW_TILE, n_rows)

    def rows(width):
        return pl.BlockSpec((tm, width), lambda i: (i, 0))

    c_spec = rows(C_WIDTH)
    consts = [lw["w_out"], lw["ln1_g"], lw["ln1_b"], lw["w_up"], lw["b_up"], lw["w_down"], lw["ln2_g"], lw["ln2_b"]]
    return pl.pallas_call(
        _post_kernel,
        grid=(n_rows // tm,),
        in_specs=[rows(D_MODEL), rows(A_WIDTH), rows(B_WIDTH), c_spec] + [_const_spec(c.shape) for c in consts],
        out_specs=rows(D_MODEL),
        out_shape=jax.ShapeDtypeStruct((n_rows, D_MODEL), F32),
        compiler_params=_params(("parallel",)),
        name="post",
    )(x2d, out_a, out_b, out_c, *consts)


def _layer_weights(l, w_in, a_rel_bias, b_conv_w, b_conv_b, b_a_log, b_dt_bias, b_norm_w, c_a_re, c_a_im, c_log_dt,
                   c_b_re, c_b_im, c_c_re, c_c_im, c_d, c_glu_w, c_glu_b, w_out, ln1_g, ln1_b, w_up, b_up, w_down,
                   ln2_g, ln2_b, s_len, n_cache):
    wi = w_in[l]
    o_q, o_k, o_v = 0, A_WIDTH, 2 * A_WIDTH
    o_qkvb = 3 * A_WIDTH
    o_beta = o_qkvb + 3 * B_WIDTH
    o_a = o_beta + B_HEADS
    o_gate = o_a + B_HEADS
    o_uc = o_gate + B_WIDTH
    per_head = lambda cols: jnp.repeat(cols, B_HEAD_DIM, axis=-1)
    per_group = lambda cols: jnp.repeat(cols, LANES, axis=-1)
    w_proj = jnp.concatenate([
        wi[:, o_q:o_q + A_WIDTH], wi[:, o_k:o_k + A_WIDTH], wi[:, o_v:o_v + A_WIDTH],
        wi[:, o_qkvb:o_qkvb + 3 * B_WIDTH], wi[:, o_gate:o_gate + B_WIDTH], wi[:, o_uc:o_uc + C_WIDTH],
        per_head(wi[:, o_beta:o_beta + B_HEADS]), per_group(wi[:, o_a:o_a + B_HEADS])], axis=1).astype(BF16)

    table = a_rel_bias[l]
    bias_sample = _rel_bias_blocks(table, s_len, n_cache + s_len, n_cache, banded=False)

    a_re, a_im = c_a_re[l].astype(F32), c_a_im[l].astype(F32)
    dt = jnp.exp(c_log_dt[l].astype(F32))[:, None]
    mag = jnp.exp(dt * a_re)
    lam_re, lam_im = mag * jnp.cos(dt * a_im), mag * jnp.sin(dt * a_im)
    den = a_re * a_re + a_im * a_im
    coef_re = ((lam_re - 1.0) * a_re + lam_im * a_im) / den
    coef_im = (lam_im * a_re - (lam_re - 1.0) * a_im) / den
    bre, bim = c_b_re[l].astype(F32), c_b_im[l].astype(F32)
    bb_re = coef_re[..., None] * bre - coef_im[..., None] * bim
    bb_im = coef_re[..., None] * bim + coef_im[..., None] * bre
    eye = jnp.eye(C_GROUPS, dtype=F32)
    in_bd = lambda t: jnp.einsum("gph,gk->ghkp", t, eye).reshape(C_WIDTH, C_LANES).astype(BF16)
    out_bd = lambda t: jnp.einsum("ghp,gk->gpkh", t.astype(F32), eye).reshape(C_LANES, C_WIDTH).astype(BF16)
    row = lambda t: t.astype(F32).reshape(1, -1)

    return {
        "w_proj": w_proj,
        "bias_prompt": _rel_bias_blocks(table, ATT_Q_SUB, ATT_K_SUB, A_WINDOW, banded=True),
        "bias_cache": bias_sample[:, :, :n_cache], "bias_new": bias_sample[:, :, n_cache:],
        "conv_w": b_conv_w[l].astype(F32), "conv_b": row(b_conv_b[l]),
        "a_log": row(per_group(b_a_log[l])), "dt_bias": row(per_group(b_dt_bias[l])),
        "norm_w": row(jnp.tile(b_norm_w[l], B_HEADS)),
        "lam_re": row(lam_re), "lam_im": row(lam_im), "b_re": in_bd(bb_re), "b_im": in_bd(bb_im),
        "c_re": out_bd(c_c_re[l]), "c_im": out_bd(c_c_im[l]), "c_d": row(c_d[l]),
        "glu_w": c_glu_w[l].astype(BF16), "glu_b": row(c_glu_b[l]),
        "w_out": w_out[l].astype(BF16), "ln1_g": row(ln1_g[l]), "ln1_b": row(ln1_b[l]),
        "w_up": w_up[l].astype(BF16), "b_up": row(b_up[l]), "w_down": w_down[l].astype(BF16),
        "ln2_g": row(ln2_g[l]), "ln2_b": row(ln2_b[l]),
    }


def _pad_conv_state(conv_buf):
    return jnp.pad(conv_buf.astype(F32), ((0, 0), (SUBLANES - (CONV_W - 1), 0), (0, 0)))


def _prompt_layer(x2d, lw, n_batch):
    t_len = x2d.shape[0] // n_batch
    q, k, v, qkvb, gate, uc, beta, a, k_t, v_t = _proj(x2d, lw["w_proj"], n_batch, prompt=True)
    out_a = _attn_prompt(q, k, v, lw["bias_prompt"], n_batch)
    out_b, conv_o, s_new = _delta(
        qkvb, gate, beta, a, jnp.zeros((n_batch, SUBLANES, 3 * B_WIDTH), F32),
        jnp.zeros((n_batch, B_HEADS, B_HEAD_DIM, B_HEAD_DIM), F32), lw, n_batch, t_valid=PAIR, rows_per_step=DELTA_ROWS)
    zeros_h = jnp.zeros((1, S5_BATCH, C_LANES), F32)
    out_c, h_re, h_im = _s5(uc.reshape(1, n_batch, t_len, C_WIDTH), zeros_h, zeros_h, lw, S5_TIME_BLOCK)
    y = _post(x2d, out_a, out_b, out_c.reshape(n_batch * t_len, C_WIDTH), lw)
    heads = lambda t: t.reshape(n_batch, A_HEADS, A_HEAD_DIM, A_WINDOW).transpose(0, 3, 1, 2)
    state = lambda t: t.reshape(n_batch, C_GROUPS, C_STATE)
    return y, (heads(k_t), heads(v_t), conv_o[:, -(CONV_W - 1):], s_new, state(h_re), state(h_im))


def _sample_layer(x2d, lw, n_batch, layer, k_cache, v_cache, conv_buf, s0, h0_re, h0_im):
    s_len = x2d.shape[0] // n_batch
    n_groups = n_batch // S5_BATCH
    q, k, v, qkvb, gate, uc, beta, a = _proj(x2d, lw["w_proj"], n_batch, prompt=False)
    out_a = _attn_sample(q, k, v, k_cache, v_cache, layer, lw["bias_cache"], lw["bias_new"], n_batch)

    def pad_rows(t):
        t = t.reshape(n_batch, s_len, -1)
        return jnp.pad(t, ((0, 0), (0, PAIR - s_len), (0, 0))).reshape(n_batch * PAIR, -1)

    out_b, conv_o, s_new = _delta(pad_rows(qkvb), pad_rows(gate), pad_rows(beta), pad_rows(a),
                                  _pad_conv_state(conv_buf), s0.astype(F32), lw, n_batch, t_valid=s_len, rows_per_step=PAIR)
    out_b = out_b.reshape(n_batch, PAIR, B_WIDTH)[:, :s_len].reshape(n_batch * s_len, B_WIDTH)

    grp = lambda t: t.astype(F32).reshape(n_groups, S5_BATCH, C_LANES)
    out_c, h_re, h_im = _s5(uc.reshape(n_groups, S5_BATCH, s_len, C_WIDTH), grp(h0_re), grp(h0_im), lw, s_len)
    y = _post(x2d, out_a, out_b, out_c.reshape(n_batch * s_len, C_WIDTH), lw)
    heads = lambda t: t.reshape(n_batch, s_len, A_HEADS, A_HEAD_DIM)
    state = lambda t: t.reshape(n_batch, C_GROUPS, C_STATE)
    return y, (heads(k), heads(v), conv_o[:, -(CONV_W - 1):], s_new, state(h_re), state(h_im))


def kernel(x_prompt, x_sample, cache_a_k, cache_a_v, state_b_conv, state_b_ssm, state_c_re, state_c_im, w_in, a_rel_bias, b_conv_w, b_conv_b, b_a_log, b_dt_bias, b_norm_w, c_a_re, c_a_im, c_log_dt, c_b_re, c_b_im, c_c_re, c_c_im, c_d, c_glu_w, c_glu_b, w_out, ln1_g, ln1_b, w_up, b_up, w_down, ln2_g, ln2_b):
    n_p, t_p, _ = x_prompt.shape
    n_s, t_s, _ = x_sample.shape
    n_cache = cache_a_k.shape[2]
    yp = x_prompt.reshape(n_p * t_p, D_MODEL)
    ys = x_sample.reshape(n_s * t_s, D_MODEL)
    feat_major = lambda t: t.astype(F32).transpose(0, 1, 3, 4, 2).reshape(DEPTH, n_s, A_WIDTH, n_cache)
    k_cache, v_cache = feat_major(cache_a_k), feat_major(cache_a_v)
    p_out, s_out = [], []
    for l in range(DEPTH):
        lw = _layer_weights(l, w_in, a_rel_bias, b_conv_w, b_conv_b, b_a_log, b_dt_bias, b_norm_w, c_a_re, c_a_im,
                            c_log_dt, c_b_re, c_b_im, c_c_re, c_c_im, c_d, c_glu_w, c_glu_b, w_out, ln1_g, ln1_b,
                            w_up, b_up, w_down, ln2_g, ln2_b, t_s, n_cache)
        yp, st_p = _prompt_layer(yp, lw, n_p)
        ys, st_s = _sample_layer(ys, lw, n_s, l, k_cache, v_cache, state_b_conv[l], state_b_ssm[l],
                                 state_c_re[l], state_c_im[l])
        p_out.append(st_p)
        s_out.append(st_s)
    stack = lambda outs, i: jnp.stack([o[i] for o in outs])
    return (yp.reshape(n_p, t_p, D_MODEL), ys.reshape(n_s, t_s, D_MODEL),
            *[stack(p_out, i) for i in range(6)], *[stack(s_out, i) for i in range(6)])
```

```python
import functools

import jax
import jax.numpy as jnp
import numpy as np
from jax import lax
from jax.experimental import pallas as pl
from jax.experimental.pallas import tpu as pltpu

F32 = jnp.float32
BF16 = jnp.bfloat16

D_MODEL = 1024
DEPTH = 2
CHUNK = 64
A_HEADS = 8
A_HEAD_DIM = 64
A_WIDTH = A_HEADS * A_HEAD_DIM
A_WINDOW = 8 * CHUNK
MAX_REL = 256
B_HEADS = 4
B_HEAD_DIM = 64
B_WIDTH = B_HEADS * B_HEAD_DIM
CONV_W = 4
C_WIDTH = D_MODEL - A_WIDTH - B_WIDTH
C_GROUP = 16
C_GROUPS = C_WIDTH // C_GROUP
C_STATE = 64
C_LANES = C_GROUPS * C_STATE
D_FF = 4 * D_MODEL
DN_ALPHA = (2.0 * DEPTH) ** 0.25
LN_EPS = 1e-5
RMS_EPS = 1e-6
NEG_INF = -1e30

SUBLANES = 8
LANES = 128
VMEM_LIMIT_BYTES = 56 * 1024 * 1024

ROW_TILE = 512
ATT_Q_BLOCK = 512
ATT_Q_SUB = 2 * CHUNK
ATT_K_SUB = ATT_Q_SUB + A_WINDOW
PAIR = 2 * CHUNK
DELTA_ROWS = 8 * PAIR
S5_TIME_BLOCK = 128
S5_BATCH = SUBLANES
FF_SLAB = 1024

_SEG = {}
_off = 0
for _name, _w in (("q", A_WIDTH), ("k", A_WIDTH), ("v", A_WIDTH), ("qkvb", 3 * B_WIDTH), ("gate", B_WIDTH),
                  ("uc", C_WIDTH), ("ba", LANES)):
    _SEG[_name] = (_off, _off + _w)
    _off += _w
PROJ_WIDTH = _off


def _bdot(a, b):
    return jnp.dot(a.astype(BF16), b.astype(BF16), preferred_element_type=F32)


def _bdot_nt(a, b):
    return lax.dot_general(a.astype(BF16), b.astype(BF16), (((1,), (1,)), ((), ())), preferred_element_type=F32)


def _split3(a):
    hi = a.astype(BF16)
    r = a - hi.astype(F32)
    mid = r.astype(BF16)
    lo = (r - mid.astype(F32)).astype(BF16)
    return hi, mid, lo


def _dot_f32_by_exact(a, e):
    lhs = jnp.concatenate(_split3(a), axis=1)
    return jnp.dot(lhs, jnp.concatenate([e, e, e], axis=0), preferred_element_type=F32)


def _dot_exact_by_f32(e, a):
    rhs = jnp.concatenate(_split3(a), axis=0)
    return jnp.dot(jnp.concatenate([e, e, e], axis=1), rhs, preferred_element_type=F32)


def _blk(idx, size):
    return jnp.bitwise_and(idx, -size)


def _layer_norm(x, g, b):
    mu = jnp.mean(x, -1, keepdims=True)
    xc = x - mu
    var = jnp.mean(xc * xc, -1, keepdims=True)
    return xc * lax.rsqrt(var + LN_EPS) * g + b


def _silu(x):
    return x * jax.nn.sigmoid(x)


def _softplus(x):
    return jnp.maximum(x, 0.0) + jnp.log1p(jnp.exp(-jnp.abs(x)))


def _gelu_tanh(x):
    c = float(np.sqrt(2.0 / np.pi))
    return x * (0.5 * (1.0 + jnp.tanh(c * (x + 0.044715 * (x * x * x)))))


def _const_spec(shape):
    nd = len(shape)
    return pl.BlockSpec(shape, lambda *_: (0,) * nd, pipeline_mode=pl.Buffered(1))


def _params(semantics):
    return pltpu.CompilerParams(dimension_semantics=semantics, vmem_limit_bytes=VMEM_LIMIT_BYTES)


_PROJ_ORDER = ("q", "k", "v", "qkvb", "gate", "uc", "ba")


def _proj_kernel(x_ref, w_ref, *out_refs, tiles_per_stream):
    xb = x_ref[...].astype(BF16)
    vals = {}
    for name, ref in zip(_PROJ_ORDER, out_refs):
        lo, hi = _SEG[name]
        vals[name] = jnp.dot(xb, w_ref[:, lo:hi], preferred_element_type=F32)
        ref[...] = vals[name]
    if len(out_refs) > len(_PROJ_ORDER):
        kt_ref, vt_ref = out_refs[len(_PROJ_ORDER):]

        @pl.when(pl.program_id(0) % tiles_per_stream == tiles_per_stream - 1)
        def _():
            kt_ref[...] = vals["k"].T
            vt_ref[...] = vals["v"].T


def _proj(x2d, w, n_batch, prompt):
    n_rows = x2d.shape[0]
    tm = min(ROW_TILE, n_rows)
    n_steps = n_rows // tm
    per_b = n_rows // n_batch // tm if prompt else 1
    widths = {name: hi - lo for name, (lo, hi) in _SEG.items()}
    out_shape, out_specs = [], []
    for name in _PROJ_ORDER:
        wd = widths[name]
        out_shape.append(jax.ShapeDtypeStruct((n_rows, wd), F32))
        out_specs.append(pl.BlockSpec((tm, wd), lambda i: (i, 0)))
    if prompt:
        assert tm == A_WINDOW
        for _ in range(2):
            out_shape.append(jax.ShapeDtypeStruct((n_batch, A_WIDTH, A_WINDOW), F32))
            out_specs.append(pl.BlockSpec((None, A_WIDTH, A_WINDOW), lambda i: (i // per_b, 0, 0)))
    return pl.pallas_call(
        functools.partial(_proj_kernel, tiles_per_stream=per_b),
        grid=(n_steps,),
        in_specs=[pl.BlockSpec((tm, D_MODEL), lambda i: (i, 0)), _const_spec(w.shape)],
        out_specs=out_specs,
        out_shape=out_shape,
        compiler_params=_params(("arbitrary",)),
        name="proj",
    )(x2d, w)


def _softmax_pv(q2s, key_segs, val_segs, bias_segs, valid_segs, feature_major=None):
    feature_major = feature_major or [False] * len(valid_segs)
    nn = (((1,), (0,)), ((), ()))
    nt = (((1,), (1,)), ((), ()))
    m_rows = q2s[0].shape[0]
    low = lax.broadcasted_iota(jnp.int32, q2s[0].shape, 1) < A_HEAD_DIM
    qs = [jnp.concatenate([jnp.where(low, q2, 0.0), jnp.where(low, 0.0, q2)], axis=0).astype(BF16) for q2 in q2s]
    scores = []
    for q, ksegs, biases in zip(qs, key_segs, bias_segs):
        sc = []
        for kseg, bias, valid, fm in zip(ksegs, biases, valid_segs, feature_major):
            s = lax.dot_general(q, kseg, nn if fm else nt, preferred_element_type=F32) + bias
            if valid is not None:
                s = jnp.where(valid, s, NEG_INF)
            sc.append(s)
        scores.append(sc)
    tops = [functools.reduce(jnp.maximum, [jnp.max(s, -1, keepdims=True) for s in sc]) for sc in scores]
    probs = [[jnp.exp(s - m) for s in sc] for sc, m in zip(scores, tops)]
    denoms = [functools.reduce(jnp.add, [jnp.sum(p, -1, keepdims=True) for p in pr]) for pr in probs]
    pvs = [functools.reduce(jnp.add, [lax.dot_general(p.astype(BF16), vseg, nt if fm else nn,
                                                      preferred_element_type=F32)
                                      for p, vseg, fm in zip(pr, vsegs, feature_major)])
           for pr, vsegs in zip(probs, val_segs)]
    outs = []
    for pv, denom in zip(pvs, denoms):
        pvn = pv * (1.0 / denom)
        outs.append(jnp.where(low, pvn[:m_rows], pvn[m_rows:]))
    return outs


def _attn_prompt_kernel(q_ref, kp_ref, kc_ref, vp_ref, vc_ref, bias_ref, o_ref, kk_ref, vv_ref):
    jb = pl.program_id(1)
    kk_ref[0:ATT_Q_BLOCK, :] = kp_ref[...].astype(BF16)
    kk_ref[ATT_Q_BLOCK:, :] = kc_ref[...].astype(BF16)
    vv_ref[0:ATT_Q_BLOCK, :] = vp_ref[...].astype(BF16)
    vv_ref[ATT_Q_BLOCK:, :] = vc_ref[...].astype(BF16)
    scale = A_HEAD_DIM ** -0.5
    for sub in range(ATT_Q_BLOCK // ATT_Q_SUB):
        r0 = sub * ATT_Q_SUB
        pos = lax.broadcasted_iota(jnp.int32, (1, ATT_K_SUB), 1) + ((jb - 1) * ATT_Q_BLOCK + r0)
        valid = pos >= 0
        pairs = range(A_HEADS // 2)
        cols = [slice(hp * LANES, (hp + 1) * LANES) for hp in pairs]
        outs = _softmax_pv([q_ref[r0:r0 + ATT_Q_SUB, c] * scale for c in cols],
                           [[kk_ref[r0:r0 + ATT_K_SUB, c]] for c in cols],
                           [[vv_ref[r0:r0 + ATT_K_SUB, c]] for c in cols],
                           [[bias_ref[hp]] for hp in pairs], [valid])
        for c, out in zip(cols, outs):
            o_ref[r0:r0 + ATT_Q_SUB, c] = out


def _attn_prompt(q, k, v, bias, n_batch):
    n_rows = q.shape[0]
    per_b = n_rows // n_batch // ATT_Q_BLOCK
    blk = (ATT_Q_BLOCK, A_WIDTH)
    cur = pl.BlockSpec(blk, lambda b, j: (b * per_b + j, 0))
    prev = pl.BlockSpec(blk, lambda b, j: (b * per_b + jnp.maximum(j - 1, 0), 0))
    return pl.pallas_call(
        _attn_prompt_kernel,
        grid=(n_batch, per_b),
        in_specs=[cur, prev, cur, prev, cur, _const_spec(bias.shape)],
        out_specs=cur,
        out_shape=jax.ShapeDtypeStruct((n_rows, A_WIDTH), F32),
        scratch_shapes=[pltpu.VMEM((2 * ATT_Q_BLOCK, A_WIDTH), BF16), pltpu.VMEM((2 * ATT_Q_BLOCK, A_WIDTH), BF16)],
        compiler_params=_params(("parallel", "arbitrary")),
        name="attn_prompt",
    )(q, k, k, v, v, bias)


def _attn_sample_kernel(q_ref, kn_ref, vn_ref, kc_ref, vc_ref, bias_c_ref, bias_n_ref, o_ref):
    scale = A_HEAD_DIM ** -0.5
    pairs = range(A_HEADS // 2)
    cols = [slice(hp * LANES, (hp + 1) * LANES) for hp in pairs]
    outs = _softmax_pv([q_ref[:, c] * scale for c in cols],
                       [[kc_ref[c, :].astype(BF16), kn_ref[:, c].astype(BF16)] for c in cols],
                       [[vc_ref[c, :].astype(BF16), vn_ref[:, c].astype(BF16)] for c in cols],
                       [[bias_c_ref[hp], bias_n_ref[hp]] for hp in pairs], [None, None], [True, False])
    for c, out in zip(cols, outs):
        o_ref[:, c] = out


def _attn_sample(q, k, v, k_cache, v_cache, layer, bias_c, bias_n, n_batch):
    n_rows = q.shape[0]
    s_len = n_rows // n_batch
    n_cache = k_cache.shape[3]
    new = pl.BlockSpec((s_len, A_WIDTH), lambda b: (b, 0))
    cache = pl.BlockSpec((None, None, A_WIDTH, n_cache), lambda b: (layer, b, 0, 0))
    return pl.pallas_call(
        _attn_sample_kernel,
        grid=(n_batch,),
        in_specs=[new, new, new, cache, cache, _const_spec(bias_c.shape), _const_spec(bias_n.shape)],
        out_specs=new,
        out_shape=jax.ShapeDtypeStruct((n_rows, A_WIDTH), F32),
        compiler_params=_params(("parallel",)),
        name="attn_sample",
    )(q, k, v, k_cache, v_cache, bias_c, bias_n)


def _rel_bias_blocks(table, n_q, n_k, q_offset, banded):
    period = n_q + n_k - 1
    rel = q_offset - ((np.arange(period) + n_q - 1) % period - (n_q - 1))
    vec = jnp.take(table.astype(F32), jnp.asarray(np.clip(rel, -MAX_REL, MAX_REL) + MAX_REL), axis=1)
    n_heads = table.shape[0]
    bias = jnp.tile(vec, (1, n_q))[:, :n_q * (period - 1)].reshape(n_heads, n_q, period - 1)[:, :, :n_k]
    if banded:
        i = np.arange(n_q)[:, None]
        jj = np.arange(n_k)[None, :] - (i // CHUNK) * CHUNK
        inband = (jj >= 0) & (jj < A_WINDOW + CHUNK)
        bias = jnp.where(jnp.asarray(inband), bias, NEG_INF)
    return bias.reshape(n_heads // 2, 2 * n_q, n_k)


def _delta_kernel(qkvb_ref, gate_ref, ba_ref, cst_ref, s0_ref, cw_ref, cb_ref, alog_ref, dtb_ref, nw_ref,
                  o_ref, cout_ref, sout_ref, xbuf, s_ref, ob_ref, *, t_valid, n_rows):
    j = pl.program_id(1)
    n_blk = n_rows // PAIR
    hd = B_HEAD_DIM
    low = lax.broadcasted_iota(jnp.int32, (PAIR, LANES), 1) < hd
    low_s = lax.broadcasted_iota(jnp.int32, (hd, LANES), 1) < hd
    zeros_s = jnp.zeros((hd, LANES), F32)

    @pl.when(j == 0)
    def _():
        xbuf[0:SUBLANES, :] = cst_ref[0]
        for h in range(B_HEADS):
            s0 = s0_ref[0, h]
            s_ref[h] = jnp.concatenate([s0, jnp.zeros_like(s0)] if h % 2 == 0 else [jnp.zeros_like(s0), s0], axis=1)

    xbuf[SUBLANES:SUBLANES + n_rows, :] = qkvb_ref[...]
    t_last = n_rows - PAIR + t_valid
    cout_ref[0] = xbuf[t_last:t_last + SUBLANES, :]

    hrow = lax.broadcasted_iota(jnp.int32, (B_WIDTH, B_WIDTH), 0)
    hcol = lax.broadcasted_iota(jnp.int32, (B_WIDTH, B_WIDTH), 1)
    head_ones = jnp.where(_blk(hrow, B_HEAD_DIM) == _blk(hcol, B_HEAD_DIM), 1.0, 0.0).astype(BF16)

    def l2n(t):
        return t * lax.rsqrt(_dot_f32_by_exact(t * t, head_ones) + RMS_EPS)

    row = lax.broadcasted_iota(jnp.int32, (PAIR, PAIR), 0)
    col = lax.broadcasted_iota(jnp.int32, (PAIR, PAIR), 1)
    same = _blk(row, CHUNK) == _blk(col, CHUNK)
    tril = same & (row >= col)
    strict = same & (row > col)
    first = row < CHUNK
    ltri = jnp.where(tril, 1.0, 0.0).astype(BF16)
    n_pairs = B_HEADS // 2
    chunks = range(PAIR // CHUNK)

    def pair_lanes(t, p):
        return jnp.where(low, t[:, (2 * p) * LANES:(2 * p + 1) * LANES], t[:, (2 * p + 1) * LANES:(2 * p + 2) * LANES])

    def own(h, t, other):
        return jnp.where(low, t, other) if h % 2 == 0 else jnp.where(low, other, t)

    prepared, solved = {}, {}

    def prepare(n):
        r0 = n * PAIR
        conv = cb_ref[...]
        for w in range(CONV_W):
            start = SUBLANES - (CONV_W - 1) + w + r0
            conv = conv + xbuf[start:start + PAIR, :] * cw_ref[w:w + 1, :]
        conv = _silu(conv)
        yield
        qn = l2n(conv[:, 0:B_WIDTH]) * (B_HEAD_DIM ** -0.5)
        yield
        kn = l2n(conv[:, B_WIDTH:2 * B_WIDTH])
        vv = conv[:, 2 * B_WIDTH:3 * B_WIDTH]
        yield
        ba = ba_ref[r0:r0 + PAIR, :]
        spread = lambda c: jnp.broadcast_to(ba[:, c:c + 1], (PAIR, LANES))
        beta = jax.nn.sigmoid(jnp.concatenate(
            [jnp.where(low, spread(2 * p), spread(2 * p + 1)) for p in range(n_pairs)], axis=1))
        a_in = jnp.concatenate([spread(B_HEADS + h) for h in range(B_HEADS)], axis=1)
        g = -jnp.exp(alog_ref[...]) * _softplus(a_in + dtb_ref[...])
        if t_valid < PAIR and n == n_blk - 1:
            def live(t):
                return jnp.where(lax.broadcasted_iota(jnp.int32, t.shape, 0) < t_valid, t, 0.0)
            kn, vv, beta, g = live(kn), live(vv), live(beta), live(g)
        yield
        gc = _dot_exact_by_f32(ltri, g)
        g_last = [gc[(c + 1) * CHUNK - 1:(c + 1) * CHUNK, :] for c in chunks]
        gl = jnp.concatenate([jnp.broadcast_to(t, (CHUNK, t.shape[1])) for t in g_last], axis=0)
        egc = jnp.exp(gc)
        erest = jnp.exp(gl - gc)
        d = {"egl": [jnp.exp(t) for t in g_last], "k": [], "q": [], "kb": [], "vb": [], "kbg_r": [], "qd_r": [],
             "kd": [], "decay": [], "decay_strict": []}
        yield
        for p in range(n_pairs):
            ls = slice(p * LANES, (p + 1) * LANES)
            k, q, v, b = kn[:, ls], qn[:, ls], vv[:, ls], beta[:, ls]
            eg = pair_lanes(egc, p)
            d["k"].append(k)
            d["q"].append(q)
            d["kb"].append(k * b)
            d["vb"].append(v * b)
            d["kbg_r"].append(pltpu.roll(k * b * eg, hd, axis=1))
            d["qd_r"].append(pltpu.roll(q * eg, hd, axis=1))
            d["kd"].append(k * pair_lanes(erest, p))
            yield
        for h in range(B_HEADS):
            gcol = gc[:, h * LANES:(h + 1) * LANES]
            e = jnp.exp(gcol - gcol.T)
            d["decay"].append(jnp.where(tril, e, 0.0))
            d["decay_strict"].append(jnp.where(strict, e, 0.0))
            yield
        prepared[n] = d
    heads = range(B_HEADS)

    def solve(n):
        d = prepared[n]
        a_low = [_bdot_nt(own(h, d["kb"][h // 2], 0.0), d["k"][h // 2]) * d["decay_strict"][h] for h in heads]
        yield
        base = 8
        diag = _blk(row, base) == _blk(col, base)
        eye = jnp.where(row == col, 1.0, 0.0)
        a0 = [jnp.where(diag, a, 0.0) for a in a_low]
        xs = [eye - a for a in a0]
        ps = [_bdot(a, a) for a in a0]
        yield
        xs = [x + _bdot(x, p) for x, p in zip(xs, ps)]
        ps = [_bdot(p, p) for p in ps]
        yield
        xs = [x + _bdot(x, p) for x, p in zip(xs, ps)]
        yield
        bs = base
        while bs < CHUNK:
            off = (_blk(row, 2 * bs) == _blk(col, 2 * bs)) & (_blk(row, bs) != _blk(col, bs))
            ts = [_bdot(jnp.where(off, a, 0.0), x) for a, x in zip(a_low, xs)]
            yield
            xs = [x - _bdot(x, t) for x, t in zip(xs, ts)]
            yield
            bs *= 2
        uws = [_bdot(x, own(h, d["vb"][h // 2], d["kbg_r"][h // 2])) for h, x in zip(heads, xs)]
        yield
        qus = []
        for h in heads:
            qk = _bdot_nt(own(h, d["q"][h // 2], 0.0), d["k"][h // 2]) * d["decay"][h]
            qus.append(_bdot(qk, uws[h]))
        yield
        kus = []
        for p in range(n_pairs):
            rhs = jnp.concatenate([jnp.where(first if c == 0 else ~first, uws[2 * p + e], 0.0)
                                   for e in range(2) for c in chunks], axis=1)
            kus.append(_bdot(d["kd"][p].T, rhs))
            yield
        solved[n] = (qus, kus)

    states = [s_ref[h] for h in heads]

    def advance(n):
        d = prepared[n]
        qus, kus = solved[n]
        for c in chunks:
            outs = []
            for h in heads:
                p, e = divmod(h, 2)
                ku = kus[p][e * hd:(e + 1) * hd, (2 * e + c) * LANES:(2 * e + c + 1) * LANES]
                qu = qus[h][c * CHUNK:(c + 1) * CHUNK, :]
                kq = jnp.concatenate([ku, d["qd_r"][p][c * CHUNK:(c + 1) * CHUNK] - qu], axis=0)
                s_ext = jnp.concatenate([zeros_s, states[h]] if e == 0 else [states[h], zeros_s], axis=0)
                r = _bdot(kq, s_ext)
                gt = d["egl"][c][:, h * LANES:(h + 1) * LANES]
                new = gt * states[h] - r[:hd] + ku
                states[h] = jnp.where(low_s, new, 0.0) if e == 0 else jnp.where(low_s, 0.0, new)
                outs.append(r[hd:] + qu)
            r0 = n * PAIR + c * CHUNK
            for p in range(n_pairs):
                ob_ref[r0:r0 + CHUNK, p * LANES:(p + 1) * LANES] = jnp.where(low_s, outs[2 * p], outs[2 * p + 1])
            yield
        rs = slice(n * PAIR, (n + 1) * PAIR)
        ob = ob_ref[rs, :]
        ms = _dot_f32_by_exact(ob * ob, head_ones) * (1.0 / hd)
        o_ref[rs, :] = ob * lax.rsqrt(ms + RMS_EPS) * nw_ref[...] * _silu(gate_ref[rs, :])
        yield

    def advance_all(blocks):
        for n in blocks:
            yield from advance(n)

    group = 2 if n_blk % 2 == 0 else 1
    n_groups = n_blk // group
    for slot in range(n_groups + 2):
        active = []
        if 0 <= slot - 2 < n_groups:
            active.append(advance_all(range((slot - 2) * group, (slot - 1) * group)))
        if 0 <= slot - 1 < n_groups:
            active += [solve(n) for n in range((slot - 1) * group, slot * group)]
        if slot < n_groups:
            active += [prepare(n) for n in range(slot * group, (slot + 1) * group)]
        while active:
            for gen in list(active):
                if next(gen, "done") == "done":
                    active.remove(gen)

    for h in heads:
        s_ref[h] = states[h]
    xbuf[0:SUBLANES, :] = xbuf[n_rows:n_rows + SUBLANES, :]

    @pl.when(j == pl.num_programs(1) - 1)
    def _():
        for h in range(B_HEADS):
            sout_ref[0, h] = s_ref[h][:, (h % 2) * hd:(h % 2 + 1) * hd]


def _delta(qkvb, gate, ba, conv_state, s0, lw, n_batch, t_valid, rows_per_step):
    n_rows = qkvb.shape[0]
    per_b = n_rows // n_batch // rows_per_step

    def rows(width):
        return pl.BlockSpec((rows_per_step, width), lambda b, j: (b * per_b + j, 0))

    cst = pl.BlockSpec((1, SUBLANES, 3 * B_WIDTH), lambda b, j: (b, 0, 0))
    st = pl.BlockSpec((1, B_HEADS, B_HEAD_DIM, B_HEAD_DIM), lambda b, j: (b, 0, 0, 0))
    consts = [lw["conv_w"], lw["conv_b"], lw["a_log"], lw["dt_bias"], lw["norm_w"]]
    return pl.pallas_call(
        functools.partial(_delta_kernel, t_valid=t_valid, n_rows=rows_per_step),
        grid=(n_batch, per_b),
        in_specs=[rows(3 * B_WIDTH), rows(B_WIDTH), rows(LANES), cst, st]
        + [_const_spec(c.shape) for c in consts],
        out_specs=[rows(B_WIDTH), cst, st],
        out_shape=[jax.ShapeDtypeStruct((n_rows, B_WIDTH), F32),
                   jax.ShapeDtypeStruct((n_batch, SUBLANES, 3 * B_WIDTH), F32),
                   jax.ShapeDtypeStruct((n_batch, B_HEADS, B_HEAD_DIM, B_HEAD_DIM), F32)],
        scratch_shapes=[pltpu.VMEM((SUBLANES + rows_per_step, 3 * B_WIDTH), F32),
                        pltpu.VMEM((B_HEADS, B_HEAD_DIM, LANES), F32),
                        pltpu.VMEM((rows_per_step, B_WIDTH), F32)],
        compiler_params=_params(("parallel", "arbitrary")),
        name="delta",
    )(qkvb, gate, ba, conv_state, s0, *consts)


def _s5_kernel(u_ref, h0r_ref, h0i_ref, lr_ref, li_ref, bre_ref, bim_ref, cre_ref, cim_ref, d_ref, gw_ref, gb_ref,
               o_ref, hro_ref, hio_ref, xr, xi, hs, *, t_block):
    tb = pl.program_id(1)

    @pl.when(tb == 0)
    def _():
        hs[0] = h0r_ref[0]
        hs[1] = h0i_ref[0]

    u = jnp.swapaxes(u_ref[0], 0, 1).reshape(t_block * S5_BATCH, C_WIDTH)
    ub = u.astype(BF16)
    xr[...] = jnp.dot(ub, bre_ref[...], preferred_element_type=F32)
    xi[...] = jnp.dot(ub, bim_ref[...], preferred_element_type=F32)
    lam_r = jnp.broadcast_to(lr_ref[...], (S5_BATCH, C_LANES))
    lam_i = jnp.broadcast_to(li_ref[...], (S5_BATCH, C_LANES))

    def step(t, carry):
        h_r, h_i = carry
        rows = pl.ds(pl.multiple_of(t * S5_BATCH, S5_BATCH), S5_BATCH)
        n_r = lam_r * h_r - lam_i * h_i + xr[rows, :]
        n_i = lam_r * h_i + lam_i * h_r + xi[rows, :]
        xr[rows, :] = n_r
        xi[rows, :] = n_i
        return n_r, n_i

    h_r, h_i = lax.fori_loop(0, t_block, step, (hs[0], hs[1]), unroll=4)
    hs[0] = h_r
    hs[1] = h_i
    hro_ref[0] = h_r
    hio_ref[0] = h_i
    y = (jnp.dot(xr[...].astype(BF16), cre_ref[...], preferred_element_type=F32)
         - jnp.dot(xi[...].astype(BF16), cim_ref[...], preferred_element_type=F32) + d_ref[...] * u)
    z = _gelu_tanh(y)
    out = z * jax.nn.sigmoid(jnp.dot(z.astype(BF16), gw_ref[...], preferred_element_type=F32) + gb_ref[...])
    o_ref[0] = jnp.swapaxes(out.reshape(t_block, S5_BATCH, C_WIDTH), 0, 1)


def _s5(u, h0_re, h0_im, lw, t_block):
    n_groups, _, t_len, _ = u.shape
    rows_blk = t_block * S5_BATCH
    n_tb = t_len // t_block
    u_spec = pl.BlockSpec((1, S5_BATCH, t_block, C_WIDTH), lambda g, t: (g, 0, t, 0))
    h_spec = pl.BlockSpec((1, S5_BATCH, C_LANES), lambda g, t: (g, 0, 0))
    consts = [lw["lam_re"], lw["lam_im"], lw["b_re"], lw["b_im"], lw["c_re"], lw["c_im"], lw["c_d"],
              lw["glu_w"], lw["glu_b"]]
    return pl.pallas_call(
        functools.partial(_s5_kernel, t_block=t_block),
        grid=(n_groups, n_tb),
        in_specs=[u_spec, h_spec, h_spec] + [_const_spec(c.shape) for c in consts],
        out_specs=[u_spec, h_spec, h_spec],
        out_shape=[jax.ShapeDtypeStruct(u.shape, F32),
                   jax.ShapeDtypeStruct(h0_re.shape, F32), jax.ShapeDtypeStruct(h0_im.shape, F32)],
        scratch_shapes=[pltpu.VMEM((rows_blk, C_LANES), F32), pltpu.VMEM((rows_blk, C_LANES), F32),
                        pltpu.VMEM((2, S5_BATCH, C_LANES), F32)],
        compiler_params=_params(("parallel", "arbitrary")),
        name="s5",
    )(u, h0_re, h0_im, *consts)


def _post_kernel(x_ref, a_ref, b_ref, c_ref, wo_ref, g1_ref, b1_ref, wu_ref, bu_ref, wd_ref, g2_ref, b2_ref, o_ref):
    mix = (jnp.dot(a_ref[...].astype(BF16), wo_ref[0:A_WIDTH, :], preferred_element_type=F32)
           + jnp.dot(b_ref[...].astype(BF16), wo_ref[A_WIDTH:A_WIDTH + B_WIDTH, :], preferred_element_type=F32)
           + jnp.dot(c_ref[...].astype(BF16), wo_ref[A_WIDTH + B_WIDTH:, :], preferred_element_type=F32))
    x1 = _layer_norm(DN_ALPHA * x_ref[...] + mix, g1_ref[...], b1_ref[...])
    x1b = x1.astype(BF16)
    acc = jnp.zeros(x1.shape, F32)
    for s in range(D_FF // FF_SLAB):
        cs = slice(s * FF_SLAB, (s + 1) * FF_SLAB)
        hid = jnp.dot(x1b, wu_ref[:, cs], preferred_element_type=F32) + bu_ref[:, cs]
        hid = jnp.square(jnp.maximum(hid, 0.0))
        acc = acc + jnp.dot(hid.astype(BF16), wd_ref[cs, :], preferred_element_type=F32)
    o_ref[...] = _layer_norm(DN_ALPHA * x1 + acc, g2_ref[...], b2_ref[...])


def _post(x2d, out_a, out_b, out_c, lw):
    n_rows = x2d.shape[0]
    tm = min(ROW_TILE, n_rows)

    def rows(width):
        return pl.BlockSpec((tm, width), lambda i: (i, 0))

    c_spec = rows(C_WIDTH)
    consts = [lw["w_out"], lw["ln1_g"], lw["ln1_b"], lw["w_up"], lw["b_up"], lw["w_down"], lw["ln2_g"], lw["ln2_b"]]
    return pl.pallas_call(
        _post_kernel,
        grid=(n_rows // tm,),
        in_specs=[rows(D_MODEL), rows(A_WIDTH), rows(B_WIDTH), c_spec] + [_const_spec(c.shape) for c in consts],
        out_specs=rows(D_MODEL),
        out_shape=jax.ShapeDtypeStruct((n_rows, D_MODEL), F32),
        compiler_params=_params(("parallel",)),
        name="post",
    )(x2d, out_a, out_b, out_c, *consts)


def _layer_weights(l, w_in, a_rel_bias, b_conv_w, b_conv_b, b_a_log, b_dt_bias, b_norm_w, c_a_re, c_a_im, c_log_dt,
                   c_b_re, c_b_im, c_c_re, c_c_im, c_d, c_glu_w, c_glu_b, w_out, ln1_g, ln1_b, w_up, b_up, w_down,
                   ln2_g, ln2_b, s_len, n_cache):
    wi = w_in[l]
    o_q, o_k, o_v = 0, A_WIDTH, 2 * A_WIDTH
    o_qkvb = 3 * A_WIDTH
    o_beta = o_qkvb + 3 * B_WIDTH
    o_a = o_beta + B_HEADS
    o_gate = o_a + B_HEADS
    o_uc = o_gate + B_WIDTH
    per_group = lambda cols: jnp.repeat(cols, LANES, axis=-1)
    assert o_a == o_beta + B_HEADS
    w_proj = jnp.concatenate([
        wi[:, o_q:o_q + A_WIDTH], wi[:, o_k:o_k + A_WIDTH], wi[:, o_v:o_v + A_WIDTH],
        wi[:, o_qkvb:o_qkvb + 3 * B_WIDTH], wi[:, o_gate:o_gate + B_WIDTH], wi[:, o_uc:o_uc + C_WIDTH],
        wi[:, o_beta:o_beta + 2 * B_HEADS], jnp.zeros((D_MODEL, LANES - 2 * B_HEADS), wi.dtype)], axis=1).astype(BF16)

    table = a_rel_bias[l]
    bias_sample = _rel_bias_blocks(table, s_len, n_cache + s_len, n_cache, banded=False)

    a_re, a_im = c_a_re[l].astype(F32), c_a_im[l].astype(F32)
    dt = jnp.exp(c_log_dt[l].astype(F32))[:, None]
    mag = jnp.exp(dt * a_re)
    lam_re, lam_im = mag * jnp.cos(dt * a_im), mag * jnp.sin(dt * a_im)
    den = a_re * a_re + a_im * a_im
    coef_re = ((lam_re - 1.0) * a_re + lam_im * a_im) / den
    coef_im = (lam_im * a_re - (lam_re - 1.0) * a_im) / den
    bre, bim = c_b_re[l].astype(F32), c_b_im[l].astype(F32)
    bb_re = coef_re[..., None] * bre - coef_im[..., None] * bim
    bb_im = coef_re[..., None] * bim + coef_im[..., None] * bre
    eye = jnp.eye(C_GROUPS, dtype=F32)
    in_bd = lambda t: jnp.einsum("gph,gk->ghkp", t, eye).reshape(C_WIDTH, C_LANES).astype(BF16)
    out_bd = lambda t: jnp.einsum("ghp,gk->gpkh", t.astype(F32), eye).reshape(C_LANES, C_WIDTH).astype(BF16)
    row = lambda t: t.astype(F32).reshape(1, -1)

    return {
        "w_proj": w_proj,
        "bias_prompt": _rel_bias_blocks(table, ATT_Q_SUB, ATT_K_SUB, A_WINDOW, banded=True),
        "bias_cache": bias_sample[:, :, :n_cache], "bias_new": bias_sample[:, :, n_cache:],
        "conv_w": b_conv_w[l].astype(F32), "conv_b": row(b_conv_b[l]),
        "a_log": row(per_group(b_a_log[l])), "dt_bias": row(per_group(b_dt_bias[l])),
        "norm_w": row(jnp.tile(b_norm_w[l], B_HEADS)),
        "lam_re": row(lam_re), "lam_im": row(lam_im), "b_re": in_bd(bb_re), "b_im": in_bd(bb_im),
        "c_re": out_bd(c_c_re[l]), "c_im": out_bd(c_c_im[l]), "c_d": row(c_d[l]),
        "glu_w": c_glu_w[l].astype(BF16), "glu_b": row(c_glu_b[l]),
        "w_out": w_out[l].astype(BF16), "ln1_g": row(ln1_g[l]), "ln1_b": row(ln1_b[l]),
        "w_up": w_up[l].astype(BF16), "b_up": row(b_up[l]), "w_down": w_down[l].astype(BF16),
        "ln2_g": row(ln2_g[l]), "ln2_b": row(ln2_b[l]),
    }


def _pad_conv_state(conv_buf):
    return jnp.pad(conv_buf.astype(F32), ((0, 0), (SUBLANES - (CONV_W - 1), 0), (0, 0)))


def _prompt_layer(x2d, lw, n_batch):
    t_len = x2d.shape[0] // n_batch
    q, k, v, qkvb, gate, uc, ba, k_t, v_t = _proj(x2d, lw["w_proj"], n_batch, prompt=True)
    out_a = _attn_prompt(q, k, v, lw["bias_prompt"], n_batch)
    out_b, conv_o, s_new = _delta(
        qkvb, gate, ba, jnp.zeros((n_batch, SUBLANES, 3 * B_WIDTH), F32),
        jnp.zeros((n_batch, B_HEADS, B_HEAD_DIM, B_HEAD_DIM), F32), lw, n_batch, t_valid=PAIR, rows_per_step=DELTA_ROWS)
    zeros_h = jnp.zeros((1, S5_BATCH, C_LANES), F32)
    out_c, h_re, h_im = _s5(uc.reshape(1, n_batch, t_len, C_WIDTH), zeros_h, zeros_h, lw, S5_TIME_BLOCK)
    y = _post(x2d, out_a, out_b, out_c.reshape(n_batch * t_len, C_WIDTH), lw)
    heads = lambda t: t.reshape(n_batch, A_HEADS, A_HEAD_DIM, A_WINDOW).transpose(0, 3, 1, 2)
    state = lambda t: t.reshape(n_batch, C_GROUPS, C_STATE)
    return y, (heads(k_t), heads(v_t), conv_o[:, -(CONV_W - 1):], s_new, state(h_re), state(h_im))


def _sample_layer(x2d, lw, n_batch, layer, k_cache, v_cache, conv_buf, s0, h0_re, h0_im):
    s_len = x2d.shape[0] // n_batch
    n_groups = n_batch // S5_BATCH
    q, k, v, qkvb, gate, uc, ba = _proj(x2d, lw["w_proj"], n_batch, prompt=False)
    out_a = _attn_sample(q, k, v, k_cache, v_cache, layer, lw["bias_cache"], lw["bias_new"], n_batch)

    def pad_rows(t):
        t = t.reshape(n_batch, s_len, -1)
        return jnp.pad(t, ((0, 0), (0, PAIR - s_len), (0, 0))).reshape(n_batch * PAIR, -1)

    out_b, conv_o, s_new = _delta(pad_rows(qkvb), pad_rows(gate), pad_rows(ba),
                                  _pad_conv_state(conv_buf), s0.astype(F32), lw, n_batch, t_valid=s_len, rows_per_step=PAIR)
    out_b = out_b.reshape(n_batch, PAIR, B_WIDTH)[:, :s_len].reshape(n_batch * s_len, B_WIDTH)

    grp = lambda t: t.astype(F32).reshape(n_groups, S5_BATCH, C_LANES)
    out_c, h_re, h_im = _s5(uc.reshape(n_groups, S5_BATCH, s_len, C_WIDTH), grp(h0_re), grp(h0_im), lw, s_len)
    y = _post(x2d, out_a, out_b, out_c.reshape(n_batch * s_len, C_WIDTH), lw)
    heads = lambda t: t.reshape(n_batch, s_len, A_HEADS, A_HEAD_DIM)
    state = lambda t: t.reshape(n_batch, C_GROUPS, C_STATE)
    return y, (heads(k), heads(v), conv_o[:, -(CONV_W - 1):], s_new, state(h_re), state(h_im))


def kernel(x_prompt, x_sample, cache_a_k, cache_a_v, state_b_conv, state_b_ssm, state_c_re, state_c_im, w_in, a_rel_bias, b_conv_w, b_conv_b, b_a_log, b_dt_bias, b_norm_w, c_a_re, c_a_im, c_log_dt, c_b_re, c_b_im, c_c_re, c_c_im, c_d, c_glu_w, c_glu_b, w_out, ln1_g, ln1_b, w_up, b_up, w_down, ln2_g, ln2_b):
    n_p, t_p, _ = x_prompt.shape
    n_s, t_s, _ = x_sample.shape
    n_cache = cache_a_k.shape[2]
    yp = x_prompt.reshape(n_p * t_p, D_MODEL)
    ys = x_sample.reshape(n_s * t_s, D_MODEL)
    feat_major = lambda t: t.astype(F32).transpose(0, 1, 3, 4, 2).reshape(DEPTH, n_s, A_WIDTH, n_cache)
    k_cache, v_cache = feat_major(cache_a_k), feat_major(cache_a_v)
    p_out, s_out = [], []
    for l in range(DEPTH):
        lw = _layer_weights(l, w_in, a_rel_bias, b_conv_w, b_conv_b, b_a_log, b_dt_bias, b_norm_w, c_a_re, c_a_im,
                            c_log_dt, c_b_re, c_b_im, c_c_re, c_c_im, c_d, c_glu_w, c_glu_b, w_out, ln1_g, ln1_b,
                            w_up, b_up, w_down, ln2_g, ln2_b, t_s, n_cache)
        yp, st_p = _prompt_layer(yp, lw, n_p)
        ys, st_s = _sample_layer(ys, lw, n_s, l, k_cache, v_cache, state_b_conv[l], state_b_ssm[l],
                                 state_c_re[l], state_c_im[l])
        p_out.append(st_p)
        s_out.append(st_s)
    stack = lambda outs, i: jnp.stack([o[i] for o in outs])
    return (yp.reshape(n_p, t_p, D_MODEL), ys.reshape(n_s, t_s, D_MODEL),
            *[stack(p_out, i) for i in range(6)], *[stack(s_out, i) for i in range(6)])
```

```python
import functools

import jax
import jax.numpy as jnp
import numpy as np
from jax import lax
from jax.experimental import pallas as pl
from jax.experimental.pallas import tpu as pltpu

F32 = jnp.float32
BF16 = jnp.bfloat16

D_MODEL = 1024
DEPTH = 2
CHUNK = 64
A_HEADS = 8
A_HEAD_DIM = 64
A_WIDTH = A_HEADS * A_HEAD_DIM
A_WINDOW = 8 * CHUNK
MAX_REL = 256
B_HEADS = 4
B_HEAD_DIM = 64
B_WIDTH = B_HEADS * B_HEAD_DIM
CONV_W = 4
C_WIDTH = D_MODEL - A_WIDTH - B_WIDTH
C_GROUP = 16
C_GROUPS = C_WIDTH // C_GROUP
C_STATE = 64
C_LANES = C_GROUPS * C_STATE
D_FF = 4 * D_MODEL
DN_ALPHA = (2.0 * DEPTH) ** 0.25
LN_EPS = 1e-5
RMS_EPS = 1e-6
NEG_INF = -1e30

SUBLANES = 8
LANES = 128
VMEM_LIMIT_BYTES = 56 * 1024 * 1024

ROW_TILE = 512
ATT_Q_BLOCK = 512
ATT_Q_SUB = 2 * CHUNK
ATT_K_SUB = ATT_Q_SUB + A_WINDOW
PAIR = 2 * CHUNK
DELTA_ROWS = 8 * PAIR
S5_TIME_BLOCK = 128
S5_BATCH = SUBLANES
FF_SLAB = 1024

_SEG = {}
_off = 0
for _name, _w in (("q", A_WIDTH), ("k", A_WIDTH), ("v", A_WIDTH), ("qkvb", 3 * B_WIDTH), ("gate", B_WIDTH),
                  ("uc", C_WIDTH), ("ba", LANES)):
    _SEG[_name] = (_off, _off + _w)
    _off += _w
PROJ_WIDTH = _off


def _bdot(a, b):
    return jnp.dot(a.astype(BF16), b.astype(BF16), preferred_element_type=F32)


def _bdot_nt(a, b):
    return lax.dot_general(a.astype(BF16), b.astype(BF16), (((1,), (1,)), ((), ())), preferred_element_type=F32)


def _split3(a):
    hi = a.astype(BF16)
    r = a - hi.astype(F32)
    mid = r.astype(BF16)
    lo = (r - mid.astype(F32)).astype(BF16)
    return hi, mid, lo


def _dot_f32_by_exact(a, e):
    lhs = jnp.concatenate(_split3(a), axis=1)
    return jnp.dot(lhs, jnp.concatenate([e, e, e], axis=0), preferred_element_type=F32)


def _dot_exact_by_f32(e, a):
    rhs = jnp.concatenate(_split3(a), axis=0)
    return jnp.dot(jnp.concatenate([e, e, e], axis=1), rhs, preferred_element_type=F32)


def _blk(idx, size):
    return jnp.bitwise_and(idx, -size)


def _layer_norm(x, g, b):
    mu = jnp.mean(x, -1, keepdims=True)
    xc = x - mu
    var = jnp.mean(xc * xc, -1, keepdims=True)
    return xc * lax.rsqrt(var + LN_EPS) * g + b


def _silu(x):
    return x * jax.nn.sigmoid(x)


def _softplus(x):
    return jnp.maximum(x, 0.0) + jnp.log1p(jnp.exp(-jnp.abs(x)))


def _gelu_tanh(x):
    c = float(np.sqrt(2.0 / np.pi))
    return x * (0.5 * (1.0 + jnp.tanh(c * (x + 0.044715 * (x * x * x)))))


def _const_spec(shape):
    nd = len(shape)
    return pl.BlockSpec(shape, lambda *_: (0,) * nd, pipeline_mode=pl.Buffered(1))


def _params(semantics):
    return pltpu.CompilerParams(dimension_semantics=semantics, vmem_limit_bytes=VMEM_LIMIT_BYTES)


_PROJ_ORDER = ("q", "k", "v", "qkvb", "gate", "uc", "ba")


def _proj_kernel(x_ref, w_ref, *out_refs, tiles_per_stream):
    xb = x_ref[...].astype(BF16)
    vals = {}
    for name, ref in zip(_PROJ_ORDER, out_refs):
        lo, hi = _SEG[name]
        vals[name] = jnp.dot(xb, w_ref[:, lo:hi], preferred_element_type=F32)
        ref[...] = vals[name].astype(ref.dtype)
    if len(out_refs) > len(_PROJ_ORDER):
        kt_ref, vt_ref = out_refs[len(_PROJ_ORDER):]

        @pl.when(pl.program_id(0) % tiles_per_stream == tiles_per_stream - 1)
        def _():
            kt_ref[...] = vals["k"].T
            vt_ref[...] = vals["v"].T


def _proj(x2d, w, n_batch, prompt):
    n_rows = x2d.shape[0]
    tm = min(ROW_TILE, n_rows)
    n_steps = n_rows // tm
    per_b = n_rows // n_batch // tm if prompt else 1
    widths = {name: hi - lo for name, (lo, hi) in _SEG.items()}
    out_shape, out_specs = [], []
    for name in _PROJ_ORDER:
        wd = widths[name]
        out_shape.append(jax.ShapeDtypeStruct((n_rows, wd), BF16 if prompt and name in ("k", "v") else F32))
        out_specs.append(pl.BlockSpec((tm, wd), lambda i: (i, 0)))
    if prompt:
        assert tm == A_WINDOW
        for _ in range(2):
            out_shape.append(jax.ShapeDtypeStruct((n_batch, A_WIDTH, A_WINDOW), F32))
            out_specs.append(pl.BlockSpec((None, A_WIDTH, A_WINDOW), lambda i: (i // per_b, 0, 0)))
    return pl.pallas_call(
        functools.partial(_proj_kernel, tiles_per_stream=per_b),
        grid=(n_steps,),
        in_specs=[pl.BlockSpec((tm, D_MODEL), lambda i: (i, 0)), _const_spec(w.shape)],
        out_specs=out_specs,
        out_shape=out_shape,
        compiler_params=_params(("arbitrary",)),
        name="proj",
    )(x2d, w)


def _softmax_pv(q2s, key_segs, val_segs, bias_segs, valid_segs, feature_major=None):
    feature_major = feature_major or [False] * len(valid_segs)
    nn = (((1,), (0,)), ((), ()))
    nt = (((1,), (1,)), ((), ()))
    m_rows = q2s[0].shape[0]
    low = lax.broadcasted_iota(jnp.int32, q2s[0].shape, 1) < A_HEAD_DIM
    qs = [jnp.concatenate([jnp.where(low, q2, 0.0), jnp.where(low, 0.0, q2)], axis=0).astype(BF16) for q2 in q2s]
    scores = []
    for q, ksegs, biases in zip(qs, key_segs, bias_segs):
        sc = []
        for kseg, bias, valid, fm in zip(ksegs, biases, valid_segs, feature_major):
            s = lax.dot_general(q, kseg, nn if fm else nt, preferred_element_type=F32) + bias
            if valid is not None:
                s = jnp.where(valid, s, NEG_INF)
            sc.append(s)
        scores.append(sc)
    tops = [functools.reduce(jnp.maximum, [jnp.max(s, -1, keepdims=True) for s in sc]) for sc in scores]
    probs = [[jnp.exp(s - m) for s in sc] for sc, m in zip(scores, tops)]
    denoms = [functools.reduce(jnp.add, [jnp.sum(p, -1, keepdims=True) for p in pr]) for pr in probs]
    pvs = [functools.reduce(jnp.add, [lax.dot_general(p.astype(BF16), vseg, nt if fm else nn,
                                                      preferred_element_type=F32)
                                      for p, vseg, fm in zip(pr, vsegs, feature_major)])
           for pr, vsegs in zip(probs, val_segs)]
    outs = []
    for pv, denom in zip(pvs, denoms):
        pvn = pv * (1.0 / denom)
        outs.append(jnp.where(low, pvn[:m_rows], pvn[m_rows:]))
    return outs


def _attn_prompt_kernel(q_ref, kp_ref, kc_ref, vp_ref, vc_ref, bias_ref, o_ref):
    jb = pl.program_id(1)
    scale = A_HEAD_DIM ** -0.5
    pairs = range(A_HEADS // 2)
    cols = [slice(hp * LANES, (hp + 1) * LANES) for hp in pairs]

    def band(prev_ref, cur_ref, r0, c):
        return jnp.concatenate([prev_ref[r0:, c], cur_ref[:r0 + ATT_Q_SUB, c]], axis=0)

    def body(first_block):
        for sub in range(ATT_Q_BLOCK // ATT_Q_SUB):
            r0 = sub * ATT_Q_SUB
            valid = None
            if first_block:
                valid = lax.broadcasted_iota(jnp.int32, (1, ATT_K_SUB), 1) >= ATT_Q_BLOCK - r0
            outs = _softmax_pv([q_ref[r0:r0 + ATT_Q_SUB, c] * scale for c in cols],
                               [[band(kp_ref, kc_ref, r0, c)] for c in cols],
                               [[band(vp_ref, vc_ref, r0, c)] for c in cols],
                               [[bias_ref[hp]] for hp in pairs], [valid])
            for c, out in zip(cols, outs):
                o_ref[r0:r0 + ATT_Q_SUB, c] = out

    pl.when(jb == 0)(lambda: body(True))
    pl.when(jb > 0)(lambda: body(False))


def _attn_prompt(q, k, v, bias, n_batch):
    n_rows = q.shape[0]
    per_b = n_rows // n_batch // ATT_Q_BLOCK
    blk = (ATT_Q_BLOCK, A_WIDTH)
    cur = pl.BlockSpec(blk, lambda b, j: (b * per_b + j, 0))
    prev = pl.BlockSpec(blk, lambda b, j: (b * per_b + jnp.maximum(j - 1, 0), 0))
    return pl.pallas_call(
        _attn_prompt_kernel,
        grid=(n_batch, per_b),
        in_specs=[cur, prev, cur, prev, cur, _const_spec(bias.shape)],
        out_specs=cur,
        out_shape=jax.ShapeDtypeStruct((n_rows, A_WIDTH), F32),
        compiler_params=_params(("parallel", "arbitrary")),
        name="attn_prompt",
    )(q, k, k, v, v, bias)


def _attn_sample_kernel(q_ref, kn_ref, vn_ref, kc_ref, vc_ref, bias_c_ref, bias_n_ref, o_ref):
    scale = A_HEAD_DIM ** -0.5
    pairs = range(A_HEADS // 2)
    cols = [slice(hp * LANES, (hp + 1) * LANES) for hp in pairs]
    outs = _softmax_pv([q_ref[:, c] * scale for c in cols],
                       [[kc_ref[c, :].astype(BF16), kn_ref[:, c].astype(BF16)] for c in cols],
                       [[vc_ref[c, :].astype(BF16), vn_ref[:, c].astype(BF16)] for c in cols],
                       [[bias_c_ref[hp], bias_n_ref[hp]] for hp in pairs], [None, None], [True, False])
    for c, out in zip(cols, outs):
        o_ref[:, c] = out


def _attn_sample(q, k, v, k_cache, v_cache, layer, bias_c, bias_n, n_batch):
    n_rows = q.shape[0]
    s_len = n_rows // n_batch
    n_cache = k_cache.shape[3]
    new = pl.BlockSpec((s_len, A_WIDTH), lambda b: (b, 0))
    cache = pl.BlockSpec((None, None, A_WIDTH, n_cache), lambda b: (layer, b, 0, 0))
    return pl.pallas_call(
        _attn_sample_kernel,
        grid=(n_batch,),
        in_specs=[new, new, new, cache, cache, _const_spec(bias_c.shape), _const_spec(bias_n.shape)],
        out_specs=new,
        out_shape=jax.ShapeDtypeStruct((n_rows, A_WIDTH), F32),
        compiler_params=_params(("parallel",)),
        name="attn_sample",
    )(q, k, v, k_cache, v_cache, bias_c, bias_n)


def _rel_bias_blocks(table, n_q, n_k, q_offset, banded):
    period = n_q + n_k - 1
    rel = q_offset - ((np.arange(period) + n_q - 1) % period - (n_q - 1))
    vec = jnp.take(table.astype(F32), jnp.asarray(np.clip(rel, -MAX_REL, MAX_REL) + MAX_REL), axis=1)
    n_heads = table.shape[0]
    bias = jnp.tile(vec, (1, n_q))[:, :n_q * (period - 1)].reshape(n_heads, n_q, period - 1)[:, :, :n_k]
    if banded:
        i = np.arange(n_q)[:, None]
        jj = np.arange(n_k)[None, :] - (i // CHUNK) * CHUNK
        inband = (jj >= 0) & (jj < A_WINDOW + CHUNK)
        bias = jnp.where(jnp.asarray(inband), bias, NEG_INF)
    return bias.reshape(n_heads // 2, 2 * n_q, n_k)


def _delta_kernel(qkvb_ref, gate_ref, ba_ref, cst_ref, s0_ref, cw_ref, cb_ref, alog_ref, dtb_ref, nw_ref,
                  o_ref, cout_ref, sout_ref, xbuf, s_ref, ob_ref, *, t_valid, n_rows):
    j = pl.program_id(1)
    n_blk = n_rows // PAIR
    hd = B_HEAD_DIM
    low = lax.broadcasted_iota(jnp.int32, (PAIR, LANES), 1) < hd
    low_s = lax.broadcasted_iota(jnp.int32, (hd, LANES), 1) < hd
    zeros_s = jnp.zeros((hd, LANES), F32)

    @pl.when(j == 0)
    def _():
        xbuf[0:SUBLANES, :] = cst_ref[0]
        for h in range(B_HEADS):
            s0 = s0_ref[0, h]
            s_ref[h] = jnp.concatenate([s0, jnp.zeros_like(s0)] if h % 2 == 0 else [jnp.zeros_like(s0), s0], axis=1)

    xbuf[SUBLANES:SUBLANES + n_rows, :] = qkvb_ref[...]
    t_last = n_rows - PAIR + t_valid
    cout_ref[0] = xbuf[t_last:t_last + SUBLANES, :]

    hrow = lax.broadcasted_iota(jnp.int32, (B_WIDTH, B_WIDTH), 0)
    hcol = lax.broadcasted_iota(jnp.int32, (B_WIDTH, B_WIDTH), 1)
    head_ones = jnp.where(_blk(hrow, B_HEAD_DIM) == _blk(hcol, B_HEAD_DIM), 1.0, 0.0).astype(BF16)

    def l2n(t):
        return t * lax.rsqrt(_dot_f32_by_exact(t * t, head_ones) + RMS_EPS)

    row = lax.broadcasted_iota(jnp.int32, (PAIR, PAIR), 0)
    col = lax.broadcasted_iota(jnp.int32, (PAIR, PAIR), 1)
    same = _blk(row, CHUNK) == _blk(col, CHUNK)
    tril = same & (row >= col)
    strict = same & (row > col)
    first = row < CHUNK
    ltri = jnp.where(tril, 1.0, 0.0).astype(BF16)
    n_pairs = B_HEADS // 2
    chunks = range(PAIR // CHUNK)

    def pair_lanes(t, p):
        return jnp.where(low, t[:, (2 * p) * LANES:(2 * p + 1) * LANES], t[:, (2 * p + 1) * LANES:(2 * p + 2) * LANES])

    def own(h, t, other):
        return jnp.where(low, t, other) if h % 2 == 0 else jnp.where(low, other, t)

    prepared, solved = {}, {}

    def prepare(n):
        r0 = n * PAIR
        conv = cb_ref[...]
        for w in range(CONV_W):
            start = SUBLANES - (CONV_W - 1) + w + r0
            conv = conv + xbuf[start:start + PAIR, :] * cw_ref[w:w + 1, :]
        conv = _silu(conv)
        yield
        qn = l2n(conv[:, 0:B_WIDTH]) * (B_HEAD_DIM ** -0.5)
        yield
        kn = l2n(conv[:, B_WIDTH:2 * B_WIDTH])
        vv = conv[:, 2 * B_WIDTH:3 * B_WIDTH]
        yield
        ba = ba_ref[r0:r0 + PAIR, :]
        spread = lambda c: jnp.broadcast_to(ba[:, c:c + 1], (PAIR, LANES))
        beta = jax.nn.sigmoid(jnp.concatenate(
            [jnp.where(low, spread(2 * p), spread(2 * p + 1)) for p in range(n_pairs)], axis=1))
        a_in = jnp.concatenate([spread(B_HEADS + h) for h in range(B_HEADS)], axis=1)
        g = -jnp.exp(alog_ref[...]) * _softplus(a_in + dtb_ref[...])
        if t_valid < PAIR and n == n_blk - 1:
            def live(t):
                return jnp.where(lax.broadcasted_iota(jnp.int32, t.shape, 0) < t_valid, t, 0.0)
            kn, vv, beta, g = live(kn), live(vv), live(beta), live(g)
        yield
        gc = _dot_exact_by_f32(ltri, g)
        g_last = [gc[(c + 1) * CHUNK - 1:(c + 1) * CHUNK, :] for c in chunks]
        gl = jnp.concatenate([jnp.broadcast_to(t, (CHUNK, t.shape[1])) for t in g_last], axis=0)
        egc = jnp.exp(gc)
        erest = jnp.exp(gl - gc)
        d = {"egl": [jnp.exp(t) for t in g_last], "k": [], "q": [], "kb": [], "vb": [], "kbg_r": [], "qd_r": [],
             "kd": [], "decay": [], "decay_strict": []}
        yield
        for p in range(n_pairs):
            ls = slice(p * LANES, (p + 1) * LANES)
            k, q, v, b = kn[:, ls], qn[:, ls], vv[:, ls], beta[:, ls]
            eg = pair_lanes(egc, p)
            d["k"].append(k)
            d["q"].append(q)
            d["kb"].append(k * b)
            d["vb"].append(v * b)
            d["kbg_r"].append(pltpu.roll(k * b * eg, hd, axis=1))
            d["qd_r"].append(pltpu.roll(q * eg, hd, axis=1))
            d["kd"].append(k * pair_lanes(erest, p))
            yield
        for h in range(B_HEADS):
            gcol = gc[:, h * LANES:(h + 1) * LANES]
            e = jnp.exp(gcol - gcol.T)
            d["decay"].append(jnp.where(tril, e, 0.0))
            d["decay_strict"].append(jnp.where(strict, e, 0.0))
            yield
        prepared[n] = d
    heads = range(B_HEADS)

    def solve(n):
        d = prepared[n]
        a_low = [_bdot_nt(own(h, d["kb"][h // 2], 0.0), d["k"][h // 2]) * d["decay_strict"][h] for h in heads]
        yield
        base = 8
        diag = _blk(row, base) == _blk(col, base)
        eye = jnp.where(row == col, 1.0, 0.0)
        a0 = [jnp.where(diag, a, 0.0) for a in a_low]
        xs = [eye - a for a in a0]
        ps = [_bdot(a, a) for a in a0]
        yield
        xs = [x + _bdot(x, p) for x, p in zip(xs, ps)]
        ps = [_bdot(p, p) for p in ps]
        yield
        xs = [x + _bdot(x, p) for x, p in zip(xs, ps)]
        yield
        bs = base
        while bs < CHUNK:
            off = (_blk(row, 2 * bs) == _blk(col, 2 * bs)) & (_blk(row, bs) != _blk(col, bs))
            ts = [_bdot(jnp.where(off, a, 0.0), x) for a, x in zip(a_low, xs)]
            yield
            xs = [x - _bdot(x, t) for x, t in zip(xs, ts)]
            yield
            bs *= 2
        uws = [_bdot(x, own(h, d["vb"][h // 2], d["kbg_r"][h // 2])) for h, x in zip(heads, xs)]
        yield
        qus = []
        for h in heads:
            qk = _bdot_nt(own(h, d["q"][h // 2], 0.0), d["k"][h // 2]) * d["decay"][h]
            qus.append(_bdot(qk, uws[h]))
        yield
        kus = []
        for p in range(n_pairs):
            rhs = jnp.concatenate([jnp.where(first if c == 0 else ~first, uws[2 * p + e], 0.0)
                                   for e in range(2) for c in chunks], axis=1)
            kus.append(_bdot(d["kd"][p].T, rhs))
            yield
        solved[n] = (qus, kus)

    states = [s_ref[h] for h in heads]

    def advance(n):
        d = prepared[n]
        qus, kus = solved[n]
        for c in chunks:
            outs = []
            for h in heads:
                p, e = divmod(h, 2)
                ku = kus[p][e * hd:(e + 1) * hd, (2 * e + c) * LANES:(2 * e + c + 1) * LANES]
                qu = qus[h][c * CHUNK:(c + 1) * CHUNK, :]
                kq = jnp.concatenate([ku, d["qd_r"][p][c * CHUNK:(c + 1) * CHUNK] - qu], axis=0)
                s_ext = jnp.concatenate([zeros_s, states[h]] if e == 0 else [states[h], zeros_s], axis=0)
                r = _bdot(kq, s_ext)
                gt = d["egl"][c][:, h * LANES:(h + 1) * LANES]
                new = gt * states[h] - r[:hd] + ku
                states[h] = jnp.where(low_s, new, 0.0) if e == 0 else jnp.where(low_s, 0.0, new)
                outs.append(r[hd:] + qu)
            r0 = n * PAIR + c * CHUNK
            for p in range(n_pairs):
                ob_ref[r0:r0 + CHUNK, p * LANES:(p + 1) * LANES] = jnp.where(low_s, outs[2 * p], outs[2 * p + 1])
            yield
        rs = slice(n * PAIR, (n + 1) * PAIR)
        ob = ob_ref[rs, :]
        ms = _dot_f32_by_exact(ob * ob, head_ones) * (1.0 / hd)
        o_ref[rs, :] = ob * lax.rsqrt(ms + RMS_EPS) * nw_ref[...] * _silu(gate_ref[rs, :])
        yield

    def advance_all(blocks):
        for n in blocks:
            yield from advance(n)

    group = 2 if n_blk % 2 == 0 else 1
    n_groups = n_blk // group
    for slot in range(n_groups + 2):
        active = []
        if 0 <= slot - 2 < n_groups:
            active.append(advance_all(range((slot - 2) * group, (slot - 1) * group)))
        if 0 <= slot - 1 < n_groups:
            active += [solve(n) for n in range((slot - 1) * group, slot * group)]
        if slot < n_groups:
            active += [prepare(n) for n in range(slot * group, (slot + 1) * group)]
        while active:
            for gen in list(active):
                if next(gen, "done") == "done":
                    active.remove(gen)

    for h in heads:
        s_ref[h] = states[h]
    xbuf[0:SUBLANES, :] = xbuf[n_rows:n_rows + SUBLANES, :]

    @pl.when(j == pl.num_programs(1) - 1)
    def _():
        for h in range(B_HEADS):
            sout_ref[0, h] = s_ref[h][:, (h % 2) * hd:(h % 2 + 1) * hd]


def _delta(qkvb, gate, ba, conv_state, s0, lw, n_batch, t_valid, rows_per_step):
    n_rows = qkvb.shape[0]
    per_b = n_rows // n_batch // rows_per_step

    def rows(width):
        return pl.BlockSpec((rows_per_step, width), lambda b, j: (b * per_b + j, 0))

    cst = pl.BlockSpec((1, SUBLANES, 3 * B_WIDTH), lambda b, j: (b, 0, 0))
    st = pl.BlockSpec((1, B_HEADS, B_HEAD_DIM, B_HEAD_DIM), lambda b, j: (b, 0, 0, 0))
    consts = [lw["conv_w"], lw["conv_b"], lw["a_log"], lw["dt_bias"], lw["norm_w"]]
    return pl.pallas_call(
        functools.partial(_delta_kernel, t_valid=t_valid, n_rows=rows_per_step),
        grid=(n_batch, per_b),
        in_specs=[rows(3 * B_WIDTH), rows(B_WIDTH), rows(LANES), cst, st]
        + [_const_spec(c.shape) for c in consts],
        out_specs=[rows(B_WIDTH), cst, st],
        out_shape=[jax.ShapeDtypeStruct((n_rows, B_WIDTH), F32),
                   jax.ShapeDtypeStruct((n_batch, SUBLANES, 3 * B_WIDTH), F32),
                   jax.ShapeDtypeStruct((n_batch, B_HEADS, B_HEAD_DIM, B_HEAD_DIM), F32)],
        scratch_shapes=[pltpu.VMEM((SUBLANES + rows_per_step, 3 * B_WIDTH), F32),
                        pltpu.VMEM((B_HEADS, B_HEAD_DIM, LANES), F32),
                        pltpu.VMEM((rows_per_step, B_WIDTH), F32)],
        compiler_params=_params(("parallel", "arbitrary")),
        name="delta",
    )(qkvb, gate, ba, conv_state, s0, *consts)


def _s5_kernel(u_ref, h0r_ref, h0i_ref, lr_ref, li_ref, bre_ref, bim_ref, cre_ref, cim_ref, d_ref, gw_ref, gb_ref,
               o_ref, hro_ref, hio_ref, xr, xi, hs, *, t_block):
    tb = pl.program_id(1)

    @pl.when(tb == 0)
    def _():
        hs[0] = h0r_ref[0]
        hs[1] = h0i_ref[0]

    u = jnp.swapaxes(u_ref[0], 0, 1).reshape(t_block * S5_BATCH, C_WIDTH)
    ub = u.astype(BF16)
    xr[...] = jnp.dot(ub, bre_ref[...], preferred_element_type=F32)
    xi[...] = jnp.dot(ub, bim_ref[...], preferred_element_type=F32)
    lam_r = jnp.broadcast_to(lr_ref[...], (S5_BATCH, C_LANES))
    lam_i = jnp.broadcast_to(li_ref[...], (S5_BATCH, C_LANES))

    def step(t, carry):
        h_r, h_i = carry
        rows = pl.ds(pl.multiple_of(t * S5_BATCH, S5_BATCH), S5_BATCH)
        n_r = lam_r * h_r - lam_i * h_i + xr[rows, :]
        n_i = lam_r * h_i + lam_i * h_r + xi[rows, :]
        xr[rows, :] = n_r
        xi[rows, :] = n_i
        return n_r, n_i

    h_r, h_i = lax.fori_loop(0, t_block, step, (hs[0], hs[1]), unroll=4)
    hs[0] = h_r
    hs[1] = h_i
    hro_ref[0] = h_r
    hio_ref[0] = h_i
    y = (jnp.dot(xr[...].astype(BF16), cre_ref[...], preferred_element_type=F32)
         - jnp.dot(xi[...].astype(BF16), cim_ref[...], preferred_element_type=F32) + d_ref[...] * u)
    z = _gelu_tanh(y)
    out = z * jax.nn.sigmoid(jnp.dot(z.astype(BF16), gw_ref[...], preferred_element_type=F32) + gb_ref[...])
    o_ref[0] = jnp.swapaxes(out.reshape(t_block, S5_BATCH, C_WIDTH), 0, 1)


def _s5(u, h0_re, h0_im, lw, t_block):
    n_groups, _, t_len, _ = u.shape
    rows_blk = t_block * S5_BATCH
    n_tb = t_len // t_block
    u_spec = pl.BlockSpec((1, S5_BATCH, t_block, C_WIDTH), lambda g, t: (g, 0, t, 0))
    h_spec = pl.BlockSpec((1, S5_BATCH, C_LANES), lambda g, t: (g, 0, 0))
    consts = [lw["lam_re"], lw["lam_im"], lw["b_re"], lw["b_im"], lw["c_re"], lw["c_im"], lw["c_d"],
              lw["glu_w"], lw["glu_b"]]
    return pl.pallas_call(
        functools.partial(_s5_kernel, t_block=t_block),
        grid=(n_groups, n_tb),
        in_specs=[u_spec, h_spec, h_spec] + [_const_spec(c.shape) for c in consts],
        out_specs=[u_spec, h_spec, h_spec],
        out_shape=[jax.ShapeDtypeStruct(u.shape, F32),
                   jax.ShapeDtypeStruct(h0_re.shape, F32), jax.ShapeDtypeStruct(h0_im.shape, F32)],
        scratch_shapes=[pltpu.VMEM((rows_blk, C_LANES), F32), pltpu.VMEM((rows_blk, C_LANES), F32),
                        pltpu.VMEM((2, S5_BATCH, C_LANES), F32)],
        compiler_params=_params(("parallel", "arbitrary")),
        name="s5",
    )(u, h0_re, h0_im, *consts)


def _post_kernel(x_ref, a_ref, b_ref, c_ref, wo_ref, g1_ref, b1_ref, wu_ref, bu_ref, wd_ref, g2_ref, b2_ref, o_ref):
    mix = (jnp.dot(a_ref[...].astype(BF16), wo_ref[0:A_WIDTH, :], preferred_element_type=F32)
           + jnp.dot(b_ref[...].astype(BF16), wo_ref[A_WIDTH:A_WIDTH + B_WIDTH, :], preferred_element_type=F32)
           + jnp.dot(c_ref[...].astype(BF16), wo_ref[A_WIDTH + B_WIDTH:, :], preferred_element_type=F32))
    x1 = _layer_norm(DN_ALPHA * x_ref[...] + mix, g1_ref[...], b1_ref[...])
    x1b = x1.astype(BF16)
    acc = jnp.zeros(x1.shape, F32)
    for s in range(D_FF // FF_SLAB):
        cs = slice(s * FF_SLAB, (s + 1) * FF_SLAB)
        hid = jnp.dot(x1b, wu_ref[:, cs], preferred_element_type=F32) + bu_ref[:, cs]
        hid = jnp.square(jnp.maximum(hid, 0.0))
        acc = acc + jnp.dot(hid.astype(BF16), wd_ref[cs, :], preferred_element_type=F32)
    o_ref[...] = _layer_norm(DN_ALPHA * x1 + acc, g2_ref[...], b2_ref[...])


def _post(x2d, out_a, out_b, out_c, lw):
    n_rows = x2d.shape[0]
    tm = min(ROW_TILE, n_rows)

    def rows(width):
        return pl.BlockSpec((tm, width), lambda i: (i, 0))

    c_spec = rows(C_WIDTH)
    consts = [lw["w_out"], lw["ln1_g"], lw["ln1_b"], lw["w_up"], lw["b_up"], lw["w_down"], lw["ln2_g"], lw["ln2_b"]]
    return pl.pallas_call(
        _post_kernel,
        grid=(n_rows // tm,),
        in_specs=[rows(D_MODEL), rows(A_WIDTH), rows(B_WIDTH), c_spec] + [_const_spec(c.shape) for c in consts],
        out_specs=rows(D_MODEL),
        out_shape=jax.ShapeDtypeStruct((n_rows, D_MODEL), F32),
        compiler_params=_params(("parallel",)),
        name="post",
    )(x2d, out_a, out_b, out_c, *consts)


def _layer_weights(l, w_in, a_rel_bias, b_conv_w, b_conv_b, b_a_log, b_dt_bias, b_norm_w, c_a_re, c_a_im, c_log_dt,
                   c_b_re, c_b_im, c_c_re, c_c_im, c_d, c_glu_w, c_glu_b, w_out, ln1_g, ln1_b, w_up, b_up, w_down,
                   ln2_g, ln2_b, s_len, n_cache):
    wi = w_in[l]
    o_q, o_k, o_v = 0, A_WIDTH, 2 * A_WIDTH
    o_qkvb = 3 * A_WIDTH
    o_beta = o_qkvb + 3 * B_WIDTH
    o_a = o_beta + B_HEADS
    o_gate = o_a + B_HEADS
    o_uc = o_gate + B_WIDTH
    per_group = lambda cols: jnp.repeat(cols, LANES, axis=-1)
    assert o_a == o_beta + B_HEADS
    w_proj = jnp.concatenate([
        wi[:, o_q:o_q + A_WIDTH], wi[:, o_k:o_k + A_WIDTH], wi[:, o_v:o_v + A_WIDTH],
        wi[:, o_qkvb:o_qkvb + 3 * B_WIDTH], wi[:, o_gate:o_gate + B_WIDTH], wi[:, o_uc:o_uc + C_WIDTH],
        wi[:, o_beta:o_beta + 2 * B_HEADS], jnp.zeros((D_MODEL, LANES - 2 * B_HEADS), wi.dtype)], axis=1).astype(BF16)

    table = a_rel_bias[l]
    bias_sample = _rel_bias_blocks(table, s_len, n_cache + s_len, n_cache, banded=False)

    a_re, a_im = c_a_re[l].astype(F32), c_a_im[l].astype(F32)
    dt = jnp.exp(c_log_dt[l].astype(F32))[:, None]
    mag = jnp.exp(dt * a_re)
    lam_re, lam_im = mag * jnp.cos(dt * a_im), mag * jnp.sin(dt * a_im)
    den = a_re * a_re + a_im * a_im
    coef_re = ((lam_re - 1.0) * a_re + lam_im * a_im) / den
    coef_im = (lam_im * a_re - (lam_re - 1.0) * a_im) / den
    bre, bim = c_b_re[l].astype(F32), c_b_im[l].astype(F32)
    bb_re = coef_re[..., None] * bre - coef_im[..., None] * bim
    bb_im = coef_re[..., None] * bim + coef_im[..., None] * bre
    eye = jnp.eye(C_GROUPS, dtype=F32)
    in_bd = lambda t: jnp.einsum("gph,gk->ghkp", t, eye).reshape(C_WIDTH, C_LANES).astype(BF16)
    out_bd = lambda t: jnp.einsum("ghp,gk->gpkh", t.astype(F32), eye).reshape(C_LANES, C_WIDTH).astype(BF16)
    row = lambda t: t.astype(F32).reshape(1, -1)

    return {
        "w_proj": w_proj,
        "bias_prompt": _rel_bias_blocks(table, ATT_Q_SUB, ATT_K_SUB, A_WINDOW, banded=True),
        "bias_cache": bias_sample[:, :, :n_cache], "bias_new": bias_sample[:, :, n_cache:],
        "conv_w": b_conv_w[l].astype(F32), "conv_b": row(b_conv_b[l]),
        "a_log": row(per_group(b_a_log[l])), "dt_bias": row(per_group(b_dt_bias[l])),
        "norm_w": row(jnp.tile(b_norm_w[l], B_HEADS)),
        "lam_re": row(lam_re), "lam_im": row(lam_im), "b_re": in_bd(bb_re), "b_im": in_bd(bb_im),
        "c_re": out_bd(c_c_re[l]), "c_im": out_bd(c_c_im[l]), "c_d": row(c_d[l]),
        "glu_w": c_glu_w[l].astype(BF16), "glu_b": row(c_glu_b[l]),
        "w_out": w_out[l].astype(BF16), "ln1_g": row(ln1_g[l]), "ln1_b": row(ln1_b[l]),
        "w_up": w_up[l].astype(BF16), "b_up": row(b_up[l]), "w_down": w_down[l].astype(BF16),
        "ln2_g": row(ln2_g[l]), "ln2_b": row(ln2_b[l]),
    }


def _pad_conv_state(conv_buf):
    return jnp.pad(conv_buf.astype(F32), ((0, 0), (SUBLANES - (CONV_W - 1), 0), (0, 0)))


def _prompt_layer(x2d, lw, n_batch):
    t_len = x2d.shape[0] // n_batch
    q, k, v, qkvb, gate, uc, ba, k_t, v_t = _proj(x2d, lw["w_proj"], n_batch, prompt=True)
    out_a = _attn_prompt(q, k, v, lw["bias_prompt"], n_batch)
    out_b, conv_o, s_new = _delta(
        qkvb, gate, ba, jnp.zeros((n_batch, SUBLANES, 3 * B_WIDTH), F32),
        jnp.zeros((n_batch, B_HEADS, B_HEAD_DIM, B_HEAD_DIM), F32), lw, n_batch, t_valid=PAIR, rows_per_step=DELTA_ROWS)
    zeros_h = jnp.zeros((1, S5_BATCH, C_LANES), F32)
    out_c, h_re, h_im = _s5(uc.reshape(1, n_batch, t_len, C_WIDTH), zeros_h, zeros_h, lw, S5_TIME_BLOCK)
    y = _post(x2d, out_a, out_b, out_c.reshape(n_batch * t_len, C_WIDTH), lw)
    heads = lambda t: t.reshape(n_batch, A_HEADS, A_HEAD_DIM, A_WINDOW).transpose(0, 3, 1, 2)
    state = lambda t: t.reshape(n_batch, C_GROUPS, C_STATE)
    return y, (heads(k_t), heads(v_t), conv_o[:, -(CONV_W - 1):], s_new, state(h_re), state(h_im))


def _sample_layer(x2d, lw, n_batch, layer, k_cache, v_cache, conv_buf, s0, h0_re, h0_im):
    s_len = x2d.shape[0] // n_batch
    n_groups = n_batch // S5_BATCH
    q, k, v, qkvb, gate, uc, ba = _proj(x2d, lw["w_proj"], n_batch, prompt=False)
    out_a = _attn_sample(q, k, v, k_cache, v_cache, layer, lw["bias_cache"], lw["bias_new"], n_batch)

    def pad_rows(t):
        t = t.reshape(n_batch, s_len, -1)
        return jnp.pad(t, ((0, 0), (0, PAIR - s_len), (0, 0))).reshape(n_batch * PAIR, -1)

    out_b, conv_o, s_new = _delta(pad_rows(qkvb), pad_rows(gate), pad_rows(ba),
                                  _pad_conv_state(conv_buf), s0.astype(F32), lw, n_batch, t_valid=s_len, rows_per_step=PAIR)
    out_b = out_b.reshape(n_batch, PAIR, B_WIDTH)[:, :s_len].reshape(n_batch * s_len, B_WIDTH)

    grp = lambda t: t.astype(F32).reshape(n_groups, S5_BATCH, C_LANES)
    out_c, h_re, h_im = _s5(uc.reshape(n_groups, S5_BATCH, s_len, C_WIDTH), grp(h0_re), grp(h0_im), lw, s_len)
    y = _post(x2d, out_a, out_b, out_c.reshape(n_batch * s_len, C_WIDTH), lw)
    heads = lambda t: t.reshape(n_batch, s_len, A_HEADS, A_HEAD_DIM)
    state = lambda t: t.reshape(n_batch, C_GROUPS, C_STATE)
    return y, (heads(k), heads(v), conv_o[:, -(CONV_W - 1):], s_new, state(h_re), state(h_im))


def kernel(x_prompt, x_sample, cache_a_k, cache_a_v, state_b_conv, state_b_ssm, state_c_re, state_c_im, w_in, a_rel_bias, b_conv_w, b_conv_b, b_a_log, b_dt_bias, b_norm_w, c_a_re, c_a_im, c_log_dt, c_b_re, c_b_im, c_c_re, c_c_im, c_d, c_glu_w, c_glu_b, w_out, ln1_g, ln1_b, w_up, b_up, w_down, ln2_g, ln2_b):
    n_p, t_p, _ = x_prompt.shape
    n_s, t_s, _ = x_sample.shape
    n_cache = cache_a_k.shape[2]
    yp = x_prompt.reshape(n_p * t_p, D_MODEL)
    ys = x_sample.reshape(n_s * t_s, D_MODEL)
    feat_major = lambda t: t.astype(F32).transpose(0, 1, 3, 4, 2).reshape(DEPTH, n_s, A_WIDTH, n_cache)
    k_cache, v_cache = feat_major(cache_a_k), feat_major(cache_a_v)
    p_out, s_out = [], []
    for l in range(DEPTH):
        lw = _layer_weights(l, w_in, a_rel_bias, b_conv_w, b_conv_b, b_a_log, b_dt_bias, b_norm_w, c_a_re, c_a_im,
                            c_log_dt, c_b_re, c_b_im, c_c_re, c_c_im, c_d, c_glu_w, c_glu_b, w_out, ln1_g, ln1_b,
                            w_up, b_up, w_down, ln2_g, ln2_b, t_s, n_cache)
        yp, st_p = _prompt_layer(yp, lw, n_p)
        ys, st_s = _sample_layer(ys, lw, n_s, l, k_cache, v_cache, state_b_conv[l], state_b_ssm[l],
                                 state_c_re[l], state_c_im[l])
        p_out.append(st_p)
        s_out.append(st_s)
    stack = lambda outs, i: jnp.stack([o[i] for o in outs])
    return (yp.reshape(n_p, t_p, D_MODEL), ys.reshape(n_s, t_s, D_MODEL),
            *[stack(p_out, i) for i in range(6)], *[stack(s_out, i) for i in range(6)])
```

```python
import functools
from typing import NamedTuple

import jax
import jax.numpy as jnp
import numpy as np
from jax import lax
from jax.experimental import pallas as pl
from jax.experimental.pallas import tpu as pltpu

F32 = jnp.float32
BF16 = jnp.bfloat16

D_MODEL = 1024
DEPTH = 2
CHUNK = 64
A_HEADS = 8
A_HEAD_DIM = 64
A_WIDTH = A_HEADS * A_HEAD_DIM
A_WINDOW = 8 * CHUNK
MAX_REL = 256
B_HEADS = 4
B_HEAD_DIM = 64
B_WIDTH = B_HEADS * B_HEAD_DIM
CONV_W = 4
C_WIDTH = D_MODEL - A_WIDTH - B_WIDTH
C_GROUP = 16
C_GROUPS = C_WIDTH // C_GROUP
C_STATE = 64
C_LANES = C_GROUPS * C_STATE
D_FF = 4 * D_MODEL
DN_ALPHA = (2.0 * DEPTH) ** 0.25
LN_EPS = 1e-5
RMS_EPS = 1e-6
NEG_INF = -1e30

SUBLANES = 8
LANES = 128
VMEM_LIMIT_BYTES = 56 * 1024 * 1024

ROW_TILE = 512
ATT_Q_BLOCK = 512
ATT_Q_SUB = 2 * CHUNK
ATT_K_SUB = ATT_Q_SUB + A_WINDOW
PAIR = 2 * CHUNK
DELTA_ROWS = 8 * PAIR
S5_TIME_BLOCK = 128
S5_BATCH = SUBLANES
FF_SLAB = 1024

_SEG = {}
_off = 0
for _name, _w in (("q", A_WIDTH), ("k", A_WIDTH), ("v", A_WIDTH), ("qkvb", 3 * B_WIDTH), ("gate", B_WIDTH),
                  ("uc", C_WIDTH), ("ba", LANES)):
    _SEG[_name] = (_off, _off + _w)
    _off += _w
PROJ_WIDTH = _off


def _bdot(a, b):
    return jnp.dot(a.astype(BF16), b.astype(BF16), preferred_element_type=F32)


def _bdot_nt(a, b):
    return lax.dot_general(a.astype(BF16), b.astype(BF16), (((1,), (1,)), ((), ())), preferred_element_type=F32)


def _split3(a):
    hi = a.astype(BF16)
    r = a - hi.astype(F32)
    mid = r.astype(BF16)
    lo = (r - mid.astype(F32)).astype(BF16)
    return hi, mid, lo


def _dot_f32_by_exact(a, e):
    lhs = jnp.concatenate(_split3(a), axis=1)
    return jnp.dot(lhs, jnp.concatenate([e, e, e], axis=0), preferred_element_type=F32)


def _dot_exact_by_f32(e, a):
    rhs = jnp.concatenate(_split3(a), axis=0)
    return jnp.dot(jnp.concatenate([e, e, e], axis=1), rhs, preferred_element_type=F32)


def _blk(idx, size):
    return jnp.bitwise_and(idx, -size)


def _layer_norm(x, g, b):
    mu = jnp.mean(x, -1, keepdims=True)
    xc = x - mu
    var = jnp.mean(xc * xc, -1, keepdims=True)
    return xc * lax.rsqrt(var + LN_EPS) * g + b


def _silu(x):
    return x * jax.nn.sigmoid(x)


def _softplus(x):
    return jnp.maximum(x, 0.0) + jnp.log1p(jnp.exp(-jnp.abs(x)))


def _gelu_tanh(x):
    c = float(np.sqrt(2.0 / np.pi))
    return x * (0.5 * (1.0 + jnp.tanh(c * (x + 0.044715 * (x * x * x)))))


class _LayerParam(NamedTuple):
    array: jax.Array
    layer: int


def _const_spec(p):
    rest = p.array.shape[1:]
    return pl.BlockSpec((None,) + rest, lambda *_: (p.layer,) + (0,) * len(rest), pipeline_mode=pl.Buffered(1))


def _params(semantics):
    return pltpu.CompilerParams(dimension_semantics=semantics, vmem_limit_bytes=VMEM_LIMIT_BYTES)


_PROJ_ORDER = ("q", "k", "v", "qkvb", "gate", "uc", "ba")


def _proj_kernel(x_ref, w_ref, *out_refs, tiles_per_stream):
    xb = x_ref[...].astype(BF16)
    vals = {}
    for name, ref in zip(_PROJ_ORDER, out_refs):
        lo, hi = _SEG[name]
        vals[name] = jnp.dot(xb, w_ref[:, lo:hi], preferred_element_type=F32)
        ref[...] = vals[name].astype(ref.dtype)
    if len(out_refs) > len(_PROJ_ORDER):
        kt_ref, vt_ref = out_refs[len(_PROJ_ORDER):]

        @pl.when(pl.program_id(0) % tiles_per_stream == tiles_per_stream - 1)
        def _():
            kt_ref[...] = vals["k"].T
            vt_ref[...] = vals["v"].T


def _proj(x2d, w, n_batch, prompt):
    n_rows = x2d.shape[0]
    tm = min(ROW_TILE, n_rows)
    n_steps = n_rows // tm
    per_b = n_rows // n_batch // tm if prompt else 1
    widths = {name: hi - lo for name, (lo, hi) in _SEG.items()}
    out_shape, out_specs = [], []
    for name in _PROJ_ORDER:
        wd = widths[name]
        out_shape.append(jax.ShapeDtypeStruct((n_rows, wd), BF16 if prompt and name in ("k", "v") else F32))
        out_specs.append(pl.BlockSpec((tm, wd), lambda i: (i, 0)))
    if prompt:
        assert tm == A_WINDOW
        for _ in range(2):
            out_shape.append(jax.ShapeDtypeStruct((n_batch, A_WIDTH, A_WINDOW), F32))
            out_specs.append(pl.BlockSpec((None, A_WIDTH, A_WINDOW), lambda i: (i // per_b, 0, 0)))
    return pl.pallas_call(
        functools.partial(_proj_kernel, tiles_per_stream=per_b),
        grid=(n_steps,),
        in_specs=[pl.BlockSpec((tm, D_MODEL), lambda i: (i, 0)), _const_spec(w)],
        out_specs=out_specs,
        out_shape=out_shape,
        compiler_params=_params(("arbitrary",)),
        name="proj",
    )(x2d, w.array)


def _softmax_pv(q2s, key_segs, val_segs, bias_segs, valid_segs, feature_major=None):
    feature_major = feature_major or [False] * len(valid_segs)
    nn = (((1,), (0,)), ((), ()))
    nt = (((1,), (1,)), ((), ()))
    m_rows = q2s[0].shape[0]
    low = lax.broadcasted_iota(jnp.int32, q2s[0].shape, 1) < A_HEAD_DIM
    qs = [jnp.concatenate([jnp.where(low, q2, 0.0), jnp.where(low, 0.0, q2)], axis=0).astype(BF16) for q2 in q2s]
    scores = []
    for q, ksegs, biases in zip(qs, key_segs, bias_segs):
        sc = []
        for kseg, bias, valid, fm in zip(ksegs, biases, valid_segs, feature_major):
            s = lax.dot_general(q, kseg, nn if fm else nt, preferred_element_type=F32) + bias
            if valid is not None:
                s = jnp.where(valid, s, NEG_INF)
            sc.append(s)
        scores.append(sc)
    tops = [functools.reduce(jnp.maximum, [jnp.max(s, -1, keepdims=True) for s in sc]) for sc in scores]
    probs = [[jnp.exp(s - m) for s in sc] for sc, m in zip(scores, tops)]
    denoms = [functools.reduce(jnp.add, [jnp.sum(p, -1, keepdims=True) for p in pr]) for pr in probs]
    pvs = [functools.reduce(jnp.add, [lax.dot_general(p.astype(BF16), vseg, nt if fm else nn,
                                                      preferred_element_type=F32)
                                      for p, vseg, fm in zip(pr, vsegs, feature_major)])
           for pr, vsegs in zip(probs, val_segs)]
    outs = []
    for pv, denom in zip(pvs, denoms):
        pvn = pv * (1.0 / denom)
        outs.append(jnp.where(low, pvn[:m_rows], pvn[m_rows:]))
    return outs


def _attn_prompt_kernel(q_ref, kp_ref, kc_ref, vp_ref, vc_ref, bias_ref, o_ref):
    jb = pl.program_id(1)
    scale = A_HEAD_DIM ** -0.5
    pairs = range(A_HEADS // 2)
    cols = [slice(hp * LANES, (hp + 1) * LANES) for hp in pairs]

    def band(prev_ref, cur_ref, r0, c):
        return jnp.concatenate([prev_ref[r0:, c], cur_ref[:r0 + ATT_Q_SUB, c]], axis=0)

    def body(first_block):
        for sub in range(ATT_Q_BLOCK // ATT_Q_SUB):
            r0 = sub * ATT_Q_SUB
            valid = None
            if first_block:
                valid = lax.broadcasted_iota(jnp.int32, (1, ATT_K_SUB), 1) >= ATT_Q_BLOCK - r0
            outs = _softmax_pv([q_ref[r0:r0 + ATT_Q_SUB, c] * scale for c in cols],
                               [[band(kp_ref, kc_ref, r0, c)] for c in cols],
                               [[band(vp_ref, vc_ref, r0, c)] for c in cols],
                               [[bias_ref[hp]] for hp in pairs], [valid])
            for c, out in zip(cols, outs):
                o_ref[r0:r0 + ATT_Q_SUB, c] = out

    pl.when(jb == 0)(lambda: body(True))
    pl.when(jb > 0)(lambda: body(False))


def _attn_prompt(q, k, v, bias, n_batch):
    n_rows = q.shape[0]
    per_b = n_rows // n_batch // ATT_Q_BLOCK
    blk = (ATT_Q_BLOCK, A_WIDTH)
    cur = pl.BlockSpec(blk, lambda b, j: (b * per_b + j, 0))
    prev = pl.BlockSpec(blk, lambda b, j: (b * per_b + jnp.maximum(j - 1, 0), 0))
    return pl.pallas_call(
        _attn_prompt_kernel,
        grid=(n_batch, per_b),
        in_specs=[cur, prev, cur, prev, cur, _const_spec(bias)],
        out_specs=cur,
        out_shape=jax.ShapeDtypeStruct((n_rows, A_WIDTH), F32),
        compiler_params=_params(("parallel", "arbitrary")),
        name="attn_prompt",
    )(q, k, k, v, v, bias.array)


def _attn_sample_kernel(q_ref, kn_ref, vn_ref, kc_ref, vc_ref, bias_c_ref, bias_n_ref, o_ref):
    scale = A_HEAD_DIM ** -0.5
    pairs = range(A_HEADS // 2)
    cols = [slice(hp * LANES, (hp + 1) * LANES) for hp in pairs]
    outs = _softmax_pv([q_ref[:, c] * scale for c in cols],
                       [[kc_ref[c, :].astype(BF16), kn_ref[:, c].astype(BF16)] for c in cols],
                       [[vc_ref[c, :].astype(BF16), vn_ref[:, c].astype(BF16)] for c in cols],
                       [[bias_c_ref[hp], bias_n_ref[hp]] for hp in pairs], [None, None], [True, False])
    for c, out in zip(cols, outs):
        o_ref[:, c] = out


def _attn_sample(q, k, v, k_cache, v_cache, layer, bias_c, bias_n, n_batch):
    n_rows = q.shape[0]
    s_len = n_rows // n_batch
    n_cache = k_cache.shape[3]
    new = pl.BlockSpec((s_len, A_WIDTH), lambda b: (b, 0))
    cache = pl.BlockSpec((None, None, A_WIDTH, n_cache), lambda b: (layer, b, 0, 0))
    return pl.pallas_call(
        _attn_sample_kernel,
        grid=(n_batch,),
        in_specs=[new, new, new, cache, cache, _const_spec(bias_c), _const_spec(bias_n)],
        out_specs=new,
        out_shape=jax.ShapeDtypeStruct((n_rows, A_WIDTH), F32),
        compiler_params=_params(("parallel",)),
        name="attn_sample",
    )(q, k, v, k_cache, v_cache, bias_c.array, bias_n.array)


def _rel_bias_blocks(table, n_q, n_k, q_offset, banded):
    period = n_q + n_k - 1
    rel = q_offset - ((np.arange(period) + n_q - 1) % period - (n_q - 1))
    vec = jnp.take(table.astype(F32), jnp.asarray(np.clip(rel, -MAX_REL, MAX_REL) + MAX_REL), axis=1)
    n_heads = table.shape[0]
    bias = jnp.tile(vec, (1, n_q))[:, :n_q * (period - 1)].reshape(n_heads, n_q, period - 1)[:, :, :n_k]
    if banded:
        i = np.arange(n_q)[:, None]
        jj = np.arange(n_k)[None, :] - (i // CHUNK) * CHUNK
        inband = (jj >= 0) & (jj < A_WINDOW + CHUNK)
        bias = jnp.where(jnp.asarray(inband), bias, NEG_INF)
    return bias.reshape(n_heads // 2, 2 * n_q, n_k)


def _delta_kernel(qkvb_ref, gate_ref, ba_ref, cst_ref, s0_ref, cw_ref, cb_ref, alog_ref, dtb_ref, nw_ref,
                  o_ref, cout_ref, sout_ref, xbuf, s_ref, ob_ref, *, t_valid, n_rows):
    j = pl.program_id(1)
    n_blk = n_rows // PAIR
    hd = B_HEAD_DIM
    low = lax.broadcasted_iota(jnp.int32, (PAIR, LANES), 1) < hd
    low_s = lax.broadcasted_iota(jnp.int32, (hd, LANES), 1) < hd
    zeros_s = jnp.zeros((hd, LANES), F32)

    @pl.when(j == 0)
    def _():
        xbuf[0:SUBLANES, :] = cst_ref[0]
        for h in range(B_HEADS):
            s0 = s0_ref[0, h]
            s_ref[h] = jnp.concatenate([s0, jnp.zeros_like(s0)] if h % 2 == 0 else [jnp.zeros_like(s0), s0], axis=1)

    xbuf[SUBLANES:SUBLANES + n_rows, :] = qkvb_ref[...]
    t_last = n_rows - PAIR + t_valid
    cout_ref[0] = xbuf[t_last:t_last + SUBLANES, :]

    hrow = lax.broadcasted_iota(jnp.int32, (B_WIDTH, B_WIDTH), 0)
    hcol = lax.broadcasted_iota(jnp.int32, (B_WIDTH, B_WIDTH), 1)
    head_ones = jnp.where(_blk(hrow, B_HEAD_DIM) == _blk(hcol, B_HEAD_DIM), 1.0, 0.0).astype(BF16)

    def l2n(t):
        return t * lax.rsqrt(_dot_f32_by_exact(t * t, head_ones) + RMS_EPS)

    row = lax.broadcasted_iota(jnp.int32, (PAIR, PAIR), 0)
    col = lax.broadcasted_iota(jnp.int32, (PAIR, PAIR), 1)
    same = _blk(row, CHUNK) == _blk(col, CHUNK)
    tril = same & (row >= col)
    strict = same & (row > col)
    first = row < CHUNK
    ltri = jnp.where(tril, 1.0, 0.0).astype(BF16)
    n_pairs = B_HEADS // 2
    chunks = range(PAIR // CHUNK)

    def pair_lanes(t, p):
        return jnp.where(low, t[:, (2 * p) * LANES:(2 * p + 1) * LANES], t[:, (2 * p + 1) * LANES:(2 * p + 2) * LANES])

    def own(h, t, other):
        return jnp.where(low, t, other) if h % 2 == 0 else jnp.where(low, other, t)

    prepared, solved = {}, {}

    def prepare(n):
        r0 = n * PAIR
        conv = cb_ref[...]
        for w in range(CONV_W):
            start = SUBLANES - (CONV_W - 1) + w + r0
            conv = conv + xbuf[start:start + PAIR, :] * cw_ref[w:w + 1, :]
        conv = _silu(conv)
        yield
        qn = l2n(conv[:, 0:B_WIDTH]) * (B_HEAD_DIM ** -0.5)
        yield
        kn = l2n(conv[:, B_WIDTH:2 * B_WIDTH])
        vv = conv[:, 2 * B_WIDTH:3 * B_WIDTH]
        yield
        ba = ba_ref[r0:r0 + PAIR, :]
        spread = lambda c: jnp.broadcast_to(ba[:, c:c + 1], (PAIR, LANES))
        beta = jax.nn.sigmoid(jnp.concatenate(
            [jnp.where(low, spread(2 * p), spread(2 * p + 1)) for p in range(n_pairs)], axis=1))
        a_in = jnp.concatenate([spread(B_HEADS + h) for h in range(B_HEADS)], axis=1)
        g = -jnp.exp(alog_ref[...]) * _softplus(a_in + dtb_ref[...])
        if t_valid < PAIR and n == n_blk - 1:
            def live(t):
                return jnp.where(lax.broadcasted_iota(jnp.int32, t.shape, 0) < t_valid, t, 0.0)
            kn, vv, beta, g = live(kn), live(vv), live(beta), live(g)
        yield
        gc = _dot_exact_by_f32(ltri, g)
        g_last = [gc[(c + 1) * CHUNK - 1:(c + 1) * CHUNK, :] for c in chunks]
        gl = jnp.concatenate([jnp.broadcast_to(t, (CHUNK, t.shape[1])) for t in g_last], axis=0)
        egc = jnp.exp(gc)
        erest = jnp.exp(gl - gc)
        d = {"egl": [jnp.exp(t) for t in g_last], "k": [], "q": [], "kb": [], "vb": [], "kbg_r": [], "qd_r": [],
             "kd": [], "decay": [], "decay_strict": []}
        yield
        for p in range(n_pairs):
            ls = slice(p * LANES, (p + 1) * LANES)
            k, q, v, b = kn[:, ls], qn[:, ls], vv[:, ls], beta[:, ls]
            eg = pair_lanes(egc, p)
            d["k"].append(k)
            d["q"].append(q)
            d["kb"].append(k * b)
            d["vb"].append(v * b)
            d["kbg_r"].append(pltpu.roll(k * b * eg, hd, axis=1))
            d["qd_r"].append(pltpu.roll(q * eg, hd, axis=1))
            d["kd"].append(k * pair_lanes(erest, p))
            yield
        for h in range(B_HEADS):
            gcol = gc[:, h * LANES:(h + 1) * LANES]
            e = jnp.exp(gcol - gcol.T)
            d["decay"].append(jnp.where(tril, e, 0.0))
            d["decay_strict"].append(jnp.where(strict, e, 0.0))
            yield
        prepared[n] = d
    heads = range(B_HEADS)

    def solve(n):
        d = prepared[n]
        a_low = [_bdot_nt(own(h, d["kb"][h // 2], 0.0), d["k"][h // 2]) * d["decay_strict"][h] for h in heads]
        yield
        base = 8
        diag = _blk(row, base) == _blk(col, base)
        eye = jnp.where(row == col, 1.0, 0.0)
        a0 = [jnp.where(diag, a, 0.0) for a in a_low]
        xs = [eye - a for a in a0]
        ps = [_bdot(a, a) for a in a0]
        yield
        xs = [x + _bdot(x, p) for x, p in zip(xs, ps)]
        ps = [_bdot(p, p) for p in ps]
        yield
        xs = [x + _bdot(x, p) for x, p in zip(xs, ps)]
        yield
        bs = base
        while bs < CHUNK:
            off = (_blk(row, 2 * bs) == _blk(col, 2 * bs)) & (_blk(row, bs) != _blk(col, bs))
            ts = [_bdot(jnp.where(off, a, 0.0), x) for a, x in zip(a_low, xs)]
            yield
            xs = [x - _bdot(x, t) for x, t in zip(xs, ts)]
            yield
            bs *= 2
        uws = [_bdot(x, own(h, d["vb"][h // 2], d["kbg_r"][h // 2])) for h, x in zip(heads, xs)]
        yield
        qus = []
        for h in heads:
            qk = _bdot_nt(own(h, d["q"][h // 2], 0.0), d["k"][h // 2]) * d["decay"][h]
            qus.append(_bdot(qk, uws[h]))
        yield
        kus = []
        for p in range(n_pairs):
            rhs = jnp.concatenate([jnp.where(first if c == 0 else ~first, uws[2 * p + e], 0.0)
                                   for e in range(2) for c in chunks], axis=1)
            kus.append(_bdot(d["kd"][p].T, rhs))
            yield
        solved[n] = (qus, kus)

    states = [s_ref[h] for h in heads]

    def advance(n):
        d = prepared[n]
        qus, kus = solved[n]
        for c in chunks:
            outs = []
            for h in heads:
                p, e = divmod(h, 2)
                ku = kus[p][e * hd:(e + 1) * hd, (2 * e + c) * LANES:(2 * e + c + 1) * LANES]
                qu = qus[h][c * CHUNK:(c + 1) * CHUNK, :]
                kq = jnp.concatenate([ku, d["qd_r"][p][c * CHUNK:(c + 1) * CHUNK] - qu], axis=0)
                s_ext = jnp.concatenate([zeros_s, states[h]] if e == 0 else [states[h], zeros_s], axis=0)
                r = _bdot(kq, s_ext)
                gt = d["egl"][c][:, h * LANES:(h + 1) * LANES]
                new = gt * states[h] - r[:hd] + ku
                states[h] = jnp.where(low_s, new, 0.0) if e == 0 else jnp.where(low_s, 0.0, new)
                outs.append(r[hd:] + qu)
            r0 = n * PAIR + c * CHUNK
            for p in range(n_pairs):
                ob_ref[r0:r0 + CHUNK, p * LANES:(p + 1) * LANES] = jnp.where(low_s, outs[2 * p], outs[2 * p + 1])
            yield
        rs = slice(n * PAIR, (n + 1) * PAIR)
        ob = ob_ref[rs, :]
        ms = _dot_f32_by_exact(ob * ob, head_ones) * (1.0 / hd)
        o_ref[rs, :] = ob * lax.rsqrt(ms + RMS_EPS) * nw_ref[...] * _silu(gate_ref[rs, :])
        yield

    def advance_all(blocks):
        for n in blocks:
            yield from advance(n)

    group = 2 if n_blk % 2 == 0 else 1
    n_groups = n_blk // group
    for slot in range(n_groups + 2):
        active = []
        if 0 <= slot - 2 < n_groups:
            active.append(advance_all(range((slot - 2) * group, (slot - 1) * group)))
        if 0 <= slot - 1 < n_groups:
            active += [solve(n) for n in range((slot - 1) * group, slot * group)]
        if slot < n_groups:
            active += [prepare(n) for n in range(slot * group, (slot + 1) * group)]
        while active:
            for gen in list(active):
                if next(gen, "done") == "done":
                    active.remove(gen)

    for h in heads:
        s_ref[h] = states[h]
    xbuf[0:SUBLANES, :] = xbuf[n_rows:n_rows + SUBLANES, :]

    @pl.when(j == pl.num_programs(1) - 1)
    def _():
        for h in range(B_HEADS):
            sout_ref[0, h] = s_ref[h][:, (h % 2) * hd:(h % 2 + 1) * hd]


def _delta(qkvb, gate, ba, conv_state, s0, lw, n_batch, t_valid, rows_per_step):
    n_rows = qkvb.shape[0]
    per_b = n_rows // n_batch // rows_per_step

    def rows(width):
        return pl.BlockSpec((rows_per_step, width), lambda b, j: (b * per_b + j, 0))

    cst = pl.BlockSpec((1, SUBLANES, 3 * B_WIDTH), lambda b, j: (b, 0, 0))
    st = pl.BlockSpec((1, B_HEADS, B_HEAD_DIM, B_HEAD_DIM), lambda b, j: (b, 0, 0, 0))
    consts = [lw["conv_w"], lw["conv_b"], lw["a_log"], lw["dt_bias"], lw["norm_w"]]
    return pl.pallas_call(
        functools.partial(_delta_kernel, t_valid=t_valid, n_rows=rows_per_step),
        grid=(n_batch, per_b),
        in_specs=[rows(3 * B_WIDTH), rows(B_WIDTH), rows(LANES), cst, st]
        + [_const_spec(c) for c in consts],
        out_specs=[rows(B_WIDTH), cst, st],
        out_shape=[jax.ShapeDtypeStruct((n_rows, B_WIDTH), F32),
                   jax.ShapeDtypeStruct((n_batch, SUBLANES, 3 * B_WIDTH), F32),
                   jax.ShapeDtypeStruct((n_batch, B_HEADS, B_HEAD_DIM, B_HEAD_DIM), F32)],
        scratch_shapes=[pltpu.VMEM((SUBLANES + rows_per_step, 3 * B_WIDTH), F32),
                        pltpu.VMEM((B_HEADS, B_HEAD_DIM, LANES), F32),
                        pltpu.VMEM((rows_per_step, B_WIDTH), F32)],
        compiler_params=_params(("parallel", "arbitrary")),
        name="delta",
    )(qkvb, gate, ba, conv_state, s0, *[c.array for c in consts])


def _s5_kernel(u_ref, h0r_ref, h0i_ref, lr_ref, li_ref, bre_ref, bim_ref, cre_ref, cim_ref, d_ref, gw_ref, gb_ref,
               o_ref, hro_ref, hio_ref, xr, xi, hs, *, t_block):
    tb = pl.program_id(1)

    @pl.when(tb == 0)
    def _():
        hs[0] = h0r_ref[0]
        hs[1] = h0i_ref[0]

    u = jnp.swapaxes(u_ref[0], 0, 1).reshape(t_block * S5_BATCH, C_WIDTH)
    ub = u.astype(BF16)
    xr[...] = jnp.dot(ub, bre_ref[...], preferred_element_type=F32)
    xi[...] = jnp.dot(ub, bim_ref[...], preferred_element_type=F32)
    lam_r = jnp.broadcast_to(lr_ref[...], (S5_BATCH, C_LANES))
    lam_i = jnp.broadcast_to(li_ref[...], (S5_BATCH, C_LANES))

    def step(t, carry):
        h_r, h_i = carry
        rows = pl.ds(pl.multiple_of(t * S5_BATCH, S5_BATCH), S5_BATCH)
        n_r = lam_r * h_r - lam_i * h_i + xr[rows, :]
        n_i = lam_r * h_i + lam_i * h_r + xi[rows, :]
        xr[rows, :] = n_r
        xi[rows, :] = n_i
        return n_r, n_i

    h_r, h_i = lax.fori_loop(0, t_block, step, (hs[0], hs[1]), unroll=4)
    hs[0] = h_r
    hs[1] = h_i
    hro_ref[0] = h_r
    hio_ref[0] = h_i
    y = (jnp.dot(xr[...].astype(BF16), cre_ref[...], preferred_element_type=F32)
         - jnp.dot(xi[...].astype(BF16), cim_ref[...], preferred_element_type=F32) + d_ref[...] * u)
    z = _gelu_tanh(y)
    out = z * jax.nn.sigmoid(jnp.dot(z.astype(BF16), gw_ref[...], preferred_element_type=F32) + gb_ref[...])
    o_ref[0] = jnp.swapaxes(out.reshape(t_block, S5_BATCH, C_WIDTH), 0, 1)


def _s5(u, h0_re, h0_im, lw, t_block):
    n_groups, _, t_len, _ = u.shape
    rows_blk = t_block * S5_BATCH
    n_tb = t_len // t_block
    u_spec = pl.BlockSpec((1, S5_BATCH, t_block, C_WIDTH), lambda g, t: (g, 0, t, 0))
    h_spec = pl.BlockSpec((1, S5_BATCH, C_LANES), lambda g, t: (g, 0, 0))
    consts = [lw["lam_re"], lw["lam_im"], lw["b_re"], lw["b_im"], lw["c_re"], lw["c_im"], lw["c_d"],
              lw["glu_w"], lw["glu_b"]]
    return pl.pallas_call(
        functools.partial(_s5_kernel, t_block=t_block),
        grid=(n_groups, n_tb),
        in_specs=[u_spec, h_spec, h_spec] + [_const_spec(c) for c in consts],
        out_specs=[u_spec, h_spec, h_spec],
        out_shape=[jax.ShapeDtypeStruct(u.shape, F32),
                   jax.ShapeDtypeStruct(h0_re.shape, F32), jax.ShapeDtypeStruct(h0_im.shape, F32)],
        scratch_shapes=[pltpu.VMEM((rows_blk, C_LANES), F32), pltpu.VMEM((rows_blk, C_LANES), F32),
                        pltpu.VMEM((2, S5_BATCH, C_LANES), F32)],
        compiler_params=_params(("parallel", "arbitrary")),
        name="s5",
    )(u, h0_re, h0_im, *[c.array for c in consts])


def _post_kernel(x_ref, a_ref, b_ref, c_ref, wo_ref, g1_ref, b1_ref, wu_ref, bu_ref, wd_ref, g2_ref, b2_ref, o_ref):
    mix = (jnp.dot(a_ref[...].astype(BF16), wo_ref[0:A_WIDTH, :], preferred_element_type=F32)
           + jnp.dot(b_ref[...].astype(BF16), wo_ref[A_WIDTH:A_WIDTH + B_WIDTH, :], preferred_element_type=F32)
           + jnp.dot(c_ref[...].astype(BF16), wo_ref[A_WIDTH + B_WIDTH:, :], preferred_element_type=F32))
    x1 = _layer_norm(DN_ALPHA * x_ref[...] + mix, g1_ref[...], b1_ref[...])
    x1b = x1.astype(BF16)
    acc = jnp.zeros(x1.shape, F32)
    for s in range(D_FF // FF_SLAB):
        cs = slice(s * FF_SLAB, (s + 1) * FF_SLAB)
        hid = jnp.dot(x1b, wu_ref[:, cs], preferred_element_type=F32) + bu_ref[:, cs]
        hid = jnp.square(jnp.maximum(hid, 0.0))
        acc = acc + jnp.dot(hid.astype(BF16), wd_ref[cs, :], preferred_element_type=F32)
    o_ref[...] = _layer_norm(DN_ALPHA * x1 + acc, g2_ref[...], b2_ref[...])


def _post(x2d, out_a, out_b, out_c, lw):
    n_rows = x2d.shape[0]
    tm = min(ROW_TILE, n_rows)

    def rows(width):
        return pl.BlockSpec((tm, width), lambda i: (i, 0))

    c_spec = rows(C_WIDTH)
    consts = [lw["w_out"], lw["ln1_g"], lw["ln1_b"], lw["w_up"], lw["b_up"], lw["w_down"], lw["ln2_g"], lw["ln2_b"]]
    return pl.pallas_call(
        _post_kernel,
        grid=(n_rows // tm,),
        in_specs=[rows(D_MODEL), rows(A_WIDTH), rows(B_WIDTH), c_spec] + [_const_spec(c) for c in consts],
        out_specs=rows(D_MODEL),
        out_shape=jax.ShapeDtypeStruct((n_rows, D_MODEL), F32),
        compiler_params=_params(("parallel",)),
        name="post",
    )(x2d, out_a, out_b, out_c, *[c.array for c in consts])


def _stacked_weights(w_in, a_rel_bias, b_conv_w, b_conv_b, b_a_log, b_dt_bias, b_norm_w, c_a_re, c_a_im, c_log_dt,
                     c_b_re, c_b_im, c_c_re, c_c_im, c_d, c_glu_w, c_glu_b, w_out, ln1_g, ln1_b, w_up, b_up, w_down,
                     ln2_g, ln2_b, s_len, n_cache):
    n_layers = w_in.shape[0]
    wi = w_in
    o_q, o_k, o_v = 0, A_WIDTH, 2 * A_WIDTH
    o_qkvb = 3 * A_WIDTH
    o_beta = o_qkvb + 3 * B_WIDTH
    o_a = o_beta + B_HEADS
    o_gate = o_a + B_HEADS
    o_uc = o_gate + B_WIDTH
    per_group = lambda cols: jnp.repeat(cols, LANES, axis=-1)
    assert o_a == o_beta + B_HEADS
    w_proj = jnp.concatenate([
        wi[..., o_q:o_q + A_WIDTH], wi[..., o_k:o_k + A_WIDTH], wi[..., o_v:o_v + A_WIDTH],
        wi[..., o_qkvb:o_qkvb + 3 * B_WIDTH], wi[..., o_gate:o_gate + B_WIDTH], wi[..., o_uc:o_uc + C_WIDTH],
        wi[..., o_beta:o_beta + 2 * B_HEADS],
        jnp.zeros((n_layers, D_MODEL, LANES - 2 * B_HEADS), wi.dtype)], axis=-1).astype(BF16)

    def bias_blocks(n_q, n_k, q_offset, banded):
        flat = _rel_bias_blocks(a_rel_bias.reshape(n_layers * A_HEADS, -1), n_q, n_k, q_offset, banded)
        return flat.reshape(n_layers, A_HEADS // 2, 2 * n_q, n_k)

    bias_sample = bias_blocks(s_len, n_cache + s_len, n_cache, banded=False)

    a_re, a_im = c_a_re.astype(F32), c_a_im.astype(F32)
    dt = jnp.exp(c_log_dt.astype(F32))[..., None]
    mag = jnp.exp(dt * a_re)
    lam_re, lam_im = mag * jnp.cos(dt * a_im), mag * jnp.sin(dt * a_im)
    den = a_re * a_re + a_im * a_im
    coef_re = ((lam_re - 1.0) * a_re + lam_im * a_im) / den
    coef_im = (lam_im * a_re - (lam_re - 1.0) * a_im) / den
    bre, bim = c_b_re.astype(F32), c_b_im.astype(F32)
    bb_re = coef_re[..., None] * bre - coef_im[..., None] * bim
    bb_im = coef_re[..., None] * bim + coef_im[..., None] * bre
    eye = jnp.eye(C_GROUPS, dtype=F32)
    in_bd = lambda t: jnp.einsum("lgph,gk->lghkp", t, eye).reshape(n_layers, C_WIDTH, C_LANES).astype(BF16)
    out_bd = lambda t: jnp.einsum("lghp,gk->lgpkh", t.astype(F32), eye).reshape(n_layers, C_LANES, C_WIDTH).astype(BF16)
    row = lambda t: t.astype(F32).reshape(n_layers, 1, -1)

    return {
        "w_proj": w_proj,
        "bias_prompt": bias_blocks(ATT_Q_SUB, ATT_K_SUB, A_WINDOW, banded=True),
        "bias_cache": bias_sample[..., :n_cache], "bias_new": bias_sample[..., n_cache:],
        "conv_w": b_conv_w.astype(F32), "conv_b": row(b_conv_b),
        "a_log": row(per_group(b_a_log)), "dt_bias": row(per_group(b_dt_bias)),
        "norm_w": row(jnp.tile(b_norm_w, (1, B_HEADS))),
        "lam_re": row(lam_re), "lam_im": row(lam_im), "b_re": in_bd(bb_re), "b_im": in_bd(bb_im),
        "c_re": out_bd(c_c_re), "c_im": out_bd(c_c_im), "c_d": row(c_d),
        "glu_w": c_glu_w.astype(BF16), "glu_b": row(c_glu_b),
        "w_out": w_out.astype(BF16), "ln1_g": row(ln1_g), "ln1_b": row(ln1_b),
        "w_up": w_up.astype(BF16), "b_up": row(b_up), "w_down": w_down.astype(BF16),
        "ln2_g": row(ln2_g), "ln2_b": row(ln2_b),
    }


def _pad_conv_state(conv_buf):
    return jnp.pad(conv_buf.astype(F32), ((0, 0), (SUBLANES - (CONV_W - 1), 0), (0, 0)))


def _prompt_layer(x2d, lw, n_batch):
    t_len = x2d.shape[0] // n_batch
    q, k, v, qkvb, gate, uc, ba, k_t, v_t = _proj(x2d, lw["w_proj"], n_batch, prompt=True)
    out_a = _attn_prompt(q, k, v, lw["bias_prompt"], n_batch)
    out_b, conv_o, s_new = _delta(
        qkvb, gate, ba, jnp.zeros((n_batch, SUBLANES, 3 * B_WIDTH), F32),
        jnp.zeros((n_batch, B_HEADS, B_HEAD_DIM, B_HEAD_DIM), F32), lw, n_batch, t_valid=PAIR, rows_per_step=DELTA_ROWS)
    zeros_h = jnp.zeros((1, S5_BATCH, C_LANES), F32)
    out_c, h_re, h_im = _s5(uc.reshape(1, n_batch, t_len, C_WIDTH), zeros_h, zeros_h, lw, S5_TIME_BLOCK)
    y = _post(x2d, out_a, out_b, out_c.reshape(n_batch * t_len, C_WIDTH), lw)
    heads = lambda t: t.reshape(n_batch, A_HEADS, A_HEAD_DIM, A_WINDOW).transpose(0, 3, 1, 2)
    state = lambda t: t.reshape(n_batch, C_GROUPS, C_STATE)
    return y, (heads(k_t), heads(v_t), conv_o[:, -(CONV_W - 1):], s_new, state(h_re), state(h_im))


def _sample_layer(x2d, lw, n_batch, layer, k_cache, v_cache, conv_buf, s0, h0_re, h0_im):
    s_len = x2d.shape[0] // n_batch
    n_groups = n_batch // S5_BATCH
    q, k, v, qkvb, gate, uc, ba = _proj(x2d, lw["w_proj"], n_batch, prompt=False)
    out_a = _attn_sample(q, k, v, k_cache, v_cache, layer, lw["bias_cache"], lw["bias_new"], n_batch)

    def pad_rows(t):
        t = t.reshape(n_batch, s_len, -1)
        return jnp.pad(t, ((0, 0), (0, PAIR - s_len), (0, 0))).reshape(n_batch * PAIR, -1)

    out_b, conv_o, s_new = _delta(pad_rows(qkvb), pad_rows(gate), pad_rows(ba),
                                  _pad_conv_state(conv_buf), s0.astype(F32), lw, n_batch, t_valid=s_len, rows_per_step=PAIR)
    out_b = out_b.reshape(n_batch, PAIR, B_WIDTH)[:, :s_len].reshape(n_batch * s_len, B_WIDTH)

    grp = lambda t: t.astype(F32).reshape(n_groups, S5_BATCH, C_LANES)
    out_c, h_re, h_im = _s5(uc.reshape(n_groups, S5_BATCH, s_len, C_WIDTH), grp(h0_re), grp(h0_im), lw, s_len)
    y = _post(x2d, out_a, out_b, out_c.reshape(n_batch * s_len, C_WIDTH), lw)
    heads = lambda t: t.reshape(n_batch, s_len, A_HEADS, A_HEAD_DIM)
    state = lambda t: t.reshape(n_batch, C_GROUPS, C_STATE)
    return y, (heads(k), heads(v), conv_o[:, -(CONV_W - 1):], s_new, state(h_re), state(h_im))


def kernel(x_prompt, x_sample, cache_a_k, cache_a_v, state_b_conv, state_b_ssm, state_c_re, state_c_im, w_in, a_rel_bias, b_conv_w, b_conv_b, b_a_log, b_dt_bias, b_norm_w, c_a_re, c_a_im, c_log_dt, c_b_re, c_b_im, c_c_re, c_c_im, c_d, c_glu_w, c_glu_b, w_out, ln1_g, ln1_b, w_up, b_up, w_down, ln2_g, ln2_b):
    n_p, t_p, _ = x_prompt.shape
    n_s, t_s, _ = x_sample.shape
    n_cache = cache_a_k.shape[2]
    yp = x_prompt.reshape(n_p * t_p, D_MODEL)
    ys = x_sample.reshape(n_s * t_s, D_MODEL)
    feat_major = lambda t: t.astype(F32).transpose(0, 1, 3, 4, 2).reshape(DEPTH, n_s, A_WIDTH, n_cache)
    k_cache, v_cache = feat_major(cache_a_k), feat_major(cache_a_v)
    stacked = _stacked_weights(w_in, a_rel_bias, b_conv_w, b_conv_b, b_a_log, b_dt_bias, b_norm_w, c_a_re, c_a_im,
                               c_log_dt, c_b_re, c_b_im, c_c_re, c_c_im, c_d, c_glu_w, c_glu_b, w_out, ln1_g, ln1_b,
                               w_up, b_up, w_down, ln2_g, ln2_b, t_s, n_cache)
    p_out, s_out = [], []
    for l in range(DEPTH):
        lw = {name: _LayerParam(arr, l) for name, arr in stacked.items()}
        yp, st_p = _prompt_layer(yp, lw, n_p)
        ys, st_s = _sample_layer(ys, lw, n_s, l, k_cache, v_cache, state_b_conv[l], state_b_ssm[l],
                                 state_c_re[l], state_c_im[l])
        p_out.append(st_p)
        s_out.append(st_s)
    stack = lambda outs, i: jnp.stack([o[i] for o in outs])
    return (yp.reshape(n_p, t_p, D_MODEL), ys.reshape(n_s, t_s, D_MODEL),
            *[stack(p_out, i) for i in range(6)], *[stack(s_out, i) for i in range(6)])
```

```python
import functools
from typing import NamedTuple

import jax
import jax.numpy as jnp
import numpy as np
from jax import lax
from jax.experimental import pallas as pl
from jax.experimental.pallas import tpu as pltpu

F32 = jnp.float32
BF16 = jnp.bfloat16

D_MODEL = 1024
DEPTH = 2
CHUNK = 64
A_HEADS = 8
A_HEAD_DIM = 64
A_WIDTH = A_HEADS * A_HEAD_DIM
A_WINDOW = 8 * CHUNK
MAX_REL = 256
B_HEADS = 4
B_HEAD_DIM = 64
B_WIDTH = B_HEADS * B_HEAD_DIM
CONV_W = 4
C_WIDTH = D_MODEL - A_WIDTH - B_WIDTH
C_GROUP = 16
C_GROUPS = C_WIDTH // C_GROUP
C_STATE = 64
C_LANES = C_GROUPS * C_STATE
D_FF = 4 * D_MODEL
DN_ALPHA = (2.0 * DEPTH) ** 0.25
LN_EPS = 1e-5
RMS_EPS = 1e-6
NEG_INF = -1e30

SUBLANES = 8
LANES = 128
VMEM_LIMIT_BYTES = 56 * 1024 * 1024

ROW_TILE = 512
ATT_Q_BLOCK = 512
ATT_Q_SUB = 2 * CHUNK
ATT_K_SUB = ATT_Q_SUB + A_WINDOW
PAIR = 2 * CHUNK
DELTA_ROWS = 8 * PAIR
S5_TIME_BLOCK = 256
S5_BATCH = SUBLANES
FF_SLAB = 1024

_SEG = {}
_off = 0
for _name, _w in (("q", A_WIDTH), ("k", A_WIDTH), ("v", A_WIDTH), ("qkvb", 3 * B_WIDTH), ("gate", B_WIDTH),
                  ("uc", C_WIDTH), ("ba", LANES)):
    _SEG[_name] = (_off, _off + _w)
    _off += _w
PROJ_WIDTH = _off


def _bdot(a, b):
    return jnp.dot(a.astype(BF16), b.astype(BF16), preferred_element_type=F32)


def _bdot_nt(a, b):
    return lax.dot_general(a.astype(BF16), b.astype(BF16), (((1,), (1,)), ((), ())), preferred_element_type=F32)


def _split3(a):
    hi = a.astype(BF16)
    r = a - hi.astype(F32)
    mid = r.astype(BF16)
    lo = (r - mid.astype(F32)).astype(BF16)
    return hi, mid, lo


def _dot_f32_by_exact(a, e):
    lhs = jnp.concatenate(_split3(a), axis=1)
    return jnp.dot(lhs, jnp.concatenate([e, e, e], axis=0), preferred_element_type=F32)


def _dot_exact_by_f32(e, a):
    rhs = jnp.concatenate(_split3(a), axis=0)
    return jnp.dot(jnp.concatenate([e, e, e], axis=1), rhs, preferred_element_type=F32)


def _blk(idx, size):
    return jnp.bitwise_and(idx, -size)


def _layer_norm(x, g, b):
    mu = jnp.mean(x, -1, keepdims=True)
    xc = x - mu
    var = jnp.mean(xc * xc, -1, keepdims=True)
    return xc * lax.rsqrt(var + LN_EPS) * g + b


def _silu(x):
    return x * jax.nn.sigmoid(x)


def _softplus(x):
    return jnp.maximum(x, 0.0) + jnp.log1p(jnp.exp(-jnp.abs(x)))


def _gelu_tanh(x):
    c = float(np.sqrt(2.0 / np.pi))
    return x * (0.5 * (1.0 + jnp.tanh(c * (x + 0.044715 * (x * x * x)))))


class _LayerParam(NamedTuple):
    array: jax.Array
    layer: int


def _const_spec(p):
    rest = p.array.shape[1:]
    return pl.BlockSpec((None,) + rest, lambda *_: (p.layer,) + (0,) * len(rest), pipeline_mode=pl.Buffered(1))


def _params(semantics):
    return pltpu.CompilerParams(dimension_semantics=semantics, vmem_limit_bytes=VMEM_LIMIT_BYTES)


_PROJ_ORDER = ("q", "k", "v", "qkvb", "gate", "uc", "ba")


def _proj_kernel(x_ref, w_ref, *out_refs, tiles_per_stream):
    xb = x_ref[...].astype(BF16)
    vals = {}
    for name, ref in zip(_PROJ_ORDER, out_refs):
        lo, hi = _SEG[name]
        vals[name] = jnp.dot(xb, w_ref[:, lo:hi], preferred_element_type=F32)
        ref[...] = vals[name].astype(ref.dtype)
    if len(out_refs) > len(_PROJ_ORDER):
        kt_ref, vt_ref = out_refs[len(_PROJ_ORDER):]

        @pl.when(pl.program_id(0) % tiles_per_stream == tiles_per_stream - 1)
        def _():
            kt_ref[...] = vals["k"].T
            vt_ref[...] = vals["v"].T


def _proj(x2d, w, n_batch, prompt):
    n_rows = x2d.shape[0]
    tm = min(ROW_TILE, n_rows)
    n_steps = n_rows // tm
    per_b = n_rows // n_batch // tm if prompt else 1
    widths = {name: hi - lo for name, (lo, hi) in _SEG.items()}
    out_shape, out_specs = [], []
    for name in _PROJ_ORDER:
        wd = widths[name]
        out_shape.append(jax.ShapeDtypeStruct((n_rows, wd), BF16 if prompt and name in ("k", "v") else F32))
        out_specs.append(pl.BlockSpec((tm, wd), lambda i: (i, 0)))
    if prompt:
        assert tm == A_WINDOW
        for _ in range(2):
            out_shape.append(jax.ShapeDtypeStruct((n_batch, A_WIDTH, A_WINDOW), F32))
            out_specs.append(pl.BlockSpec((None, A_WIDTH, A_WINDOW), lambda i: (i // per_b, 0, 0)))
    return pl.pallas_call(
        functools.partial(_proj_kernel, tiles_per_stream=per_b),
        grid=(n_steps,),
        in_specs=[pl.BlockSpec((tm, D_MODEL), lambda i: (i, 0)), _const_spec(w)],
        out_specs=out_specs,
        out_shape=out_shape,
        compiler_params=_params(("arbitrary",)),
        name="proj",
    )(x2d, w.array)


def _softmax_pv(q2s, key_segs, val_segs, bias_segs, valid_segs, feature_major=None):
    feature_major = feature_major or [False] * len(valid_segs)
    nn = (((1,), (0,)), ((), ()))
    nt = (((1,), (1,)), ((), ()))
    m_rows = q2s[0].shape[0]
    low = lax.broadcasted_iota(jnp.int32, q2s[0].shape, 1) < A_HEAD_DIM
    qs = [jnp.concatenate([jnp.where(low, q2, 0.0), jnp.where(low, 0.0, q2)], axis=0).astype(BF16) for q2 in q2s]
    scores = []
    for q, ksegs, biases in zip(qs, key_segs, bias_segs):
        sc = []
        for kseg, bias, valid, fm in zip(ksegs, biases, valid_segs, feature_major):
            s = lax.dot_general(q, kseg, nn if fm else nt, preferred_element_type=F32) + bias
            if valid is not None:
                s = jnp.where(valid, s, NEG_INF)
            sc.append(s)
        scores.append(sc)
    tops = [functools.reduce(jnp.maximum, [jnp.max(s, -1, keepdims=True) for s in sc]) for sc in scores]
    probs = [[jnp.exp(s - m) for s in sc] for sc, m in zip(scores, tops)]
    denoms = [functools.reduce(jnp.add, [jnp.sum(p, -1, keepdims=True) for p in pr]) for pr in probs]
    pvs = [functools.reduce(jnp.add, [lax.dot_general(p.astype(BF16), vseg, nt if fm else nn,
                                                      preferred_element_type=F32)
                                      for p, vseg, fm in zip(pr, vsegs, feature_major)])
           for pr, vsegs in zip(probs, val_segs)]
    outs = []
    for pv, denom in zip(pvs, denoms):
        pvn = pv * (1.0 / denom)
        outs.append(jnp.where(low, pvn[:m_rows], pvn[m_rows:]))
    return outs


def _attn_prompt_kernel(q_ref, kp_ref, kc_ref, vp_ref, vc_ref, bias_ref, o_ref):
    jb = pl.program_id(1)
    scale = A_HEAD_DIM ** -0.5
    pairs = range(A_HEADS // 2)
    cols = [slice(hp * LANES, (hp + 1) * LANES) for hp in pairs]

    def band(prev_ref, cur_ref, r0, c):
        return jnp.concatenate([prev_ref[r0:, c], cur_ref[:r0 + ATT_Q_SUB, c]], axis=0)

    def body(first_block):
        for sub in range(ATT_Q_BLOCK // ATT_Q_SUB):
            r0 = sub * ATT_Q_SUB
            valid = None
            if first_block:
                valid = lax.broadcasted_iota(jnp.int32, (1, ATT_K_SUB), 1) >= ATT_Q_BLOCK - r0
            outs = _softmax_pv([q_ref[r0:r0 + ATT_Q_SUB, c] * scale for c in cols],
                               [[band(kp_ref, kc_ref, r0, c)] for c in cols],
                               [[band(vp_ref, vc_ref, r0, c)] for c in cols],
                               [[bias_ref[hp]] for hp in pairs], [valid])
            for c, out in zip(cols, outs):
                o_ref[r0:r0 + ATT_Q_SUB, c] = out

    pl.when(jb == 0)(lambda: body(True))
    pl.when(jb > 0)(lambda: body(False))


def _attn_prompt(q, k, v, bias, n_batch):
    n_rows = q.shape[0]
    per_b = n_rows // n_batch // ATT_Q_BLOCK
    blk = (ATT_Q_BLOCK, A_WIDTH)
    cur = pl.BlockSpec(blk, lambda b, j: (b * per_b + j, 0))
    prev = pl.BlockSpec(blk, lambda b, j: (b * per_b + jnp.maximum(j - 1, 0), 0))
    return pl.pallas_call(
        _attn_prompt_kernel,
        grid=(n_batch, per_b),
        in_specs=[cur, prev, cur, prev, cur, _const_spec(bias)],
        out_specs=cur,
        out_shape=jax.ShapeDtypeStruct((n_rows, A_WIDTH), F32),
        compiler_params=_params(("parallel", "arbitrary")),
        name="attn_prompt",
    )(q, k, k, v, v, bias.array)


def _attn_sample_kernel(q_ref, kn_ref, vn_ref, kc_ref, vc_ref, bias_c_ref, bias_n_ref, o_ref):
    scale = A_HEAD_DIM ** -0.5
    pairs = range(A_HEADS // 2)
    cols = [slice(hp * LANES, (hp + 1) * LANES) for hp in pairs]
    outs = _softmax_pv([q_ref[:, c] * scale for c in cols],
                       [[kc_ref[c, :].astype(BF16), kn_ref[:, c].astype(BF16)] for c in cols],
                       [[vc_ref[c, :].astype(BF16), vn_ref[:, c].astype(BF16)] for c in cols],
                       [[bias_c_ref[hp], bias_n_ref[hp]] for hp in pairs], [None, None], [True, False])
    for c, out in zip(cols, outs):
        o_ref[:, c] = out


def _attn_sample(q, k, v, k_cache, v_cache, layer, bias_c, bias_n, n_batch):
    n_rows = q.shape[0]
    s_len = n_rows // n_batch
    n_cache = k_cache.shape[3]
    new = pl.BlockSpec((s_len, A_WIDTH), lambda b: (b, 0))
    cache = pl.BlockSpec((None, None, A_WIDTH, n_cache), lambda b: (layer, b, 0, 0))
    return pl.pallas_call(
        _attn_sample_kernel,
        grid=(n_batch,),
        in_specs=[new, new, new, cache, cache, _const_spec(bias_c), _const_spec(bias_n)],
        out_specs=new,
        out_shape=jax.ShapeDtypeStruct((n_rows, A_WIDTH), F32),
        compiler_params=_params(("parallel",)),
        name="attn_sample",
    )(q, k, v, k_cache, v_cache, bias_c.array, bias_n.array)


def _rel_bias_blocks(table, n_q, n_k, q_offset, banded):
    period = n_q + n_k - 1
    rel = q_offset - ((np.arange(period) + n_q - 1) % period - (n_q - 1))
    vec = jnp.take(table.astype(F32), jnp.asarray(np.clip(rel, -MAX_REL, MAX_REL) + MAX_REL), axis=1)
    n_heads = table.shape[0]
    bias = jnp.tile(vec, (1, n_q))[:, :n_q * (period - 1)].reshape(n_heads, n_q, period - 1)[:, :, :n_k]
    if banded:
        i = np.arange(n_q)[:, None]
        jj = np.arange(n_k)[None, :] - (i // CHUNK) * CHUNK
        inband = (jj >= 0) & (jj < A_WINDOW + CHUNK)
        bias = jnp.where(jnp.asarray(inband), bias, NEG_INF)
    return bias.reshape(n_heads // 2, 2 * n_q, n_k)


def _delta_kernel(qkvb_ref, gate_ref, ba_ref, cst_ref, s0_ref, cw_ref, cb_ref, alog_ref, dtb_ref, nw_ref,
                  o_ref, cout_ref, sout_ref, xbuf, s_ref, ob_ref, *, t_valid, n_rows):
    j = pl.program_id(1)
    n_blk = n_rows // PAIR
    hd = B_HEAD_DIM
    low = lax.broadcasted_iota(jnp.int32, (PAIR, LANES), 1) < hd
    low_s = lax.broadcasted_iota(jnp.int32, (hd, LANES), 1) < hd
    zeros_s = jnp.zeros((hd, LANES), F32)

    @pl.when(j == 0)
    def _():
        xbuf[0:SUBLANES, :] = cst_ref[0]
        for h in range(B_HEADS):
            s0 = s0_ref[0, h]
            s_ref[h] = jnp.concatenate([s0, jnp.zeros_like(s0)] if h % 2 == 0 else [jnp.zeros_like(s0), s0], axis=1)

    xbuf[SUBLANES:SUBLANES + n_rows, :] = qkvb_ref[...]
    t_last = n_rows - PAIR + t_valid
    cout_ref[0] = xbuf[t_last:t_last + SUBLANES, :]

    hrow = lax.broadcasted_iota(jnp.int32, (B_WIDTH, B_WIDTH), 0)
    hcol = lax.broadcasted_iota(jnp.int32, (B_WIDTH, B_WIDTH), 1)
    head_ones = jnp.where(_blk(hrow, B_HEAD_DIM) == _blk(hcol, B_HEAD_DIM), 1.0, 0.0).astype(BF16)

    def l2n(t):
        return t * lax.rsqrt(_dot_f32_by_exact(t * t, head_ones) + RMS_EPS)

    row = lax.broadcasted_iota(jnp.int32, (PAIR, PAIR), 0)
    col = lax.broadcasted_iota(jnp.int32, (PAIR, PAIR), 1)
    same = _blk(row, CHUNK) == _blk(col, CHUNK)
    tril = same & (row >= col)
    strict = same & (row > col)
    first = row < CHUNK
    ltri = jnp.where(tril, 1.0, 0.0).astype(BF16)
    n_pairs = B_HEADS // 2
    chunks = range(PAIR // CHUNK)

    def pair_lanes(t, p):
        return jnp.where(low, t[:, (2 * p) * LANES:(2 * p + 1) * LANES], t[:, (2 * p + 1) * LANES:(2 * p + 2) * LANES])

    def own(h, t, other):
        return jnp.where(low, t, other) if h % 2 == 0 else jnp.where(low, other, t)

    prepared, solved = {}, {}

    def prepare(n):
        r0 = n * PAIR
        conv = cb_ref[...]
        for w in range(CONV_W):
            start = SUBLANES - (CONV_W - 1) + w + r0
            conv = conv + xbuf[start:start + PAIR, :] * cw_ref[w:w + 1, :]
        conv = _silu(conv)
        yield
        qn = l2n(conv[:, 0:B_WIDTH]) * (B_HEAD_DIM ** -0.5)
        yield
        kn = l2n(conv[:, B_WIDTH:2 * B_WIDTH])
        vv = conv[:, 2 * B_WIDTH:3 * B_WIDTH]
        yield
        ba = ba_ref[r0:r0 + PAIR, :]
        spread = lambda c: jnp.broadcast_to(ba[:, c:c + 1], (PAIR, LANES))
        beta = jax.nn.sigmoid(jnp.concatenate(
            [jnp.where(low, spread(2 * p), spread(2 * p + 1)) for p in range(n_pairs)], axis=1))
        a_in = jnp.concatenate([spread(B_HEADS + h) for h in range(B_HEADS)], axis=1)
        g = -jnp.exp(alog_ref[...]) * _softplus(a_in + dtb_ref[...])
        if t_valid < PAIR and n == n_blk - 1:
            def live(t):
                return jnp.where(lax.broadcasted_iota(jnp.int32, t.shape, 0) < t_valid, t, 0.0)
            kn, vv, beta, g = live(kn), live(vv), live(beta), live(g)
        yield
        gc = _dot_exact_by_f32(ltri, g)
        g_last = [gc[(c + 1) * CHUNK - 1:(c + 1) * CHUNK, :] for c in chunks]
        gl = jnp.concatenate([jnp.broadcast_to(t, (CHUNK, t.shape[1])) for t in g_last], axis=0)
        egc = jnp.exp(gc)
        erest = jnp.exp(gl - gc)
        d = {"egl": [jnp.exp(t) for t in g_last], "k": [], "q": [], "kb": [], "vb": [], "kbg_r": [], "qd_r": [],
             "kd": [], "decay": [], "decay_strict": []}
        yield
        for p in range(n_pairs):
            ls = slice(p * LANES, (p + 1) * LANES)
            k, q, v, b = kn[:, ls], qn[:, ls], vv[:, ls], beta[:, ls]
            eg = pair_lanes(egc, p)
            d["k"].append(k)
            d["q"].append(q)
            d["kb"].append(k * b)
            d["vb"].append(v * b)
            d["kbg_r"].append(pltpu.roll(k * b * eg, hd, axis=1))
            d["qd_r"].append(pltpu.roll(q * eg, hd, axis=1))
            d["kd"].append(k * pair_lanes(erest, p))
            yield
        for h in range(B_HEADS):
            gcol = gc[:, h * LANES:(h + 1) * LANES]
            e = jnp.exp(gcol - gcol.T)
            d["decay"].append(jnp.where(tril, e, 0.0))
            d["decay_strict"].append(jnp.where(strict, e, 0.0))
            yield
        prepared[n] = d
    heads = range(B_HEADS)

    def solve(n):
        d = prepared[n]
        a_low = [_bdot_nt(own(h, d["kb"][h // 2], 0.0), d["k"][h // 2]) * d["decay_strict"][h] for h in heads]
        yield
        base = 8
        diag = _blk(row, base) == _blk(col, base)
        eye = jnp.where(row == col, 1.0, 0.0)
        a0 = [jnp.where(diag, a, 0.0) for a in a_low]
        xs = [eye - a for a in a0]
        ps = [_bdot(a, a) for a in a0]
        yield
        xs = [x + _bdot(x, p) for x, p in zip(xs, ps)]
        ps = [_bdot(p, p) for p in ps]
        yield
        xs = [x + _bdot(x, p) for x, p in zip(xs, ps)]
        yield
        bs = base
        while bs < CHUNK:
            off = (_blk(row, 2 * bs) == _blk(col, 2 * bs)) & (_blk(row, bs) != _blk(col, bs))
            ts = [_bdot(jnp.where(off, a, 0.0), x) for a, x in zip(a_low, xs)]
            yield
            xs = [x - _bdot(x, t) for x, t in zip(xs, ts)]
            yield
            bs *= 2
        uws = [_bdot(x, own(h, d["vb"][h // 2], d["kbg_r"][h // 2])) for h, x in zip(heads, xs)]
        yield
        qus = []
        for h in heads:
            qk = _bdot_nt(own(h, d["q"][h // 2], 0.0), d["k"][h // 2]) * d["decay"][h]
            qus.append(_bdot(qk, uws[h]))
        yield
        kus = []
        for p in range(n_pairs):
            rhs = jnp.concatenate([jnp.where(first if c == 0 else ~first, uws[2 * p + e], 0.0)
                                   for e in range(2) for c in chunks], axis=1)
            kus.append(_bdot(d["kd"][p].T, rhs))
            yield
        solved[n] = (qus, kus)

    states = [s_ref[h] for h in heads]

    def advance(n):
        d = prepared[n]
        qus, kus = solved[n]
        for c in chunks:
            outs = []
            for h in heads:
                p, e = divmod(h, 2)
                ku = kus[p][e * hd:(e + 1) * hd, (2 * e + c) * LANES:(2 * e + c + 1) * LANES]
                qu = qus[h][c * CHUNK:(c + 1) * CHUNK, :]
                kq = jnp.concatenate([ku, d["qd_r"][p][c * CHUNK:(c + 1) * CHUNK] - qu], axis=0)
                s_ext = jnp.concatenate([zeros_s, states[h]] if e == 0 else [states[h], zeros_s], axis=0)
                r = _bdot(kq, s_ext)
                gt = d["egl"][c][:, h * LANES:(h + 1) * LANES]
                new = gt * states[h] - r[:hd] + ku
                states[h] = jnp.where(low_s, new, 0.0) if e == 0 else jnp.where(low_s, 0.0, new)
                outs.append(r[hd:] + qu)
            r0 = n * PAIR + c * CHUNK
            for p in range(n_pairs):
                ob_ref[r0:r0 + CHUNK, p * LANES:(p + 1) * LANES] = jnp.where(low_s, outs[2 * p], outs[2 * p + 1])
            yield
        rs = slice(n * PAIR, (n + 1) * PAIR)
        ob = ob_ref[rs, :]
        ms = _dot_f32_by_exact(ob * ob, head_ones) * (1.0 / hd)
        o_ref[rs, :] = ob * lax.rsqrt(ms + RMS_EPS) * nw_ref[...] * _silu(gate_ref[rs, :])
        yield

    def advance_all(blocks):
        for n in blocks:
            yield from advance(n)

    group = 4 if n_blk % 4 == 0 else 1
    n_groups = n_blk // group
    for slot in range(n_groups + 2):
        active = []
        if 0 <= slot - 2 < n_groups:
            active.append(advance_all(range((slot - 2) * group, (slot - 1) * group)))
        if 0 <= slot - 1 < n_groups:
            active += [solve(n) for n in range((slot - 1) * group, slot * group)]
        if slot < n_groups:
            active += [prepare(n) for n in range(slot * group, (slot + 1) * group)]
        while active:
            for gen in list(active):
                if next(gen, "done") == "done":
                    active.remove(gen)

    for h in heads:
        s_ref[h] = states[h]
    xbuf[0:SUBLANES, :] = xbuf[n_rows:n_rows + SUBLANES, :]

    @pl.when(j == pl.num_programs(1) - 1)
    def _():
        for h in range(B_HEADS):
            sout_ref[0, h] = s_ref[h][:, (h % 2) * hd:(h % 2 + 1) * hd]


def _delta(qkvb, gate, ba, conv_state, s0, lw, n_batch, t_valid, rows_per_step):
    n_rows = qkvb.shape[0]
    per_b = n_rows // n_batch // rows_per_step

    def rows(width):
        return pl.BlockSpec((rows_per_step, width), lambda b, j: (b * per_b + j, 0))

    cst = pl.BlockSpec((1, SUBLANES, 3 * B_WIDTH), lambda b, j: (b, 0, 0))
    st = pl.BlockSpec((1, B_HEADS, B_HEAD_DIM, B_HEAD_DIM), lambda b, j: (b, 0, 0, 0))
    consts = [lw["conv_w"], lw["conv_b"], lw["a_log"], lw["dt_bias"], lw["norm_w"]]
    return pl.pallas_call(
        functools.partial(_delta_kernel, t_valid=t_valid, n_rows=rows_per_step),
        grid=(n_batch, per_b),
        in_specs=[rows(3 * B_WIDTH), rows(B_WIDTH), rows(LANES), cst, st]
        + [_const_spec(c) for c in consts],
        out_specs=[rows(B_WIDTH), cst, st],
        out_shape=[jax.ShapeDtypeStruct((n_rows, B_WIDTH), F32),
                   jax.ShapeDtypeStruct((n_batch, SUBLANES, 3 * B_WIDTH), F32),
                   jax.ShapeDtypeStruct((n_batch, B_HEADS, B_HEAD_DIM, B_HEAD_DIM), F32)],
        scratch_shapes=[pltpu.VMEM((SUBLANES + rows_per_step, 3 * B_WIDTH), F32),
                        pltpu.VMEM((B_HEADS, B_HEAD_DIM, LANES), F32),
                        pltpu.VMEM((rows_per_step, B_WIDTH), F32)],
        compiler_params=_params(("parallel", "arbitrary")),
        name="delta",
    )(qkvb, gate, ba, conv_state, s0, *[c.array for c in consts])


def _s5_kernel(u_ref, h0r_ref, h0i_ref, lr_ref, li_ref, bre_ref, bim_ref, cre_ref, cim_ref, d_ref, gw_ref, gb_ref,
               o_ref, hro_ref, hio_ref, xr, xi, hs, *, t_block):
    tb = pl.program_id(1)

    @pl.when(tb == 0)
    def _():
        hs[0] = h0r_ref[0]
        hs[1] = h0i_ref[0]

    u = jnp.swapaxes(u_ref[0], 0, 1).reshape(t_block * S5_BATCH, C_WIDTH)
    ub = u.astype(BF16)
    xr[...] = jnp.dot(ub, bre_ref[...], preferred_element_type=F32)
    xi[...] = jnp.dot(ub, bim_ref[...], preferred_element_type=F32)
    lam_r = jnp.broadcast_to(lr_ref[...], (S5_BATCH, C_LANES))
    lam_i = jnp.broadcast_to(li_ref[...], (S5_BATCH, C_LANES))

    def step(t, carry):
        h_r, h_i = carry
        rows = pl.ds(pl.multiple_of(t * S5_BATCH, S5_BATCH), S5_BATCH)
        n_r = lam_r * h_r - lam_i * h_i + xr[rows, :]
        n_i = lam_r * h_i + lam_i * h_r + xi[rows, :]
        xr[rows, :] = n_r
        xi[rows, :] = n_i
        return n_r, n_i

    h_r, h_i = lax.fori_loop(0, t_block, step, (hs[0], hs[1]), unroll=4)
    hs[0] = h_r
    hs[1] = h_i
    hro_ref[0] = h_r
    hio_ref[0] = h_i
    y = (jnp.dot(xr[...].astype(BF16), cre_ref[...], preferred_element_type=F32)
         - jnp.dot(xi[...].astype(BF16), cim_ref[...], preferred_element_type=F32) + d_ref[...] * u)
    z = _gelu_tanh(y)
    out = z * jax.nn.sigmoid(jnp.dot(z.astype(BF16), gw_ref[...], preferred_element_type=F32) + gb_ref[...])
    o_ref[0] = jnp.swapaxes(out.reshape(t_block, S5_BATCH, C_WIDTH), 0, 1)


def _s5(u, h0_re, h0_im, lw, t_block):
    n_groups, _, t_len, _ = u.shape
    rows_blk = t_block * S5_BATCH
    n_tb = t_len // t_block
    u_spec = pl.BlockSpec((1, S5_BATCH, t_block, C_WIDTH), lambda g, t: (g, 0, t, 0))
    h_spec = pl.BlockSpec((1, S5_BATCH, C_LANES), lambda g, t: (g, 0, 0))
    consts = [lw["lam_re"], lw["lam_im"], lw["b_re"], lw["b_im"], lw["c_re"], lw["c_im"], lw["c_d"],
              lw["glu_w"], lw["glu_b"]]
    return pl.pallas_call(
        functools.partial(_s5_kernel, t_block=t_block),
        grid=(n_groups, n_tb),
        in_specs=[u_spec, h_spec, h_spec] + [_const_spec(c) for c in consts],
        out_specs=[u_spec, h_spec, h_spec],
        out_shape=[jax.ShapeDtypeStruct(u.shape, F32),
                   jax.ShapeDtypeStruct(h0_re.shape, F32), jax.ShapeDtypeStruct(h0_im.shape, F32)],
        scratch_shapes=[pltpu.VMEM((rows_blk, C_LANES), F32), pltpu.VMEM((rows_blk, C_LANES), F32),
                        pltpu.VMEM((2, S5_BATCH, C_LANES), F32)],
        compiler_params=_params(("parallel", "arbitrary")),
        name="s5",
    )(u, h0_re, h0_im, *[c.array for c in consts])


def _post_kernel(x_ref, a_ref, b_ref, c_ref, wo_ref, g1_ref, b1_ref, wu_ref, bu_ref, wd_ref, g2_ref, b2_ref, o_ref):
    mix = (jnp.dot(a_ref[...].astype(BF16), wo_ref[0:A_WIDTH, :], preferred_element_type=F32)
           + jnp.dot(b_ref[...].astype(BF16), wo_ref[A_WIDTH:A_WIDTH + B_WIDTH, :], preferred_element_type=F32)
           + jnp.dot(c_ref[...].astype(BF16), wo_ref[A_WIDTH + B_WIDTH:, :], preferred_element_type=F32))
    x1 = _layer_norm(DN_ALPHA * x_ref[...] + mix, g1_ref[...], b1_ref[...])
    x1b = x1.astype(BF16)
    acc = jnp.zeros(x1.shape, F32)
    for s in range(D_FF // FF_SLAB):
        cs = slice(s * FF_SLAB, (s + 1) * FF_SLAB)
        hid = jnp.dot(x1b, wu_ref[:, cs], preferred_element_type=F32) + bu_ref[:, cs]
        hid = jnp.square(jnp.maximum(hid, 0.0))
        acc = acc + jnp.dot(hid.astype(BF16), wd_ref[cs, :], preferred_element_type=F32)
    o_ref[...] = _layer_norm(DN_ALPHA * x1 + acc, g2_ref[...], b2_ref[...])


def _post(x2d, out_a, out_b, out_c, lw):
    n_rows = x2d.shape[0]
    tm = min(ROW_TILE, n_rows)

    def rows(width):
        return pl.BlockSpec((tm, width), lambda i: (i, 0))

    c_spec = rows(C_WIDTH)
    consts = [lw["w_out"], lw["ln1_g"], lw["ln1_b"], lw["w_up"], lw["b_up"], lw["w_down"], lw["ln2_g"], lw["ln2_b"]]
    return pl.pallas_call(
        _post_kernel,
        grid=(n_rows // tm,),
        in_specs=[rows(D_MODEL), rows(A_WIDTH), rows(B_WIDTH), c_spec] + [_const_spec(c) for c in consts],
        out_specs=rows(D_MODEL),
        out_shape=jax.ShapeDtypeStruct((n_rows, D_MODEL), F32),
        compiler_params=_params(("parallel",)),
        name="post",
    )(x2d, out_a, out_b, out_c, *[c.array for c in consts])


def _stacked_weights(w_in, a_rel_bias, b_conv_w, b_conv_b, b_a_log, b_dt_bias, b_norm_w, c_a_re, c_a_im, c_log_dt,
                     c_b_re, c_b_im, c_c_re, c_c_im, c_d, c_glu_w, c_glu_b, w_out, ln1_g, ln1_b, w_up, b_up, w_down,
                     ln2_g, ln2_b, s_len, n_cache):
    n_layers = w_in.shape[0]
    wi = w_in
    o_q, o_k, o_v = 0, A_WIDTH, 2 * A_WIDTH
    o_qkvb = 3 * A_WIDTH
    o_beta = o_qkvb + 3 * B_WIDTH
    o_a = o_beta + B_HEADS
    o_gate = o_a + B_HEADS
    o_uc = o_gate + B_WIDTH
    per_group = lambda cols: jnp.repeat(cols, LANES, axis=-1)
    assert o_a == o_beta + B_HEADS
    w_proj = jnp.concatenate([
        wi[..., o_q:o_q + A_WIDTH], wi[..., o_k:o_k + A_WIDTH], wi[..., o_v:o_v + A_WIDTH],
        wi[..., o_qkvb:o_qkvb + 3 * B_WIDTH], wi[..., o_gate:o_gate + B_WIDTH], wi[..., o_uc:o_uc + C_WIDTH],
        wi[..., o_beta:o_beta + 2 * B_HEADS],
        jnp.zeros((n_layers, D_MODEL, LANES - 2 * B_HEADS), wi.dtype)], axis=-1).astype(BF16)

    def bias_blocks(n_q, n_k, q_offset, banded):
        flat = _rel_bias_blocks(a_rel_bias.reshape(n_layers * A_HEADS, -1), n_q, n_k, q_offset, banded)
        return flat.reshape(n_layers, A_HEADS // 2, 2 * n_q, n_k)

    bias_sample = bias_blocks(s_len, n_cache + s_len, n_cache, banded=False)

    a_re, a_im = c_a_re.astype(F32), c_a_im.astype(F32)
    dt = jnp.exp(c_log_dt.astype(F32))[..., None]
    mag = jnp.exp(dt * a_re)
    lam_re, lam_im = mag * jnp.cos(dt * a_im), mag * jnp.sin(dt * a_im)
    den = a_re * a_re + a_im * a_im
    coef_re = ((lam_re - 1.0) * a_re + lam_im * a_im) / den
    coef_im = (lam_im * a_re - (lam_re - 1.0) * a_im) / den
    bre, bim = c_b_re.astype(F32), c_b_im.astype(F32)
    bb_re = coef_re[..., None] * bre - coef_im[..., None] * bim
    bb_im = coef_re[..., None] * bim + coef_im[..., None] * bre
    eye = jnp.eye(C_GROUPS, dtype=F32)
    in_bd = lambda t: jnp.einsum("lgph,gk->lghkp", t, eye).reshape(n_layers, C_WIDTH, C_LANES).astype(BF16)
    out_bd = lambda t: jnp.einsum("lghp,gk->lgpkh", t.astype(F32), eye).reshape(n_layers, C_LANES, C_WIDTH).astype(BF16)
    row = lambda t: t.astype(F32).reshape(n_layers, 1, -1)

    return {
        "w_proj": w_proj,
        "bias_prompt": bias_blocks(ATT_Q_SUB, ATT_K_SUB, A_WINDOW, banded=True),
        "bias_cache": bias_sample[..., :n_cache], "bias_new": bias_sample[..., n_cache:],
        "conv_w": b_conv_w.astype(F32), "conv_b": row(b_conv_b),
        "a_log": row(per_group(b_a_log)), "dt_bias": row(per_group(b_dt_bias)),
        "norm_w": row(jnp.tile(b_norm_w, (1, B_HEADS))),
        "lam_re": row(lam_re), "lam_im": row(lam_im), "b_re": in_bd(bb_re), "b_im": in_bd(bb_im),
        "c_re": out_bd(c_c_re), "c_im": out_bd(c_c_im), "c_d": row(c_d),
        "glu_w": c_glu_w.astype(BF16), "glu_b": row(c_glu_b),
        "w_out": w_out.astype(BF16), "ln1_g": row(ln1_g), "ln1_b": row(ln1_b),
        "w_up": w_up.astype(BF16), "b_up": row(b_up), "w_down": w_down.astype(BF16),
        "ln2_g": row(ln2_g), "ln2_b": row(ln2_b),
    }


def _pad_conv_state(conv_buf):
    return jnp.pad(conv_buf.astype(F32), ((0, 0), (SUBLANES - (CONV_W - 1), 0), (0, 0)))


def _prompt_layer(x2d, lw, n_batch):
    t_len = x2d.shape[0] // n_batch
    q, k, v, qkvb, gate, uc, ba, k_t, v_t = _proj(x2d, lw["w_proj"], n_batch, prompt=True)
    out_a = _attn_prompt(q, k, v, lw["bias_prompt"], n_batch)
    out_b, conv_o, s_new = _delta(
        qkvb, gate, ba, jnp.zeros((n_batch, SUBLANES, 3 * B_WIDTH), F32),
        jnp.zeros((n_batch, B_HEADS, B_HEAD_DIM, B_HEAD_DIM), F32), lw, n_batch, t_valid=PAIR, rows_per_step=DELTA_ROWS)
    zeros_h = jnp.zeros((1, S5_BATCH, C_LANES), F32)
    out_c, h_re, h_im = _s5(uc.reshape(1, n_batch, t_len, C_WIDTH), zeros_h, zeros_h, lw, S5_TIME_BLOCK)
    y = _post(x2d, out_a, out_b, out_c.reshape(n_batch * t_len, C_WIDTH), lw)
    heads = lambda t: t.reshape(n_batch, A_HEADS, A_HEAD_DIM, A_WINDOW).transpose(0, 3, 1, 2)
    state = lambda t: t.reshape(n_batch, C_GROUPS, C_STATE)
    return y, (heads(k_t), heads(v_t), conv_o[:, -(CONV_W - 1):], s_new, state(h_re), state(h_im))


def _sample_layer(x2d, lw, n_batch, layer, k_cache, v_cache, conv_buf, s0, h0_re, h0_im):
    s_len = x2d.shape[0] // n_batch
    n_groups = n_batch // S5_BATCH
    q, k, v, qkvb, gate, uc, ba = _proj(x2d, lw["w_proj"], n_batch, prompt=False)
    out_a = _attn_sample(q, k, v, k_cache, v_cache, layer, lw["bias_cache"], lw["bias_new"], n_batch)

    def pad_rows(t):
        t = t.reshape(n_batch, s_len, -1)
        return jnp.pad(t, ((0, 0), (0, PAIR - s_len), (0, 0))).reshape(n_batch * PAIR, -1)

    out_b, conv_o, s_new = _delta(pad_rows(qkvb), pad_rows(gate), pad_rows(ba),
                                  _pad_conv_state(conv_buf), s0.astype(F32), lw, n_batch, t_valid=s_len, rows_per_step=PAIR)
    out_b = out_b.reshape(n_batch, PAIR, B_WIDTH)[:, :s_len].reshape(n_batch * s_len, B_WIDTH)

    grp = lambda t: t.astype(F32).reshape(n_groups, S5_BATCH, C_LANES)
    out_c, h_re, h_im = _s5(uc.reshape(n_groups, S5_BATCH, s_len, C_WIDTH), grp(h0_re), grp(h0_im), lw, s_len)
    y = _post(x2d, out_a, out_b, out_c.reshape(n_batch * s_len, C_WIDTH), lw)
    heads = lambda t: t.reshape(n_batch, s_len, A_HEADS, A_HEAD_DIM)
    state = lambda t: t.reshape(n_batch, C_GROUPS, C_STATE)
    return y, (heads(k), heads(v), conv_o[:, -(CONV_W - 1):], s_new, state(h_re), state(h_im))


def kernel(x_prompt, x_sample, cache_a_k, cache_a_v, state_b_conv, state_b_ssm, state_c_re, state_c_im, w_in, a_rel_bias, b_conv_w, b_conv_b, b_a_log, b_dt_bias, b_norm_w, c_a_re, c_a_im, c_log_dt, c_b_re, c_b_im, c_c_re, c_c_im, c_d, c_glu_w, c_glu_b, w_out, ln1_g, ln1_b, w_up, b_up, w_down, ln2_g, ln2_b):
    n_p, t_p, _ = x_prompt.shape
    n_s, t_s, _ = x_sample.shape
    n_cache = cache_a_k.shape[2]
    yp = x_prompt.reshape(n_p * t_p, D_MODEL)
    ys = x_sample.reshape(n_s * t_s, D_MODEL)
    feat_major = lambda t: t.astype(F32).transpose(0, 1, 3, 4, 2).reshape(DEPTH, n_s, A_WIDTH, n_cache)
    k_cache, v_cache = feat_major(cache_a_k), feat_major(cache_a_v)
    stacked = _stacked_weights(w_in, a_rel_bias, b_conv_w, b_conv_b, b_a_log, b_dt_bias, b_norm_w, c_a_re, c_a_im,
                               c_log_dt, c_b_re, c_b_im, c_c_re, c_c_im, c_d, c_glu_w, c_glu_b, w_out, ln1_g, ln1_b,
                               w_up, b_up, w_down, ln2_g, ln2_b, t_s, n_cache)
    p_out, s_out = [], []
    for l in range(DEPTH):
        lw = {name: _LayerParam(arr, l) for name, arr in stacked.items()}
        yp, st_p = _prompt_layer(yp, lw, n_p)
        ys, st_s = _sample_layer(ys, lw, n_s, l, k_cache, v_cache, state_b_conv[l], state_b_ssm[l],
                                 state_c_re[l], state_c_im[l])
        p_out.append(st_p)
        s_out.append(st_s)
    stack = lambda outs, i: jnp.stack([o[i] for o in outs])
    return (yp.reshape(n_p, t_p, D_MODEL), ys.reshape(n_s, t_s, D_MODEL),
            *[stack(p_out, i) for i in range(6)], *[stack(s_out, i) for i in range(6)])
```
